```python
import math
import jax
import jax.numpy as jnp
from jax import lax
import numpy as np

D_MODEL = 1024
BATCH = 2
SEQ = 8192
DEPTH = 2

CHUNK = 64
EPS = 1e-6
PLE_DIM = 256
FFN_DIM = 2816
N_BRANCH = 4
BRANCH_WIDTH = 512
S5_GROUP_CH = 16
S5_GROUPS = BRANCH_WIDTH // S5_GROUP_CH
S5_STATE = 64
LRU_HEADS = 8
LRU_HEAD_DIM = BRANCH_WIDTH // LRU_HEADS
LRU_CONV = 4
LRU_C = 8.0
M2_HEAD_DIM = 64
M2_HEADS = BRANCH_WIDTH // M2_HEAD_DIM
M2_GROUPS = 2
M2_STATE = 128
M2_CONV = 4
M2_CONV_DIM = BRANCH_WIDTH + 2 * M2_GROUPS * M2_STATE
GDN_HEAD_DIM = 128
GDN_HEADS = BRANCH_WIDTH // GDN_HEAD_DIM
GDN_CONV = 4
IN_SPLITS = (BRANCH_WIDTH, BRANCH_WIDTH, BRANCH_WIDTH, BRANCH_WIDTH, M2_CONV_DIM, M2_HEADS, 3 * BRANCH_WIDTH, BRANCH_WIDTH, GDN_HEADS, GDN_HEADS)
IN_WIDTH = 4 * BRANCH_WIDTH + M2_CONV_DIM + M2_HEADS + 3 * BRANCH_WIDTH + BRANCH_WIDTH + 2 * GDN_HEADS

kernel_name = "hybrid_parallel_ssm_lru_ssd_gdn_trunk"


def _split(t, sizes):
    out, start = [], 0
    for s in sizes:
        out.append(t[..., start:start + s])
        start += s
    return out


def rmsnorm(x, g):
    xf = x.astype(jnp.float32)
    y = xf * lax.rsqrt(jnp.mean(xf * xf, axis=-1, keepdims=True) + EPS)
    return (y * g.astype(jnp.float32)).astype(x.dtype)


def _l2norm(x):
    return x * lax.rsqrt(jnp.sum(x * x, axis=-1, keepdims=True) + EPS)


def swiglu(h, w_in, w_out):
    gate, up = jnp.split(h @ w_in, 2, axis=-1)
    return (jax.nn.silu(gate) * up) @ w_out


def causal_depthwise_conv(x, w, b=None):
    k, c = w.shape
    y = lax.conv_general_dilated(x, w[:, None, :].astype(x.dtype), window_strides=(1,), padding=[(k - 1, 0)], dimension_numbers=('NWC', 'WIO', 'NWC'), feature_group_count=c)
    return y if b is None else y + b.astype(y.dtype)


def _linear_combine(l, r):
    return (l[0] * r[0], r[0] * l[1] + r[1])


def _complex_combine(l, r):
    a1r, a1i, b1r, b1i = l
    a2r, a2i, b2r, b2i = r
    return (a2r * a1r - a2i * a1i, a2r * a1i + a2i * a1r, a2r * b1r - a2i * b1i + b2r, a2r * b1i + a2i * b1r + b2i)


def _decay_matrix(cs):
    n = cs.shape[-1]
    causal = jnp.tril(jnp.ones((n, n), dtype=bool))
    diff = cs[..., :, None] - cs[..., None, :]
    return jnp.where(causal, jnp.exp(jnp.where(causal, diff, 0.0)), 0.0)


def s5_mixer(u, log_step, a_re, a_im, b_re, b_im, c_re, c_im, d_skip, w_glu, b_glu):
    f32 = jnp.float32
    dtype = u.dtype
    bsz, s, _ = u.shape
    ug = u.astype(f32).reshape(bsz, s, S5_GROUPS, S5_GROUP_CH)
    a_re, a_im = a_re.astype(f32), a_im.astype(f32)
    b_re, b_im = b_re.astype(f32), b_im.astype(f32)
    step = jnp.exp(log_step.astype(f32))[:, None]
    mag = jnp.exp(a_re * step)
    ab_re, ab_im = mag * jnp.cos(a_im * step), mag * jnp.sin(a_im * step)
    den = a_re * a_re + a_im * a_im
    num_re = ab_re - 1.0
    f_re = (num_re * a_re + ab_im * a_im) / den
    f_im = (ab_im * a_re - num_re * a_im) / den
    bb_re = f_re[..., None] * b_re - f_im[..., None] * b_im
    bb_im = f_re[..., None] * b_im + f_im[..., None] * b_re
    bu_re = jnp.einsum('bsgc,gpc->bsgp', ug, bb_re)
    bu_im = jnp.einsum('bsgc,gpc->bsgp', ug, bb_im)
    ar = jnp.broadcast_to(ab_re, bu_re.shape)
    ai = jnp.broadcast_to(ab_im, bu_im.shape)
    _, _, h_re, h_im = lax.associative_scan(_complex_combine, (ar, ai, bu_re, bu_im), axis=1)
    y = jnp.einsum('bsgp,gcp->bsgc', h_re, c_re.astype(f32)) - jnp.einsum('bsgp,gcp->bsgc', h_im, c_im.astype(f32)) + d_skip.astype(f32) * ug
    z = jax.nn.gelu(y.reshape(bsz, s, BRANCH_WIDTH))
    return (z * jax.nn.sigmoid(z @ w_glu.astype(f32) + b_glu.astype(f32))).astype(dtype)


def rglru_mixer(xb, gate, conv_w, conv_b, w_r, b_r, w_i, b_i, lam):
    f32 = jnp.float32
    dtype = xb.dtype
    bsz, s, _ = xb.shape
    xc = causal_depthwise_conv(xb, conv_w, conv_b).astype(f32)
    xh = xc.reshape(bsz, s, LRU_HEADS, LRU_HEAD_DIM)
    r = jax.nn.sigmoid(jnp.einsum('bshi,hij->bshj', xh, w_r.astype(f32)).reshape(bsz, s, BRANCH_WIDTH) + b_r.astype(f32))
    i_g = jax.nn.sigmoid(jnp.einsum('bshi,hij->bshj', xh, w_i.astype(f32)).reshape(bsz, s, BRANCH_WIDTH) + b_i.astype(f32))
    log_a = -LRU_C * r * jax.nn.softplus(-lam.astype(f32))
    a = jnp.exp(log_a)
    inp = jnp.sqrt(-jnp.expm1(2.0 * log_a)) * (i_g * xc)
    _, h = lax.associative_scan(_linear_combine, (a, inp), axis=1)
    return (h * jax.nn.gelu(gate.astype(f32))).astype(dtype)


def ssd_chunked(x, a, bm, cm):
    bsz, s, h, pdim = x.shape
    n = bm.shape[-1]
    c = s // CHUNK
    xc = x.reshape(bsz, c, CHUNK, h, pdim)
    bc = bm.reshape(bsz, c, CHUNK, h, n)
    cc = cm.reshape(bsz, c, CHUNK, h, n)
    a_cs = jnp.cumsum(a.reshape(bsz, c, CHUNK, h).transpose(0, 3, 1, 2), axis=-1)
    scores = jnp.einsum('bclhn,bcshn->bhcls', cc, bc) * _decay_matrix(a_cs)
    y_diag = jnp.einsum('bhcls,bcshp->bclhp', scores, xc)
    decay_to_end = jnp.exp(a_cs[..., -1:] - a_cs)
    states = jnp.einsum('bclhn,bhcl,bclhp->bchpn', bc, decay_to_end, xc)
    chunk_decay = jnp.exp(a_cs[..., -1])

    def step(carry, inp):
        st, dec = inp
        return carry * dec[..., None, None] + st, carry

    init = jnp.zeros((bsz, h, pdim, n), x.dtype)
    _, prev = lax.scan(step, init, (jnp.moveaxis(states, 1, 0), jnp.moveaxis(chunk_decay, 2, 0)))
    prev = jnp.moveaxis(prev, 0, 1)
    y_off = jnp.einsum('bclhn,bchpn,bhcl->bclhp', cc, prev, jnp.exp(a_cs))
    return (y_diag + y_off).reshape(bsz, s, h, pdim)


def mamba2_mixer(z, xbc, dt_raw, conv_w, conv_b, dt_bias, a_log, d_skip, norm_g):
    f32 = jnp.float32
    dtype = z.dtype
    bsz, s, _ = z.shape
    xbc = jax.nn.silu(causal_depthwise_conv(xbc, conv_w, conv_b)).astype(f32)
    xs, bm, cm = _split(xbc, (BRANCH_WIDTH, M2_GROUPS * M2_STATE, M2_GROUPS * M2_STATE))
    xh = xs.reshape(bsz, s, M2_HEADS, M2_HEAD_DIM)
    rep = M2_HEADS // M2_GROUPS
    bm = jnp.repeat(bm.reshape(bsz, s, M2_GROUPS, M2_STATE), rep, axis=2)
    cm = jnp.repeat(cm.reshape(bsz, s, M2_GROUPS, M2_STATE), rep, axis=2)
    dt = jax.nn.softplus(dt_raw.astype(f32) + dt_bias.astype(f32))
    a = -jnp.exp(a_log.astype(f32))
    y = ssd_chunked(xh * dt[..., None], dt * a, bm, cm)
    y = y + d_skip.astype(f32)[:, None] * xh
    y = y.reshape(bsz, s, BRANCH_WIDTH) * jax.nn.silu(z.astype(f32))
    return rmsnorm(y, norm_g).astype(dtype)


def chunk_gated_delta_rule(q, k, v, g, beta):
    bsz, s, h, dk = q.shape
    dv = v.shape[-1]
    c = s // CHUNK

    def to_chunks(t):
        return t.reshape(bsz, c, CHUNK, h, -1).transpose(0, 3, 1, 2, 4)

    q, k, v = to_chunks(q), to_chunks(k), to_chunks(v)
    g_cs = jnp.cumsum(g.reshape(bsz, c, CHUNK, h).transpose(0, 3, 1, 2), axis=-1)
    beta = beta.reshape(bsz, c, CHUNK, h).transpose(0, 3, 1, 2)[..., None]
    decay = _decay_matrix(g_cs)
    kb = k * beta
    strict = jnp.tril(jnp.ones((CHUNK, CHUNK), dtype=bool), -1)
    m = jnp.where(strict, jnp.einsum('bhcld,bhcsd->bhcls', kb, k) * decay, 0.0)
    eye = jnp.eye(CHUNK, dtype=m.dtype)
    rhs = jnp.concatenate([v * beta, kb * jnp.exp(g_cs)[..., None]], axis=-1)
    sol = lax.linalg.triangular_solve(m + eye, rhs, left_side=True, lower=True, unit_diagonal=True)
    u, w = sol[..., :dv], sol[..., dv:]
    qk = jnp.einsum('bhcld,bhcsd->bhcls', q, k) * decay
    q_dec = q * jnp.exp(g_cs)[..., None]
    k_dec = k * jnp.exp(g_cs[..., -1:] - g_cs)[..., None]
    chunk_decay = jnp.exp(g_cs[..., -1])

    def step(state, inp):
        q_i, k_i, w_i, u_i, qk_i, dec_i = inp
        v_new = u_i - jnp.einsum('bhld,bhde->bhle', w_i, state)
        o_i = jnp.einsum('bhld,bhde->bhle', q_i, state) + jnp.einsum('bhls,bhse->bhle', qk_i, v_new)
        state = state * dec_i[..., None, None] + jnp.einsum('bhld,bhle->bhde', k_i, v_new)
        return state, o_i

    xs = (jnp.moveaxis(q_dec, 2, 0), jnp.moveaxis(k_dec, 2, 0), jnp.moveaxis(w, 2, 0), jnp.moveaxis(u, 2, 0), jnp.moveaxis(qk, 2, 0), jnp.moveaxis(chunk_decay, 2, 0))
    _, o = lax.scan(step, jnp.zeros((bsz, h, dk, dv), q.dtype), xs)
    return o.transpose(1, 0, 3, 2, 4).reshape(bsz, s, h, dv)


def gated_deltanet_mixer(qkv, gate, beta_raw, a_raw, conv_w, dt_bias, a_log, norm_g):
    f32 = jnp.float32
    dtype = qkv.dtype
    bsz, s, _ = qkv.shape
    qkv = jax.nn.silu(causal_depthwise_conv(qkv, conv_w)).astype(f32)
    q, k, v = _split(qkv, (BRANCH_WIDTH, BRANCH_WIDTH, BRANCH_WIDTH))
    shape = (bsz, s, GDN_HEADS, GDN_HEAD_DIM)
    q = _l2norm(q.reshape(shape)) * (GDN_HEAD_DIM ** -0.5)
    k = _l2norm(k.reshape(shape))
    v = v.reshape(shape)
    beta = jax.nn.sigmoid(beta_raw.astype(f32))
    g = -jnp.exp(a_log.astype(f32)) * jax.nn.softplus(a_raw.astype(f32) + dt_bias.astype(f32))
    o = chunk_gated_delta_rule(q, k, v, g, beta)
    o = rmsnorm(o, norm_g) * jax.nn.silu(gate.astype(f32).reshape(shape))
    return o.reshape(bsz, s, BRANCH_WIDTH).astype(dtype)


def setup_inputs(seed: int = 0) -> dict:
    key = jax.random.key(seed)
    keys = iter(jax.random.split(key, 64))
    f32 = jnp.float32

    def normal(shape, scale):
        return scale * jax.random.normal(next(keys), shape, f32)

    def gain(shape):
        return 1.0 + 0.01 * jax.random.normal(next(keys), shape, f32)

    def uniform(shape, lo, hi):
        return jax.random.uniform(next(keys), shape, f32, lo, hi)

    def dt_bias(shape):
        dt = jnp.exp(uniform(shape, math.log(1e-3), math.log(1e-1)))
        return dt + jnp.log(-jnp.expm1(-dt))

    L, D, W = DEPTH, D_MODEL, BRANCH_WIDTH
    n_idx = jnp.arange(S5_STATE, dtype=f32)
    a_pow = uniform((L, W), 0.9, 0.999) ** (1.0 / LRU_C)
    return {
        'x': normal((BATCH, SEQ, D), 1.0),
        'p': normal((DEPTH, BATCH, SEQ, PLE_DIM), 1.0),
        'ffn1_norm': gain((L, D)),
        'ffn1_w_in': normal((L, D, 2 * FFN_DIM), D ** -0.5),
        'ffn1_w_out': normal((L, FFN_DIM, D), FFN_DIM ** -0.5),
        'mix_norm': gain((L, D)),
        'w_in': normal((L, D, IN_WIDTH), D ** -0.5),
        'w_gate': normal((L, D, N_BRANCH * D), D ** -0.5),
        'b_gate': normal((L, N_BRANCH * D), 0.01),
        's5_log_step': uniform((L, S5_GROUPS), math.log(1e-3), math.log(1e-1)),
        's5_a_re': -0.5 + normal((L, S5_GROUPS, S5_STATE), 0.01),
        's5_a_im': math.pi * n_idx + normal((L, S5_GROUPS, S5_STATE), 0.01),
        's5_b_re': normal((L, S5_GROUPS, S5_STATE, S5_GROUP_CH), (2 * S5_GROUP_CH) ** -0.5),
        's5_b_im': normal((L, S5_GROUPS, S5_STATE, S5_GROUP_CH), (2 * S5_GROUP_CH) ** -0.5),
        's5_c_re': normal((L, S5_GROUPS, S5_GROUP_CH, S5_STATE), (2 * S5_STATE) ** -0.5),
        's5_c_im': normal((L, S5_GROUPS, S5_GROUP_CH, S5_STATE), (2 * S5_STATE) ** -0.5),
        's5_d': normal((L, S5_GROUPS, S5_GROUP_CH), 1.0),
        's5_w_glu': normal((L, W, W), W ** -0.5),
        's5_b_glu': normal((L, W), 0.01),
        'lru_conv_w': normal((L, LRU_CONV, W), LRU_CONV ** -0.5),
        'lru_conv_b': normal((L, W), 0.01),
        'lru_w_r': normal((L, LRU_HEADS, LRU_HEAD_DIM, LRU_HEAD_DIM), LRU_HEAD_DIM ** -0.5),
        'lru_b_r': normal((L, W), 0.01),
        'lru_w_i': normal((L, LRU_HEADS, LRU_HEAD_DIM, LRU_HEAD_DIM), LRU_HEAD_DIM ** -0.5),
        'lru_b_i': normal((L, W), 0.01),
        'lru_lambda': jnp.log(a_pow) - jnp.log1p(-a_pow),
        'm2_conv_w': normal((L, M2_CONV, M2_CONV_DIM), M2_CONV ** -0.5),
        'm2_conv_b': normal((L, M2_CONV_DIM), 0.01),
        'm2_dt_bias': dt_bias((L, M2_HEADS)),
        'm2_a_log': jnp.log(uniform((L, M2_HEADS), 1.0, 16.0)),
        'm2_d': gain((L, M2_HEADS)),
        'm2_norm': gain((L, W)),
        'gdn_conv_w': normal((L, GDN_CONV, 3 * W), GDN_CONV ** -0.5),
        'gdn_dt_bias': dt_bias((L, GDN_HEADS)),
        'gdn_a_log': jnp.log(uniform((L, GDN_HEADS), 1.0, 16.0)),
        'gdn_norm': gain((L, GDN_HEAD_DIM)),
        'w_branch': normal((L, N_BRANCH, W, D), W ** -0.5),
        'w_out': normal((L, D, D), D ** -0.5),
        'ffn2_norm': gain((L, D)),
        'ffn2_w_in': normal((L, D, 2 * FFN_DIM), D ** -0.5),
        'ffn2_w_out': normal((L, FFN_DIM, D), FFN_DIM ** -0.5),
        'ple_norm': gain((L, D)),
        'ple_w_gate': normal((L, D, D), D ** -0.5),
        'ple_w_proj': normal((L, PLE_DIM, D), PLE_DIM ** -0.5),
        'final_norm': gain((D,)),
    }


def reference(x, p, ffn1_norm, ffn1_w_in, ffn1_w_out, mix_norm, w_in, w_gate, b_gate, s5_log_step, s5_a_re, s5_a_im, s5_b_re, s5_b_im, s5_c_re, s5_c_im, s5_d, s5_w_glu, s5_b_glu, lru_conv_w, lru_conv_b, lru_w_r, lru_b_r, lru_w_i, lru_b_i, lru_lambda, m2_conv_w, m2_conv_b, m2_dt_bias, m2_a_log, m2_d, m2_norm, gdn_conv_w, gdn_dt_bias, gdn_a_log, gdn_norm, w_branch, w_out, ffn2_norm, ffn2_w_in, ffn2_w_out, ple_norm, ple_w_gate, ple_w_proj, final_norm):
    bsz, s, d = x.shape
    h = x
    for i in range(DEPTH):
        h = h + 0.5 * swiglu(rmsnorm(h, ffn1_norm[i]), ffn1_w_in[i], ffn1_w_out[i])
        u = rmsnorm(h, mix_norm[i])
        s5_u, lru_x, lru_g, m2_z, m2_xbc, m2_dt, gdn_qkv, gdn_g, gdn_b, gdn_a = _split(u @ w_in[i], IN_SPLITS)
        y_a = s5_mixer(s5_u, s5_log_step[i], s5_a_re[i], s5_a_im[i], s5_b_re[i], s5_b_im[i], s5_c_re[i], s5_c_im[i], s5_d[i], s5_w_glu[i], s5_b_glu[i])
        y_b = rglru_mixer(lru_x, lru_g, lru_conv_w[i], lru_conv_b[i], lru_w_r[i], lru_b_r[i], lru_w_i[i], lru_b_i[i], lru_lambda[i])
        y_c = mamba2_mixer(m2_z, m2_xbc, m2_dt, m2_conv_w[i], m2_conv_b[i], m2_dt_bias[i], m2_a_log[i], m2_d[i], m2_norm[i])
        y_d = gated_deltanet_mixer(gdn_qkv, gdn_g, gdn_b, gdn_a, gdn_conv_w[i], gdn_dt_bias[i], gdn_a_log[i], gdn_norm[i])
        ys = jnp.stack([y_a, y_b, y_c, y_d], axis=2)
        yb = jnp.einsum('bsnc,ncd->bsnd', ys, w_branch[i])
        gates = jax.nn.sigmoid(u @ w_gate[i] + b_gate[i]).reshape(bsz, s, N_BRANCH, d)
        h = h + jnp.einsum('bsnd,bsnd->bsd', gates, yb) @ w_out[i]
        h = h + 0.5 * swiglu(rmsnorm(h, ffn2_norm[i]), ffn2_w_in[i], ffn2_w_out[i])
        h = h + jax.nn.sigmoid(rmsnorm(h, ple_norm[i]) @ ple_w_gate[i]) * (p[i] @ ple_w_proj[i])
    return rmsnorm(h, final_norm)
```

```python
import functools
import math

import jax
import jax.numpy as jnp
from jax import lax
from jax.experimental import pallas as pl
from jax.experimental.pallas import tpu as pltpu

EPS = 1e-6
BF16 = jnp.bfloat16
F32 = jnp.float32

V7X_VMEM_BYTES = 64 * 1024 * 1024
VMEM_LIMIT_BYTES = V7X_VMEM_BYTES - 8 * 1024 * 1024

CHUNK = 64
BRANCH_WIDTH = 512
S5_GROUP_CH = 16
S5_GROUPS = BRANCH_WIDTH // S5_GROUP_CH
LRU_HEADS = 8
LRU_HEAD_DIM = BRANCH_WIDTH // LRU_HEADS
LRU_C = 8.0
M2_HEAD_DIM = 64
M2_HEADS = BRANCH_WIDTH // M2_HEAD_DIM
M2_GROUPS = 2
M2_STATE = 128
M2_CONV_DIM = BRANCH_WIDTH + 2 * M2_GROUPS * M2_STATE
GDN_HEAD_DIM = 128
GDN_HEADS = BRANCH_WIDTH // GDN_HEAD_DIM
IN_SPLITS = (BRANCH_WIDTH, BRANCH_WIDTH, BRANCH_WIDTH, BRANCH_WIDTH, M2_CONV_DIM, M2_HEADS, 3 * BRANCH_WIDTH, BRANCH_WIDTH, GDN_HEADS, GDN_HEADS)


def _bdot(a, b):
    return jnp.dot(a.astype(BF16), b.astype(BF16), preferred_element_type=F32)


def _rms(x, g):
    return x * lax.rsqrt(jnp.mean(x * x, axis=-1, keepdims=True) + EPS) * g


def _const_spec(shape):
    return pl.BlockSpec(shape, lambda *_: (0,) * len(shape))


def _ffn_body(h_ref, g_ref, wg_ref, wu_ref, wo_ref, o_ref, xn_ref, acc_ref):
    j = pl.program_id(1)

    @pl.when(j == 0)
    def _():
        xn_ref[...] = _rms(h_ref[...], g_ref[...]).astype(BF16)
        acc_ref[...] = jnp.zeros_like(acc_ref)

    xn = xn_ref[...]
    gate = jnp.dot(xn, wg_ref[...], preferred_element_type=F32)
    up = jnp.dot(xn, wu_ref[...], preferred_element_type=F32)
    act = (gate * jax.nn.sigmoid(gate) * up).astype(BF16)
    acc_ref[...] += jnp.dot(act, wo_ref[...], preferred_element_type=F32)

    @pl.when(j == pl.num_programs(1) - 1)
    def _():
        o_ref[...] = h_ref[...] + 0.5 * acc_ref[...]


def _ffn(h, norm_g, w_in, w_out, *, tm, tf):
    t, d = h.shape
    f = w_out.shape[0]
    nf = f // tf
    return pl.pallas_call(
        _ffn_body,
        out_shape=jax.ShapeDtypeStruct((t, d), F32),
        grid=(t // tm, nf),
        in_specs=[
            pl.BlockSpec((tm, d), lambda i, j: (i, 0)),
            pl.BlockSpec((1, d), lambda i, j: (0, 0)),
            pl.BlockSpec((d, tf), lambda i, j: (0, j)),
            pl.BlockSpec((d, tf), lambda i, j: (0, j + nf)),
            pl.BlockSpec((tf, d), lambda i, j: (j, 0)),
        ],
        out_specs=pl.BlockSpec((tm, d), lambda i, j: (i, 0)),
        scratch_shapes=[pltpu.VMEM((tm, d), BF16), pltpu.VMEM((tm, d), F32)],
        compiler_params=pltpu.CompilerParams(
            dimension_semantics=("parallel", "arbitrary"),
            vmem_limit_bytes=VMEM_LIMIT_BYTES),
        name="ffn",
    )(h, norm_g.reshape(1, d), w_in, w_in, w_out)


def _merge_body(h_ref, g_ref, ya_ref, yb_ref, yc_ref, yd_ref, wg_ref, bg_ref, wb_ref, wo_ref, o_ref):
    h = h_ref[...]
    d = h.shape[-1]
    u = _rms(h, g_ref[...]).astype(BF16)
    mixed = jnp.zeros(h.shape, F32)
    for n, y_ref in enumerate((ya_ref, yb_ref, yc_ref, yd_ref)):
        gate = jax.nn.sigmoid(
            jnp.dot(u, wg_ref[:, n * d:(n + 1) * d], preferred_element_type=F32) + bg_ref[:, n * d:(n + 1) * d])
        mixed = mixed + gate * jnp.dot(y_ref[...], wb_ref[n], preferred_element_type=F32)
    o_ref[...] = h + jnp.dot(mixed.astype(BF16), wo_ref[...], preferred_element_type=F32)


def _merge(h, norm_g, ys, w_gate, b_gate, w_branch, w_out, *, tm):
    t, d = h.shape
    nb, w, _ = w_branch.shape
    y_spec = pl.BlockSpec((tm, w), lambda i: (i, 0))
    return pl.pallas_call(
        _merge_body,
        out_shape=jax.ShapeDtypeStruct((t, d), F32),
        grid=(t // tm,),
        in_specs=[
            pl.BlockSpec((tm, d), lambda i: (i, 0)),
            _const_spec((1, d)),
            y_spec, y_spec, y_spec, y_spec,
            _const_spec((d, nb * d)),
            _const_spec((1, nb * d)),
            _const_spec((nb, w, d)),
            _const_spec((d, d)),
        ],
        out_specs=pl.BlockSpec((tm, d), lambda i: (i, 0)),
        compiler_params=pltpu.CompilerParams(
            dimension_semantics=("parallel",), vmem_limit_bytes=VMEM_LIMIT_BYTES),
        name="merge",
    )(h, norm_g.reshape(1, d), *ys, w_gate, b_gate.reshape(1, nb * d), w_branch, w_out)


def _ple_body(h_ref, g_ref, p_ref, wg_ref, wp_ref, fg_ref, o_ref, *, final_norm):
    h = h_ref[...]
    gate = jax.nn.sigmoid(_bdot(_rms(h, g_ref[...]), wg_ref[...]))
    out = h + gate * _bdot(p_ref[...], wp_ref[...])
    if final_norm:
        out = _rms(out, fg_ref[...])
    o_ref[...] = out


def _ple(h, norm_g, p, w_gate, w_proj, final_g, *, tm, final_norm):
    t, d = h.shape
    pd = p.shape[-1]
    return pl.pallas_call(
        functools.partial(_ple_body, final_norm=final_norm),
        out_shape=jax.ShapeDtypeStruct((t, d), F32),
        grid=(t // tm,),
        in_specs=[
            pl.BlockSpec((tm, d), lambda i: (i, 0)),
            _const_spec((1, d)),
            pl.BlockSpec((tm, pd), lambda i: (i, 0)),
            _const_spec((d, d)),
            _const_spec((pd, d)),
            _const_spec((1, d)),
        ],
        out_specs=pl.BlockSpec((tm, d), lambda i: (i, 0)),
        compiler_params=pltpu.CompilerParams(
            dimension_semantics=("parallel",), vmem_limit_bytes=VMEM_LIMIT_BYTES),
        name="ple",
    )(h, norm_g.reshape(1, d), p, w_gate, w_proj, final_g.reshape(1, d))


def _split(t, sizes):
    out, start = [], 0
    for s in sizes:
        out.append(t[..., start:start + s])
        start += s
    return out


def _rmsnorm_jax(x, g):
    xf = x.astype(jnp.float32)
    y = xf * lax.rsqrt(jnp.mean(xf * xf, axis=-1, keepdims=True) + EPS)
    return (y * g.astype(jnp.float32)).astype(x.dtype)


def _l2norm(x):
    return x * lax.rsqrt(jnp.sum(x * x, axis=-1, keepdims=True) + EPS)


def _causal_depthwise_conv(x, w, b=None):
    k, c = w.shape
    y = lax.conv_general_dilated(x, w[:, None, :].astype(x.dtype), window_strides=(1,), padding=[(k - 1, 0)], dimension_numbers=('NWC', 'WIO', 'NWC'), feature_group_count=c)
    return y if b is None else y + b.astype(y.dtype)


def _linear_combine(l, r):
    return (l[0] * r[0], r[0] * l[1] + r[1])


def _complex_combine(l, r):
    a1r, a1i, b1r, b1i = l
    a2r, a2i, b2r, b2i = r
    return (a2r * a1r - a2i * a1i, a2r * a1i + a2i * a1r, a2r * b1r - a2i * b1i + b2r, a2r * b1i + a2i * b1r + b2i)


def _decay_matrix(cs):
    n = cs.shape[-1]
    causal = jnp.tril(jnp.ones((n, n), dtype=bool))
    diff = cs[..., :, None] - cs[..., None, :]
    return jnp.where(causal, jnp.exp(jnp.where(causal, diff, 0.0)), 0.0)


def _s5_mixer(u, log_step, a_re, a_im, b_re, b_im, c_re, c_im, d_skip, w_glu, b_glu):
    f32 = jnp.float32
    dtype = u.dtype
    bsz, s, _ = u.shape
    ug = u.astype(f32).reshape(bsz, s, S5_GROUPS, S5_GROUP_CH)
    step = jnp.exp(log_step.astype(f32))[:, None]
    mag = jnp.exp(a_re * step)
    ab_re, ab_im = mag * jnp.cos(a_im * step), mag * jnp.sin(a_im * step)
    den = a_re * a_re + a_im * a_im
    num_re = ab_re - 1.0
    f_re = (num_re * a_re + ab_im * a_im) / den
    f_im = (ab_im * a_re - num_re * a_im) / den
    bb_re = f_re[..., None] * b_re - f_im[..., None] * b_im
    bb_im = f_re[..., None] * b_im + f_im[..., None] * b_re
    bu_re = jnp.einsum('bsgc,gpc->bsgp', ug, bb_re)
    bu_im = jnp.einsum('bsgc,gpc->bsgp', ug, bb_im)
    ar = jnp.broadcast_to(ab_re, bu_re.shape)
    ai = jnp.broadcast_to(ab_im, bu_im.shape)
    _, _, h_re, h_im = lax.associative_scan(_complex_combine, (ar, ai, bu_re, bu_im), axis=1)
    y = jnp.einsum('bsgp,gcp->bsgc', h_re, c_re.astype(f32)) - jnp.einsum('bsgp,gcp->bsgc', h_im, c_im.astype(f32)) + d_skip.astype(f32) * ug
    z = jax.nn.gelu(y.reshape(bsz, s, BRANCH_WIDTH))
    return (z * jax.nn.sigmoid(z @ w_glu.astype(f32) + b_glu.astype(f32))).astype(dtype)


def _rglru_mixer(xb, gate, conv_w, conv_b, w_r, b_r, w_i, b_i, lam):
    f32 = jnp.float32
    dtype = xb.dtype
    bsz, s, _ = xb.shape
    xc = _causal_depthwise_conv(xb, conv_w, conv_b).astype(f32)
    xh = xc.reshape(bsz, s, LRU_HEADS, LRU_HEAD_DIM)
    r = jax.nn.sigmoid(jnp.einsum('bshi,hij->bshj', xh, w_r.astype(f32)).reshape(bsz, s, BRANCH_WIDTH) + b_r.astype(f32))
    i_g = jax.nn.sigmoid(jnp.einsum('bshi,hij->bshj', xh, w_i.astype(f32)).reshape(bsz, s, BRANCH_WIDTH) + b_i.astype(f32))
    log_a = -LRU_C * r * jax.nn.softplus(-lam.astype(f32))
    a = jnp.exp(log_a)
    inp = jnp.sqrt(-jnp.expm1(2.0 * log_a)) * (i_g * xc)
    _, h = lax.associative_scan(_linear_combine, (a, inp), axis=1)
    return (h * jax.nn.gelu(gate.astype(f32))).astype(dtype)


def _ssd_chunked(x, a, bm, cm):
    bsz, s, h, pdim = x.shape
    n = bm.shape[-1]
    c = s // CHUNK
    xc = x.reshape(bsz, c, CHUNK, h, pdim)
    bc = bm.reshape(bsz, c, CHUNK, h, n)
    cc = cm.reshape(bsz, c, CHUNK, h, n)
    a_cs = jnp.cumsum(a.reshape(bsz, c, CHUNK, h).transpose(0, 3, 1, 2), axis=-1)
    scores = jnp.einsum('bclhn,bcshn->bhcls', cc, bc) * _decay_matrix(a_cs)
    y_diag = jnp.einsum('bhcls,bcshp->bclhp', scores, xc)
    decay_to_end = jnp.exp(a_cs[..., -1:] - a_cs)
    states = jnp.einsum('bclhn,bhcl,bclhp->bchpn', bc, decay_to_end, xc)
    chunk_decay = jnp.exp(a_cs[..., -1])

    def step(carry, inp):
        st, dec = inp
        return carry * dec[..., None, None] + st, carry

    init = jnp.zeros((bsz, h, pdim, n), x.dtype)
    _, prev = lax.scan(step, init, (jnp.moveaxis(states, 1, 0), jnp.moveaxis(chunk_decay, 2, 0)))
    prev = jnp.moveaxis(prev, 0, 1)
    y_off = jnp.einsum('bclhn,bchpn,bhcl->bclhp', cc, prev, jnp.exp(a_cs))
    return (y_diag + y_off).reshape(bsz, s, h, pdim)


def _mamba2_mixer(z, xbc, dt_raw, conv_w, conv_b, dt_bias, a_log, d_skip, norm_g):
    f32 = jnp.float32
    dtype = z.dtype
    bsz, s, _ = z.shape
    xbc = jax.nn.silu(_causal_depthwise_conv(xbc, conv_w, conv_b)).astype(f32)
    xs, bm, cm = _split(xbc, (BRANCH_WIDTH, M2_GROUPS * M2_STATE, M2_GROUPS * M2_STATE))
    xh = xs.reshape(bsz, s, M2_HEADS, M2_HEAD_DIM)
    rep = M2_HEADS // M2_GROUPS
    bm = jnp.repeat(bm.reshape(bsz, s, M2_GROUPS, M2_STATE), rep, axis=2)
    cm = jnp.repeat(cm.reshape(bsz, s, M2_GROUPS, M2_STATE), rep, axis=2)
    dt = jax.nn.softplus(dt_raw.astype(f32) + dt_bias.astype(f32))
    a = -jnp.exp(a_log.astype(f32))
    y = _ssd_chunked(xh * dt[..., None], dt * a, bm, cm)
    y = y + d_skip.astype(f32)[:, None] * xh
    y = y.reshape(bsz, s, BRANCH_WIDTH) * jax.nn.silu(z.astype(f32))
    return _rmsnorm_jax(y, norm_g).astype(dtype)


def _chunk_gated_delta_rule(q, k, v, g, beta):
    bsz, s, h, dk = q.shape
    dv = v.shape[-1]
    c = s // CHUNK

    def to_chunks(t):
        return t.reshape(bsz, c, CHUNK, h, -1).transpose(0, 3, 1, 2, 4)

    q, k, v = to_chunks(q), to_chunks(k), to_chunks(v)
    g_cs = jnp.cumsum(g.reshape(bsz, c, CHUNK, h).transpose(0, 3, 1, 2), axis=-1)
    beta = beta.reshape(bsz, c, CHUNK, h).transpose(0, 3, 1, 2)[..., None]
    decay = _decay_matrix(g_cs)
    kb = k * beta
    strict = jnp.tril(jnp.ones((CHUNK, CHUNK), dtype=bool), -1)
    m = jnp.where(strict, jnp.einsum('bhcld,bhcsd->bhcls', kb, k) * decay, 0.0)
    eye = jnp.eye(CHUNK, dtype=m.dtype)
    rhs = jnp.concatenate([v * beta, kb * jnp.exp(g_cs)[..., None]], axis=-1)
    sol = lax.linalg.triangular_solve(m + eye, rhs, left_side=True, lower=True, unit_diagonal=True)
    u, w = sol[..., :dv], sol[..., dv:]
    qk = jnp.einsum('bhcld,bhcsd->bhcls', q, k) * decay
    q_dec = q * jnp.exp(g_cs)[..., None]
    k_dec = k * jnp.exp(g_cs[..., -1:] - g_cs)[..., None]
    chunk_decay = jnp.exp(g_cs[..., -1])

    def step(state, inp):
        q_i, k_i, w_i, u_i, qk_i, dec_i = inp
        v_new = u_i - jnp.einsum('bhld,bhde->bhle', w_i, state)
        o_i = jnp.einsum('bhld,bhde->bhle', q_i, state) + jnp.einsum('bhls,bhse->bhle', qk_i, v_new)
        state = state * dec_i[..., None, None] + jnp.einsum('bhld,bhle->bhde', k_i, v_new)
        return state, o_i

    xs = (jnp.moveaxis(q_dec, 2, 0), jnp.moveaxis(k_dec, 2, 0), jnp.moveaxis(w, 2, 0), jnp.moveaxis(u, 2, 0), jnp.moveaxis(qk, 2, 0), jnp.moveaxis(chunk_decay, 2, 0))
    _, o = lax.scan(step, jnp.zeros((bsz, h, dk, dv), q.dtype), xs)
    return o.transpose(1, 0, 3, 2, 4).reshape(bsz, s, h, dv)


def _gated_deltanet_mixer(qkv, gate, beta_raw, a_raw, conv_w, dt_bias, a_log, norm_g):
    f32 = jnp.float32
    dtype = qkv.dtype
    bsz, s, _ = qkv.shape
    qkv = jax.nn.silu(_causal_depthwise_conv(qkv, conv_w)).astype(f32)
    q, k, v = _split(qkv, (BRANCH_WIDTH, BRANCH_WIDTH, BRANCH_WIDTH))
    shape = (bsz, s, GDN_HEADS, GDN_HEAD_DIM)
    q = _l2norm(q.reshape(shape)) * (GDN_HEAD_DIM ** -0.5)
    k = _l2norm(k.reshape(shape))
    v = v.reshape(shape)
    beta = jax.nn.sigmoid(beta_raw.astype(f32))
    g = -jnp.exp(a_log.astype(f32)) * jax.nn.softplus(a_raw.astype(f32) + dt_bias.astype(f32))
    o = _chunk_gated_delta_rule(q, k, v, g, beta)
    o = _rmsnorm_jax(o, norm_g) * jax.nn.silu(gate.astype(f32).reshape(shape))
    return o.reshape(bsz, s, BRANCH_WIDTH).astype(dtype)


def kernel(x, p, ffn1_norm, ffn1_w_in, ffn1_w_out, mix_norm, w_in, w_gate, b_gate, s5_log_step, s5_a_re, s5_a_im, s5_b_re, s5_b_im, s5_c_re, s5_c_im, s5_d, s5_w_glu, s5_b_glu, lru_conv_w, lru_conv_b, lru_w_r, lru_b_r, lru_w_i, lru_b_i, lru_lambda, m2_conv_w, m2_conv_b, m2_dt_bias, m2_a_log, m2_d, m2_norm, gdn_conv_w, gdn_dt_bias, gdn_a_log, gdn_norm, w_branch, w_out, ffn2_norm, ffn2_w_in, ffn2_w_out, ple_norm, ple_w_gate, ple_w_proj, final_norm):
    bsz, s, d = x.shape
    depth = p.shape[0]
    t = bsz * s
    h = x.reshape(t, d)
    tm_ffn, tf_ffn, tm = 512, ffn1_w_out.shape[1] // 2, 512
    for i in range(depth):
        h = _ffn(h, ffn1_norm[i], ffn1_w_in[i].astype(BF16), ffn1_w_out[i].astype(BF16), tm=tm_ffn, tf=tf_ffn)
        u = _rmsnorm_jax(h, mix_norm[i]).reshape(bsz, s, d)
        s5_u, lru_x, lru_g, m2_z, m2_xbc, m2_dt, gdn_qkv, gdn_g, gdn_b, gdn_a = _split(u @ w_in[i], IN_SPLITS)
        y_a = _s5_mixer(s5_u, s5_log_step[i], s5_a_re[i], s5_a_im[i], s5_b_re[i], s5_b_im[i], s5_c_re[i], s5_c_im[i], s5_d[i], s5_w_glu[i], s5_b_glu[i])
        y_b = _rglru_mixer(lru_x, lru_g, lru_conv_w[i], lru_conv_b[i], lru_w_r[i], lru_b_r[i], lru_w_i[i], lru_b_i[i], lru_lambda[i])
        y_c = _mamba2_mixer(m2_z, m2_xbc, m2_dt, m2_conv_w[i], m2_conv_b[i], m2_dt_bias[i], m2_a_log[i], m2_d[i], m2_norm[i])
        y_d = _gated_deltanet_mixer(gdn_qkv, gdn_g, gdn_b, gdn_a, gdn_conv_w[i], gdn_dt_bias[i], gdn_a_log[i], gdn_norm[i])
        ys = [y.reshape(t, BRANCH_WIDTH).astype(BF16) for y in (y_a, y_b, y_c, y_d)]
        h = _merge(h, mix_norm[i], ys, w_gate[i].astype(BF16), b_gate[i], w_branch[i].astype(BF16), w_out[i].astype(BF16), tm=tm)
        h = _ffn(h, ffn2_norm[i], ffn2_w_in[i].astype(BF16), ffn2_w_out[i].astype(BF16), tm=tm_ffn, tf=tf_ffn)
        h = _ple(h, ple_norm[i], p[i].reshape(t, -1), ple_w_gate[i].astype(BF16), ple_w_proj[i].astype(BF16), final_norm,
                 tm=tm, final_norm=(i == depth - 1))
    return h.reshape(bsz, s, d)
```

```python
import functools

import jax
import jax.numpy as jnp
from jax import lax
from jax.experimental import pallas as pl
from jax.experimental.pallas import tpu as pltpu

EPS = 1e-6
BF16 = jnp.bfloat16
F32 = jnp.float32

V7X_VMEM_BYTES = 64 * 1024 * 1024
VMEM_LIMIT_BYTES = V7X_VMEM_BYTES - 8 * 1024 * 1024
SUBLANES = 8
LANES = 128

BRANCH_WIDTH = 512
S5_GROUP_CH = 16
S5_STATE = 64
S5_BLOCK_STATES = (LANES // S5_GROUP_CH) * S5_STATE
LRU_C = 8.0
M2_HEAD_DIM = 64
M2_HEADS = BRANCH_WIDTH // M2_HEAD_DIM
M2_GROUPS = 2
M2_STATE = 128
M2_CONV_DIM = BRANCH_WIDTH + 2 * M2_GROUPS * M2_STATE
M2_CHUNK = 128
GDN_HEAD_DIM = 128
GDN_HEADS = BRANCH_WIDTH // GDN_HEAD_DIM
GDN_CHUNK = 64
IN_SPLITS = (BRANCH_WIDTH, BRANCH_WIDTH, BRANCH_WIDTH, BRANCH_WIDTH, M2_CONV_DIM, M2_HEADS, 3 * BRANCH_WIDTH, BRANCH_WIDTH, GDN_HEADS, GDN_HEADS)


def _rms(x, g):
    return x * lax.rsqrt(jnp.mean(x * x, axis=-1, keepdims=True) + EPS) * g


def _silu(x):
    return x * jax.nn.sigmoid(x)


def _mm(a, b):
    return jnp.dot(a.astype(BF16), b.astype(BF16), preferred_element_type=F32)


def _mm_nt(a, b):
    return lax.dot_general(a.astype(BF16), b.astype(BF16), (((1,), (1,)), ((), ())), preferred_element_type=F32)


def _const_spec(shape):
    return pl.BlockSpec(shape, lambda *_: (0,) * len(shape))


def _row(v):
    return v.reshape(1, -1)


def _seq_params():
    return pltpu.CompilerParams(dimension_semantics=("parallel", "arbitrary"), vmem_limit_bytes=VMEM_LIMIT_BYTES)


def _ffn_body(h_ref, g_ref, wg_ref, wu_ref, wo_ref, o_ref, xn_ref, acc_ref):
    j = pl.program_id(1)

    @pl.when(j == 0)
    def _():
        xn_ref[...] = _rms(h_ref[...], g_ref[...]).astype(BF16)
        acc_ref[...] = jnp.zeros_like(acc_ref)

    xn = xn_ref[...]
    gate = jnp.dot(xn, wg_ref[...], preferred_element_type=F32)
    up = jnp.dot(xn, wu_ref[...], preferred_element_type=F32)
    acc_ref[...] += _mm(_silu(gate) * up, wo_ref[...])

    @pl.when(j == pl.num_programs(1) - 1)
    def _():
        o_ref[...] = h_ref[...] + 0.5 * acc_ref[...]


def _ffn(h, norm_g, w_in, w_out, *, tm, tf):
    t, d = h.shape
    f = w_out.shape[0]
    nf = f // tf
    return pl.pallas_call(
        _ffn_body,
        out_shape=jax.ShapeDtypeStruct((t, d), F32),
        grid=(t // tm, nf),
        in_specs=[
            pl.BlockSpec((tm, d), lambda i, j: (i, 0)),
            pl.BlockSpec((1, d), lambda i, j: (0, 0)),
            pl.BlockSpec((d, tf), lambda i, j: (0, j)),
            pl.BlockSpec((d, tf), lambda i, j: (0, j + nf)),
            pl.BlockSpec((tf, d), lambda i, j: (j, 0)),
        ],
        out_specs=pl.BlockSpec((tm, d), lambda i, j: (i, 0)),
        scratch_shapes=[pltpu.VMEM((tm, d), BF16), pltpu.VMEM((tm, d), F32)],
        compiler_params=_seq_params(),
        name="ffn",
    )(h, _row(norm_g), w_in, w_in, w_out)


def _merge_body(h_ref, g_ref, ya_ref, yb_ref, yc_ref, yd_ref, wg_ref, bg_ref, wb_ref, wo_ref, o_ref):
    h = h_ref[...]
    d = h.shape[-1]
    u = _rms(h, g_ref[...]).astype(BF16)
    mixed = jnp.zeros(h.shape, F32)
    for n, y_ref in enumerate((ya_ref, yb_ref, yc_ref, yd_ref)):
        gate = jax.nn.sigmoid(
            jnp.dot(u, wg_ref[:, n * d:(n + 1) * d], preferred_element_type=F32) + bg_ref[:, n * d:(n + 1) * d])
        mixed = mixed + gate * jnp.dot(y_ref[...], wb_ref[n], preferred_element_type=F32)
    o_ref[...] = h + _mm(mixed, wo_ref[...])


def _merge(h, norm_g, ys, w_gate, b_gate, w_branch, w_out, *, tm):
    t, d = h.shape
    nb, w, _ = w_branch.shape
    y_spec = pl.BlockSpec((tm, w), lambda i: (i, 0))
    return pl.pallas_call(
        _merge_body,
        out_shape=jax.ShapeDtypeStruct((t, d), F32),
        grid=(t // tm,),
        in_specs=[
            pl.BlockSpec((tm, d), lambda i: (i, 0)),
            _const_spec((1, d)),
            y_spec, y_spec, y_spec, y_spec,
            _const_spec((d, nb * d)),
            _const_spec((1, nb * d)),
            _const_spec((nb, w, d)),
            _const_spec((d, d)),
        ],
        out_specs=pl.BlockSpec((tm, d), lambda i: (i, 0)),
        compiler_params=pltpu.CompilerParams(
            dimension_semantics=("parallel",), vmem_limit_bytes=VMEM_LIMIT_BYTES),
        name="merge",
    )(h, _row(norm_g), *ys, w_gate, _row(b_gate), w_branch, w_out)


def _ple_body(h_ref, g_ref, p_ref, wg_ref, wp_ref, fg_ref, o_ref, *, final_norm):
    h = h_ref[...]
    gate = jax.nn.sigmoid(_mm(_rms(h, g_ref[...]), wg_ref[...]))
    out = h + gate * _mm(p_ref[...], wp_ref[...])
    if final_norm:
        out = _rms(out, fg_ref[...])
    o_ref[...] = out


def _ple(h, norm_g, p, w_gate, w_proj, final_g, *, tm, final_norm):
    t, d = h.shape
    pd = p.shape[-1]
    return pl.pallas_call(
        functools.partial(_ple_body, final_norm=final_norm),
        out_shape=jax.ShapeDtypeStruct((t, d), F32),
        grid=(t // tm,),
        in_specs=[
            pl.BlockSpec((tm, d), lambda i: (i, 0)),
            _const_spec((1, d)),
            pl.BlockSpec((tm, pd), lambda i: (i, 0)),
            _const_spec((d, d)),
            _const_spec((pd, d)),
            _const_spec((1, d)),
        ],
        out_specs=pl.BlockSpec((tm, d), lambda i: (i, 0)),
        compiler_params=pltpu.CompilerParams(
            dimension_semantics=("parallel",), vmem_limit_bytes=VMEM_LIMIT_BYTES),
        name="ple",
    )(h, _row(norm_g), p, w_gate, w_proj, _row(final_g))


def _segment_perm(l):
    r = jnp.arange(l)
    src = (r % SUBLANES) * (l // SUBLANES) + r // SUBLANES
    return (src[:, None] == jnp.arange(l)[None, :]).astype(BF16)


def _mixer_call(body, name, h, consts, scratch_shapes, *, bsz, tl, width):
    t, d = h.shape
    nt = t // bsz // tl
    return pl.pallas_call(
        body,
        out_shape=jax.ShapeDtypeStruct((t, width), BF16),
        grid=(bsz, nt),
        in_specs=[pl.BlockSpec((tl, d), lambda b, i: (b * nt + i, 0))] + [_const_spec(c.shape) for c in consts],
        out_specs=pl.BlockSpec((tl, width), lambda b, i: (b * nt + i, 0)),
        scratch_shapes=scratch_shapes,
        compiler_params=_seq_params(),
        name=name,
    )(h, *consts)


def _conv_halo(cur_tail, prev_tail):
    row = lax.broadcasted_iota(jnp.int32, cur_tail.shape, 0) % SUBLANES
    slab = jnp.where(row == SUBLANES - 1, prev_tail, cur_tail)
    n = cur_tail.shape[0] // SUBLANES
    return jnp.concatenate(
        [pltpu.roll(slab[g * SUBLANES:(g + 1) * SUBLANES], 1, 0) for g in range(n)], axis=0)


def _lru_body(h_ref, g_ref, p_ref, pt_ref, wx_ref, wg_ref, cw_ref, cb_ref, wr_ref, br_ref, wi_ref, bi_ref,
              lam_ref, o_ref, a_scr, b_scr, state_scr, tail_scr):
    l, w = a_scr.shape
    ls = l // SUBLANES
    kconv = cw_ref.shape[0]
    halo = (kconv - 1) * SUBLANES

    @pl.when(pl.program_id(1) == 0)
    def _():
        state_scr[...] = jnp.zeros_like(state_scr)
        tail_scr[...] = jnp.zeros_like(tail_scr)

    u = _rms(h_ref[...], g_ref[...]).astype(BF16)
    up = jnp.dot(p_ref[...], u, preferred_element_type=F32).astype(BF16)
    x = jnp.dot(up, wx_ref[...], preferred_element_type=F32)
    gate = jnp.dot(up, wg_ref[...], preferred_element_type=F32)

    cur_tail = x[l - halo:, :]
    xpad = jnp.concatenate([_conv_halo(cur_tail, tail_scr[...]), x], axis=0)
    tail_scr[...] = cur_tail
    xc = cb_ref[...] + sum(cw_ref[k:k + 1, :] * xpad[k * SUBLANES:k * SUBLANES + l] for k in range(kconv))

    r = jax.nn.sigmoid(_mm(xc, wr_ref[...]) + br_ref[...])
    ig = jax.nn.sigmoid(_mm(xc, wi_ref[...]) + bi_ref[...])
    log_a = (-LRU_C * jax.nn.softplus(-lam_ref[...])) * r
    a_scr[...] = jnp.exp(log_a)
    b_scr[...] = jnp.sqrt(-jnp.tanh(log_a) * (jnp.exp(2.0 * log_a) + 1.0)) * (ig * xc)

    def step(i, carry):
        hh, pp = carry
        rows = pl.ds(pl.multiple_of(i * SUBLANES, SUBLANES), SUBLANES)
        a_i = a_scr[rows, :]
        hh = a_i * hh + b_scr[rows, :]
        pp = a_i * pp
        b_scr[rows, :] = hh
        a_scr[rows, :] = pp
        return hh, pp

    h_end, p_end = lax.fori_loop(0, ls, step, (jnp.zeros((SUBLANES, w), F32), jnp.ones((SUBLANES, w), F32)),
                                 unroll=4)
    c = state_scr[...]
    carries = []
    for j in range(SUBLANES):
        carries.append(c)
        c = h_end[j:j + 1, :] + p_end[j:j + 1, :] * c
    state_scr[...] = c
    cmat = jnp.concatenate(carries, axis=0)
    hfull = b_scr[...].reshape(ls, SUBLANES, w) + a_scr[...].reshape(ls, SUBLANES, w) * cmat[None]
    out = (hfull.reshape(l, w) * jax.nn.gelu(gate)).astype(BF16)
    o_ref[...] = jnp.dot(pt_ref[...], out, preferred_element_type=F32).astype(BF16)


def _lru_mixer(h, norm_g, w_x, w_g, conv_w, conv_b, w_r, b_r, w_i, b_i, lam, *, bsz, tl):
    w = w_x.shape[1]
    perm = _segment_perm(tl)
    blockdiag = lambda m: jax.scipy.linalg.block_diag(*m).astype(BF16)
    consts = (_row(norm_g), perm, perm.T, w_x, w_g, conv_w, _row(conv_b), blockdiag(w_r), _row(b_r), blockdiag(w_i),
              _row(b_i), _row(lam))
    scratch = [pltpu.VMEM((tl, w), F32), pltpu.VMEM((tl, w), F32), pltpu.VMEM((1, w), F32),
               pltpu.VMEM(((conv_w.shape[0] - 1) * SUBLANES, w), F32)]
    return _mixer_call(_lru_body, "rglru", h, consts, scratch, bsz=bsz, tl=tl, width=w)


def _cmul(ar, ai, br, bi):
    return ar * br - ai * bi, ar * bi + ai * br


def _s5_body(h_ref, g_ref, p_ref, pt_ref, wu_ref, ls_ref, are_ref, aim_ref, bre_ref, bim_ref, cre_ref, cim_ref,
             d_ref, wglu_ref, bglu_ref, o_ref, hs_scr, bmat_scr, abar_scr, pow_scr, state_scr):
    nq, l, ns2 = hs_scr.shape
    ns = ns2 // 2
    ls = l // SUBLANES

    @pl.when(pl.program_id(1) == 0)
    def _():
        step = jnp.exp(ls_ref[...])
        a_re, a_im = are_ref[...], aim_ref[...]
        mag = jnp.exp(a_re * step)
        ab_re, ab_im = mag * jnp.cos(a_im * step), mag * jnp.sin(a_im * step)
        den = a_re * a_re + a_im * a_im
        num_re = ab_re - 1.0
        f_re = (num_re * a_re + ab_im * a_im) / den
        f_im = (ab_im * a_re - num_re * a_im) / den
        b_re, b_im = bre_ref[...], bim_ref[...]
        bmat_scr[:, :, :ns] = (f_re * b_re - f_im * b_im).astype(BF16)
        bmat_scr[:, :, ns:] = (f_re * b_im + f_im * b_re).astype(BF16)
        abar_scr[:, :, :ns] = ab_re
        abar_scr[:, :, ns:] = ab_im
        n = (lax.broadcasted_iota(jnp.int32, (1, ls, 1), 1) + 1).astype(F32)
        pmag = jnp.exp(n * (a_re * step))
        ang = n * (a_im * step)
        pow_scr[:, :, :ns] = pmag * jnp.cos(ang)
        pow_scr[:, :, ns:] = pmag * jnp.sin(ang)
        state_scr[...] = jnp.zeros_like(state_scr)

    u = _rms(h_ref[...], g_ref[...]).astype(BF16)
    up = jnp.dot(p_ref[...], u, preferred_element_type=F32).astype(BF16)
    su = jnp.dot(up, wu_ref[...], preferred_element_type=F32)
    sub = su.astype(BF16)

    ys = []
    for q in range(nq):
        hs_scr[q] = jnp.dot(sub[:, q * LANES:(q + 1) * LANES], bmat_scr[q], preferred_element_type=F32)
        ar = jnp.broadcast_to(abar_scr[q, :, :ns], (SUBLANES, ns))
        ai = jnp.broadcast_to(abar_scr[q, :, ns:], (SUBLANES, ns))

        def scan_step(i, carry, q=q, ar=ar, ai=ai):
            hr, hi = carry
            rows = pl.ds(pl.multiple_of(i * SUBLANES, SUBLANES), SUBLANES)
            pr, pi = _cmul(ar, ai, hr, hi)
            hr = pr + hs_scr[q, rows, :ns]
            hi = pi + hs_scr[q, rows, ns:]
            hs_scr[q, rows, :ns] = hr
            hs_scr[q, rows, ns:] = hi
            return hr, hi

        zero = jnp.zeros((SUBLANES, ns), F32)
        er, ei = lax.fori_loop(0, ls, scan_step, (zero, zero), unroll=4)

        alr, ali = pow_scr[q, ls - 1:ls, :ns], pow_scr[q, ls - 1:ls, ns:]
        cr, ci = state_scr[q, :, :ns], state_scr[q, :, ns:]
        crs, cis = [], []
        for j in range(SUBLANES):
            crs.append(cr)
            cis.append(ci)
            pr, pi = _cmul(alr, ali, cr, ci)
            cr, ci = er[j:j + 1] + pr, ei[j:j + 1] + pi
        state_scr[q, :, :ns] = cr
        state_scr[q, :, ns:] = ci
        cmr, cmi = jnp.concatenate(crs, axis=0), jnp.concatenate(cis, axis=0)

        def fix_step(i, _, q=q, cmr=cmr, cmi=cmi):
            rows = pl.ds(pl.multiple_of(i * SUBLANES, SUBLANES), SUBLANES)
            pr, pi = _cmul(pow_scr[q, pl.ds(i, 1), :ns], pow_scr[q, pl.ds(i, 1), ns:], cmr, cmi)
            hs_scr[q, rows, :ns] += pr
            hs_scr[q, rows, ns:] += pi
            return 0

        lax.fori_loop(0, ls, fix_step, 0, unroll=4)
        ys.append(_mm(hs_scr[q, :, :ns], cre_ref[q]) - _mm(hs_scr[q, :, ns:], cim_ref[q]))

    y = jnp.concatenate(ys, axis=1) + d_ref[...] * su
    z = jax.nn.gelu(y)
    out = z * jax.nn.sigmoid(_mm(z, wglu_ref[...]) + bglu_ref[...])
    o_ref[...] = jnp.dot(pt_ref[...], out.astype(BF16), preferred_element_type=F32).astype(BF16)


def _s5_place_b(b, nq):
    g, p, c = b.shape
    gb = g // nq
    bt = b.reshape(nq, gb, p, c).transpose(0, 1, 3, 2)
    return jnp.einsum('qgcp,gh->qgchp', bt, jnp.eye(gb, dtype=b.dtype)).reshape(nq, gb * c, gb * p)


def _s5_place_c(cm, nq):
    g, c, p = cm.shape
    gb = g // nq
    ct = cm.reshape(nq, gb, c, p).transpose(0, 1, 3, 2)
    return jnp.einsum('qgpc,gh->qgphc', ct, jnp.eye(gb, dtype=cm.dtype)).reshape(nq, gb * p, gb * c)


def _s5_mixer(h, norm_g, w_u, log_step, a_re, a_im, b_re, b_im, c_re, c_im, d_skip, w_glu, b_glu, *, bsz, tl):
    w = w_u.shape[1]
    nq = w // LANES
    ns = S5_BLOCK_STATES
    ls = tl // SUBLANES
    perm = _segment_perm(tl)
    consts = (_row(norm_g), perm, perm.T, w_u,
              jnp.repeat(log_step, S5_STATE).reshape(nq, 1, ns), a_re.reshape(nq, 1, ns), a_im.reshape(nq, 1, ns),
              _s5_place_b(b_re, nq), _s5_place_b(b_im, nq),
              _s5_place_c(c_re, nq).astype(BF16), _s5_place_c(c_im, nq).astype(BF16),
              _row(d_skip), w_glu.astype(BF16), _row(b_glu))
    scratch = [pltpu.VMEM((nq, tl, 2 * ns), F32), pltpu.VMEM((nq, LANES, 2 * ns), BF16),
               pltpu.VMEM((nq, 1, 2 * ns), F32), pltpu.VMEM((nq, ls, 2 * ns), F32),
               pltpu.VMEM((nq, 1, 2 * ns), F32)]
    return _mixer_call(_s5_body, "s5", h, consts, scratch, bsz=bsz, tl=tl, width=w)


def _causal_conv(x, xpad_scr, tail_scr, cw_ref, first):
    l = x.shape[0]
    k = cw_ref.shape[0]

    @pl.when(first)
    def _():
        tail_scr[...] = jnp.zeros_like(tail_scr)

    xpad_scr[:SUBLANES, :] = tail_scr[...]
    xpad_scr[SUBLANES:, :] = x
    tail_scr[...] = x[l - SUBLANES:, :]
    off = SUBLANES - (k - 1)
    return sum(cw_ref[j:j + 1, :] * xpad_scr[off + j:off + j + l, :] for j in range(k))


def _split_dot(tri, x):
    hi = x.astype(BF16)
    lo = (x - hi.astype(F32)).astype(BF16)
    return jnp.dot(tri, hi, preferred_element_type=F32) + jnp.dot(tri, lo, preferred_element_type=F32)


def _chunk_masks(q):
    ri = lax.broadcasted_iota(jnp.int32, (q, q), 0)
    ci = lax.broadcasted_iota(jnp.int32, (q, q), 1)
    return ri >= ci, ri > ci


def _decay_matrix(cs, cst, lane, causal):
    diff = cs[:, lane:lane + 1] - cst[lane:lane + 1, :]
    return jnp.where(causal, jnp.exp(jnp.where(causal, diff, 0.0)), 0.0)


def _ssd_body(h_ref, g_ref, wz_ref, wx_ref, wdt_ref, cw_ref, cb_ref, dtb_ref, alog_ref, dsk_ref, ng_ref, o_ref,
              xpad_scr, tail_scr, state_scr):
    l, w = o_ref.shape
    ngroups, nstate, gw = state_scr.shape
    q = M2_CHUNK
    first = pl.program_id(1) == 0

    @pl.when(first)
    def _():
        state_scr[...] = jnp.zeros_like(state_scr)

    u = _rms(h_ref[...], g_ref[...]).astype(BF16)
    z = jnp.dot(u, wz_ref[...], preferred_element_type=F32)
    xbc_raw = jnp.dot(u, wx_ref[...], preferred_element_type=F32)
    dt_raw = jnp.dot(u, wdt_ref[...], preferred_element_type=F32)
    xbc = _silu(_causal_conv(xbc_raw, xpad_scr, tail_scr, cw_ref, first) + cb_ref[...])
    xs = xbc[:, :w]
    bm = xbc[:, w:w + ngroups * nstate]
    cm = xbc[:, w + ngroups * nstate:]
    dt = jax.nn.softplus(dt_raw + dtb_ref[...])
    da = dt * (-jnp.exp(alog_ref[...]))
    xdt = xs * dt

    causal, _ = _chunk_masks(q)
    tri = causal.astype(BF16)
    head_of_lane = lax.broadcasted_iota(jnp.int32, (1, gw), 1) // M2_HEAD_DIM

    states = [state_scr[g] for g in range(ngroups)]
    for c in range(l // q):
        rows = slice(c * q, (c + 1) * q)
        cs = _split_dot(tri, da[rows])
        cs_last = cs[q - 1:q, :]
        dte = jnp.exp(cs_last - cs)
        ecs = jnp.exp(cs)
        cdec = jnp.exp(cs_last)
        cst = cs.T
        ychunk = []
        for g in range(ngroups):
            gl = slice(g * gw, (g + 1) * gw)
            bg = bm[rows, g * nstate:(g + 1) * nstate]
            cg = cm[rows, g * nstate:(g + 1) * nstate]
            cb = _mm_nt(cg, bg)
            xg = xdt[rows, gl]
            yg = _mm(cg, states[g]) * ecs[:, gl]
            for hh in range(gw // M2_HEAD_DIM):
                dec = _decay_matrix(cs, cst, g * gw + hh * M2_HEAD_DIM, causal)
                yg = yg + _mm(cb * dec, jnp.where(head_of_lane == hh, xg, 0.0))
            states[g] = states[g] * cdec[:, gl] + _mm(bg.T, xg * dte[:, gl])
            ychunk.append(yg)
        y = jnp.concatenate(ychunk, axis=1) + dsk_ref[...] * xs[rows]
        y = y * _silu(z[rows])
        o_ref[rows, :] = _rms(y, ng_ref[...]).astype(BF16)
    for g in range(ngroups):
        state_scr[g] = states[g]


def _ssd_mixer(h, norm_g, w_z, w_xbc, w_dt, conv_w, conv_b, dt_bias, a_log, d_skip, out_norm_g, *, bsz, tl):
    w = w_z.shape[1]
    cdim = w_xbc.shape[1]
    rep = lambda v: jnp.repeat(v, M2_HEAD_DIM).reshape(1, w)
    consts = (_row(norm_g), w_z, w_xbc, jnp.repeat(w_dt, M2_HEAD_DIM, axis=1), conv_w, _row(conv_b), rep(dt_bias),
              rep(a_log), rep(d_skip), _row(out_norm_g))
    scratch = [pltpu.VMEM((tl + SUBLANES, cdim), F32), pltpu.VMEM((SUBLANES, cdim), F32),
               pltpu.VMEM((M2_GROUPS, M2_STATE, w // M2_GROUPS), F32)]
    return _mixer_call(_ssd_body, "ssd", h, consts, scratch, bsz=bsz, tl=tl, width=w)


def _l2norm(x):
    return x * lax.rsqrt(jnp.sum(x * x, axis=-1, keepdims=True) + EPS)


def _unit_lower_inverse(m):
    q = m.shape[0]
    eye = (lax.broadcasted_iota(jnp.int32, (q, q), 0) == lax.broadcasted_iota(jnp.int32, (q, q), 1)).astype(F32)
    pw = -m
    inv = eye + pw
    span = 2
    while span < q:
        pw = _mm(pw, pw)
        inv = inv + _mm(inv, pw)
        span *= 2
    return inv


def _gdn_body(h_ref, g_ref, wqkv_ref, wg_ref, wb_ref, wa_ref, cw_ref, dtb_ref, alog_ref, ng_ref, o_ref,
              xpad_scr, tail_scr, state_scr):
    l, w = o_ref.shape
    nheads, dk, dv = state_scr.shape
    q = GDN_CHUNK
    first = pl.program_id(1) == 0

    @pl.when(first)
    def _():
        state_scr[...] = jnp.zeros_like(state_scr)

    u = _rms(h_ref[...], g_ref[...]).astype(BF16)
    qkv_raw = jnp.dot(u, wqkv_ref[...], preferred_element_type=F32)
    gate = jnp.dot(u, wg_ref[...], preferred_element_type=F32)
    beta = jax.nn.sigmoid(jnp.dot(u, wb_ref[...], preferred_element_type=F32))
    a_raw = jnp.dot(u, wa_ref[...], preferred_element_type=F32)
    glog = -jnp.exp(alog_ref[...]) * jax.nn.softplus(a_raw + dtb_ref[...])
    qkv = _silu(_causal_conv(qkv_raw, xpad_scr, tail_scr, cw_ref, first))

    causal, strict = _chunk_masks(q)
    tri = causal.astype(BF16)

    states = [state_scr[hd] for hd in range(nheads)]
    for c in range(l // q):
        rows = slice(c * q, (c + 1) * q)
        gcs = _split_dot(tri, glog[rows])
        g_last = gcs[q - 1:q, :]
        egcs = jnp.exp(gcs)
        kscale = jnp.exp(g_last - gcs)
        cdec = jnp.exp(g_last)
        gcst = gcs.T
        for hd in range(nheads):
            hl = slice(hd * dk, (hd + 1) * dk)
            qh = _l2norm(qkv[rows, hd * dk:(hd + 1) * dk]) * (dk ** -0.5)
            kh = _l2norm(qkv[rows, w + hd * dk:w + (hd + 1) * dk])
            vh = qkv[rows, 2 * w + hd * dv:2 * w + (hd + 1) * dv]
            bh = beta[rows, hl]
            decay = _decay_matrix(gcs, gcst, hd * dk, causal)
            kb = kh * bh
            inv = _unit_lower_inverse(jnp.where(strict, _mm_nt(kb, kh) * decay, 0.0))
            sol = _mm(inv, jnp.concatenate([vh * bh, kb * egcs[:, hl]], axis=1))
            uu, ww = sol[:, :dv], sol[:, dv:]
            qk = _mm_nt(qh, kh) * decay
            st = states[hd]
            v_new = uu - _mm(ww, st)
            o = _mm(qh * egcs[:, hl], st) + _mm(qk, v_new)
            states[hd] = st * cdec[:, hl] + _mm((kh * kscale[:, hl]).T, v_new)
            o_ref[rows, hl] = (_rms(o, ng_ref[...]) * _silu(gate[rows, hl])).astype(BF16)
    for hd in range(nheads):
        state_scr[hd] = states[hd]


def _gdn_mixer(h, norm_g, w_qkv, w_g, w_beta, w_a, conv_w, dt_bias, a_log, out_norm_g, *, bsz, tl):
    w = w_g.shape[1]
    cdim = w_qkv.shape[1]
    rep = lambda v: jnp.repeat(v, GDN_HEAD_DIM).reshape(1, w)
    consts = (_row(norm_g), w_qkv, w_g, jnp.repeat(w_beta, GDN_HEAD_DIM, axis=1),
              jnp.repeat(w_a, GDN_HEAD_DIM, axis=1), conv_w, rep(dt_bias), rep(a_log), _row(out_norm_g))
    scratch = [pltpu.VMEM((tl + SUBLANES, cdim), F32), pltpu.VMEM((SUBLANES, cdim), F32),
               pltpu.VMEM((w // GDN_HEAD_DIM, GDN_HEAD_DIM, GDN_HEAD_DIM), F32)]
    return _mixer_call(_gdn_body, "gdn", h, consts, scratch, bsz=bsz, tl=tl, width=w)


def _split_cols(t, sizes):
    out, start = [], 0
    for s in sizes:
        out.append(t[:, start:start + s])
        start += s
    return out


def kernel(x, p, ffn1_norm, ffn1_w_in, ffn1_w_out, mix_norm, w_in, w_gate, b_gate, s5_log_step, s5_a_re, s5_a_im, s5_b_re, s5_b_im, s5_c_re, s5_c_im, s5_d, s5_w_glu, s5_b_glu, lru_conv_w, lru_conv_b, lru_w_r, lru_b_r, lru_w_i, lru_b_i, lru_lambda, m2_conv_w, m2_conv_b, m2_dt_bias, m2_a_log, m2_d, m2_norm, gdn_conv_w, gdn_dt_bias, gdn_a_log, gdn_norm, w_branch, w_out, ffn2_norm, ffn2_w_in, ffn2_w_out, ple_norm, ple_w_gate, ple_w_proj, final_norm):
    bsz, s, d = x.shape
    depth = p.shape[0]
    t = bsz * s
    h = x.reshape(t, d)
    tm_ffn, tf_ffn, tm, tl = 512, ffn1_w_out.shape[1] // 2, 512, 512
    for i in range(depth):
        h = _ffn(h, ffn1_norm[i], ffn1_w_in[i].astype(BF16), ffn1_w_out[i].astype(BF16), tm=tm_ffn, tf=tf_ffn)
        (w_s5, w_lx, w_lg, w_mz, w_mx, w_mdt, w_gqkv, w_gg, w_gb, w_ga) = _split_cols(w_in[i].astype(BF16), IN_SPLITS)
        y_a = _s5_mixer(h, mix_norm[i], w_s5, s5_log_step[i], s5_a_re[i], s5_a_im[i], s5_b_re[i], s5_b_im[i],
                        s5_c_re[i], s5_c_im[i], s5_d[i], s5_w_glu[i], s5_b_glu[i], bsz=bsz, tl=tl)
        y_b = _lru_mixer(h, mix_norm[i], w_lx, w_lg, lru_conv_w[i], lru_conv_b[i], lru_w_r[i], lru_b_r[i],
                         lru_w_i[i], lru_b_i[i], lru_lambda[i], bsz=bsz, tl=tl)
        y_c = _ssd_mixer(h, mix_norm[i], w_mz, w_mx, w_mdt, m2_conv_w[i], m2_conv_b[i], m2_dt_bias[i],
                         m2_a_log[i], m2_d[i], m2_norm[i], bsz=bsz, tl=tl)
        y_d = _gdn_mixer(h, mix_norm[i], w_gqkv, w_gg, w_gb, w_ga, gdn_conv_w[i], gdn_dt_bias[i], gdn_a_log[i],
                         gdn_norm[i], bsz=bsz, tl=tl)
        h = _merge(h, mix_norm[i], (y_a, y_b, y_c, y_d), w_gate[i].astype(BF16), b_gate[i],
                   w_branch[i].astype(BF16), w_out[i].astype(BF16), tm=tm)
        h = _ffn(h, ffn2_norm[i], ffn2_w_in[i].astype(BF16), ffn2_w_out[i].astype(BF16), tm=tm_ffn, tf=tf_ffn)
        h = _ple(h, ple_norm[i], p[i].reshape(t, -1), ple_w_gate[i].astype(BF16), ple_w_proj[i].astype(BF16),
                 final_norm, tm=tm, final_norm=(i == depth - 1))
    return h.reshape(bsz, s, d)
```

```python
import functools

import jax
import jax.numpy as jnp
from jax import lax
from jax.experimental import pallas as pl
from jax.experimental.pallas import tpu as pltpu

EPS = 1e-6
BF16 = jnp.bfloat16
F32 = jnp.float32

V7X_VMEM_BYTES = 64 * 1024 * 1024
VMEM_LIMIT_BYTES = V7X_VMEM_BYTES - 8 * 1024 * 1024
SUBLANES = 8
LANES = 128

BRANCH_WIDTH = 512
S5_GROUP_CH = 16
S5_STATE = 64
S5_BLOCK_STATES = (LANES // S5_GROUP_CH) * S5_STATE
LRU_C = 8.0
M2_HEAD_DIM = 64
M2_HEADS = BRANCH_WIDTH // M2_HEAD_DIM
M2_GROUPS = 2
M2_STATE = 128
M2_CONV_DIM = BRANCH_WIDTH + 2 * M2_GROUPS * M2_STATE
M2_CHUNK = 128
GDN_HEAD_DIM = 128
GDN_HEADS = BRANCH_WIDTH // GDN_HEAD_DIM
GDN_CHUNK = 64
GDN_INTERLEAVE = 4
MXU_WIDTH = 256
IN_SPLITS = (BRANCH_WIDTH, BRANCH_WIDTH, BRANCH_WIDTH, BRANCH_WIDTH, M2_CONV_DIM, M2_HEADS, 3 * BRANCH_WIDTH, BRANCH_WIDTH, GDN_HEADS, GDN_HEADS)


def _rms(x, g):
    return x * lax.rsqrt(jnp.mean(x * x, axis=-1, keepdims=True) + EPS) * g


def _silu(x):
    return x * jax.nn.sigmoid(x)


def _mm(a, b):
    return jnp.dot(a.astype(BF16), b.astype(BF16), preferred_element_type=F32)


def _mm_nt(a, b):
    return lax.dot_general(a.astype(BF16), b.astype(BF16), (((1,), (1,)), ((), ())), preferred_element_type=F32)


def _const_spec(shape):
    return pl.BlockSpec(shape, lambda *_: (0,) * len(shape))


def _row(v):
    return v.reshape(1, -1)


def _seq_params():
    return pltpu.CompilerParams(dimension_semantics=("parallel", "arbitrary"), vmem_limit_bytes=VMEM_LIMIT_BYTES)


def _ffn_body(h_ref, g_ref, wi_ref, wo_ref, o_ref):
    f = wo_ref.shape[0]
    h = h_ref[...]
    xn = _rms(h, g_ref[...]).astype(BF16)
    acts = []
    for s in range(0, f, MXU_WIDTH):
        gate = jnp.dot(xn, wi_ref[:, s:s + MXU_WIDTH], preferred_element_type=F32)
        up = jnp.dot(xn, wi_ref[:, f + s:f + s + MXU_WIDTH], preferred_element_type=F32)
        acts.append((_silu(gate) * up).astype(BF16))
    o_ref[...] = h + 0.5 * jnp.dot(jnp.concatenate(acts, axis=1), wo_ref[...], preferred_element_type=F32)


def _resident_spec(shape):
    return pl.BlockSpec(shape, lambda *_: (0,) * len(shape), pipeline_mode=pl.Buffered(1))


def _ffn(h, norm_g, w_in, w_out, *, tm):
    t, d = h.shape
    f = w_out.shape[0]
    return pl.pallas_call(
        _ffn_body,
        out_shape=jax.ShapeDtypeStruct((t, d), F32),
        grid=(t // tm,),
        in_specs=[
            pl.BlockSpec((tm, d), lambda i: (i, 0)),
            _resident_spec((1, d)),
            _resident_spec((d, 2 * f)),
            _resident_spec((f, d)),
        ],
        out_specs=pl.BlockSpec((tm, d), lambda i: (i, 0)),
        compiler_params=pltpu.CompilerParams(
            dimension_semantics=("parallel",), vmem_limit_bytes=VMEM_LIMIT_BYTES),
        name="ffn",
    )(h, _row(norm_g), w_in, w_out)


def _merge_body(h_ref, g_ref, ya_ref, yb_ref, yc_ref, yd_ref, wg_ref, bg_ref, wb_ref, wo_ref, o_ref):
    h = h_ref[...]
    d = h.shape[-1]
    u = _rms(h, g_ref[...]).astype(BF16)
    mixed = jnp.zeros(h.shape, F32)
    for n, y_ref in enumerate((ya_ref, yb_ref, yc_ref, yd_ref)):
        gate = jax.nn.sigmoid(
            jnp.dot(u, wg_ref[:, n * d:(n + 1) * d], preferred_element_type=F32) + bg_ref[:, n * d:(n + 1) * d])
        mixed = mixed + gate * jnp.dot(y_ref[...], wb_ref[n], preferred_element_type=F32)
    o_ref[...] = h + _mm(mixed, wo_ref[...])


def _merge(h, norm_g, ys, w_gate, b_gate, w_branch, w_out, *, tm):
    t, d = h.shape
    nb, w, _ = w_branch.shape
    y_spec = pl.BlockSpec((tm, w), lambda i: (i, 0))
    return pl.pallas_call(
        _merge_body,
        out_shape=jax.ShapeDtypeStruct((t, d), F32),
        grid=(t // tm,),
        in_specs=[
            pl.BlockSpec((tm, d), lambda i: (i, 0)),
            _const_spec((1, d)),
            y_spec, y_spec, y_spec, y_spec,
            _const_spec((d, nb * d)),
            _const_spec((1, nb * d)),
            _const_spec((nb, w, d)),
            _const_spec((d, d)),
        ],
        out_specs=pl.BlockSpec((tm, d), lambda i: (i, 0)),
        compiler_params=pltpu.CompilerParams(
            dimension_semantics=("parallel",), vmem_limit_bytes=VMEM_LIMIT_BYTES),
        name="merge",
    )(h, _row(norm_g), *ys, w_gate, _row(b_gate), w_branch, w_out)


def _ple_body(h_ref, g_ref, p_ref, wg_ref, wp_ref, fg_ref, o_ref, *, final_norm):
    h = h_ref[...]
    gate = jax.nn.sigmoid(_mm(_rms(h, g_ref[...]), wg_ref[...]))
    out = h + gate * _mm(p_ref[...], wp_ref[...])
    if final_norm:
        out = _rms(out, fg_ref[...])
    o_ref[...] = out


def _ple(h, norm_g, p, w_gate, w_proj, final_g, *, tm, final_norm):
    t, d = h.shape
    pd = p.shape[-1]
    return pl.pallas_call(
        functools.partial(_ple_body, final_norm=final_norm),
        out_shape=jax.ShapeDtypeStruct((t, d), F32),
        grid=(t // tm,),
        in_specs=[
            pl.BlockSpec((tm, d), lambda i: (i, 0)),
            _const_spec((1, d)),
            pl.BlockSpec((tm, pd), lambda i: (i, 0)),
            _const_spec((d, d)),
            _const_spec((pd, d)),
            _const_spec((1, d)),
        ],
        out_specs=pl.BlockSpec((tm, d), lambda i: (i, 0)),
        compiler_params=pltpu.CompilerParams(
            dimension_semantics=("parallel",), vmem_limit_bytes=VMEM_LIMIT_BYTES),
        name="ple",
    )(h, _row(norm_g), p, w_gate, w_proj, _row(final_g))


def _segment_perm(l):
    r = jnp.arange(l)
    src = (r % SUBLANES) * (l // SUBLANES) + r // SUBLANES
    return (src[:, None] == jnp.arange(l)[None, :]).astype(BF16)


def _mixer_call(body, name, h, consts, scratch_shapes, *, bsz, tl, width):
    t, d = h.shape
    nt = t // bsz // tl
    return pl.pallas_call(
        body,
        out_shape=jax.ShapeDtypeStruct((t, width), BF16),
        grid=(bsz, nt),
        in_specs=[pl.BlockSpec((tl, d), lambda b, i: (b * nt + i, 0))] + [_const_spec(c.shape) for c in consts],
        out_specs=pl.BlockSpec((tl, width), lambda b, i: (b * nt + i, 0)),
        scratch_shapes=scratch_shapes,
        compiler_params=_seq_params(),
        name=name,
    )(h, *consts)


def _conv_halo(cur_tail, prev_tail):
    row = lax.broadcasted_iota(jnp.int32, cur_tail.shape, 0) % SUBLANES
    slab = jnp.where(row == SUBLANES - 1, prev_tail, cur_tail)
    n = cur_tail.shape[0] // SUBLANES
    return jnp.concatenate(
        [pltpu.roll(slab[g * SUBLANES:(g + 1) * SUBLANES], 1, 0) for g in range(n)], axis=0)


def _lru_body(h_ref, g_ref, p_ref, pt_ref, wx_ref, wg_ref, cw_ref, cb_ref, wr_ref, br_ref, wi_ref, bi_ref,
              lam_ref, o_ref, a_scr, b_scr, state_scr, tail_scr):
    l, w = a_scr.shape
    ls = l // SUBLANES
    kconv = cw_ref.shape[0]
    halo = (kconv - 1) * SUBLANES

    @pl.when(pl.program_id(1) == 0)
    def _():
        state_scr[...] = jnp.zeros_like(state_scr)
        tail_scr[...] = jnp.zeros_like(tail_scr)

    u = _rms(h_ref[...], g_ref[...]).astype(BF16)
    up = jnp.dot(p_ref[...], u, preferred_element_type=F32).astype(BF16)
    x = jnp.dot(up, wx_ref[...], preferred_element_type=F32)
    gate = jnp.dot(up, wg_ref[...], preferred_element_type=F32)

    cur_tail = x[l - halo:, :]
    xpad = jnp.concatenate([_conv_halo(cur_tail, tail_scr[...]), x], axis=0)
    tail_scr[...] = cur_tail
    xc = cb_ref[...] + sum(cw_ref[k:k + 1, :] * xpad[k * SUBLANES:k * SUBLANES + l] for k in range(kconv))

    r = jax.nn.sigmoid(_mm(xc, wr_ref[...]) + br_ref[...])
    ig = jax.nn.sigmoid(_mm(xc, wi_ref[...]) + bi_ref[...])
    log_a = (-LRU_C * jax.nn.softplus(-lam_ref[...])) * r
    a_scr[...] = jnp.exp(log_a)
    b_scr[...] = jnp.sqrt(-jnp.tanh(log_a) * (jnp.exp(2.0 * log_a) + 1.0)) * (ig * xc)

    def step(i, carry):
        hh, pp = carry
        rows = pl.ds(pl.multiple_of(i * SUBLANES, SUBLANES), SUBLANES)
        a_i = a_scr[rows, :]
        hh = a_i * hh + b_scr[rows, :]
        pp = a_i * pp
        b_scr[rows, :] = hh
        a_scr[rows, :] = pp
        return hh, pp

    h_end, p_end = lax.fori_loop(0, ls, step, (jnp.zeros((SUBLANES, w), F32), jnp.ones((SUBLANES, w), F32)),
                                 unroll=4)
    c = state_scr[...]
    carries = []
    for j in range(SUBLANES):
        carries.append(c)
        c = h_end[j:j + 1, :] + p_end[j:j + 1, :] * c
    state_scr[...] = c
    cmat = jnp.concatenate(carries, axis=0)
    hfull = b_scr[...].reshape(ls, SUBLANES, w) + a_scr[...].reshape(ls, SUBLANES, w) * cmat[None]
    out = (hfull.reshape(l, w) * jax.nn.gelu(gate)).astype(BF16)
    o_ref[...] = jnp.dot(pt_ref[...], out, preferred_element_type=F32).astype(BF16)


def _lru_mixer(h, norm_g, w_x, w_g, conv_w, conv_b, w_r, b_r, w_i, b_i, lam, *, bsz, tl):
    w = w_x.shape[1]
    perm = _segment_perm(tl)
    blockdiag = lambda m: jax.scipy.linalg.block_diag(*m).astype(BF16)
    consts = (_row(norm_g), perm, perm.T, w_x, w_g, conv_w, _row(conv_b), blockdiag(w_r), _row(b_r), blockdiag(w_i),
              _row(b_i), _row(lam))
    scratch = [pltpu.VMEM((tl, w), F32), pltpu.VMEM((tl, w), F32), pltpu.VMEM((1, w), F32),
               pltpu.VMEM(((conv_w.shape[0] - 1) * SUBLANES, w), F32)]
    return _mixer_call(_lru_body, "rglru", h, consts, scratch, bsz=bsz, tl=tl, width=w)


def _cmul(ar, ai, br, bi):
    return ar * br - ai * bi, ar * bi + ai * br


def _s5_body(h_ref, g_ref, p_ref, pt_ref, wu_ref, ls_ref, are_ref, aim_ref, bre_ref, bim_ref, cre_ref, cim_ref,
             d_ref, wglu_ref, bglu_ref, o_ref, hs_scr, bmat_scr, abar_scr, pow_scr, state_scr):
    nq, l, ns2 = hs_scr.shape
    ns = ns2 // 2
    ls = l // SUBLANES

    @pl.when(pl.program_id(1) == 0)
    def _():
        step = jnp.exp(ls_ref[...])
        a_re, a_im = are_ref[...], aim_ref[...]
        mag = jnp.exp(a_re * step)
        ab_re, ab_im = mag * jnp.cos(a_im * step), mag * jnp.sin(a_im * step)
        den = a_re * a_re + a_im * a_im
        num_re = ab_re - 1.0
        f_re = (num_re * a_re + ab_im * a_im) / den
        f_im = (ab_im * a_re - num_re * a_im) / den
        b_re, b_im = bre_ref[...], bim_ref[...]
        bmat_scr[:, :, :ns] = (f_re * b_re - f_im * b_im).astype(BF16)
        bmat_scr[:, :, ns:] = (f_re * b_im + f_im * b_re).astype(BF16)
        abar_scr[:, :, :ns] = ab_re
        abar_scr[:, :, ns:] = ab_im
        n = (lax.broadcasted_iota(jnp.int32, (1, ls, 1), 1) + 1).astype(F32)
        pmag = jnp.exp(n * (a_re * step))
        ang = n * (a_im * step)
        pow_scr[:, :, :ns] = pmag * jnp.cos(ang)
        pow_scr[:, :, ns:] = pmag * jnp.sin(ang)
        state_scr[...] = jnp.zeros_like(state_scr)

    u = _rms(h_ref[...], g_ref[...]).astype(BF16)
    up = jnp.dot(p_ref[...], u, preferred_element_type=F32).astype(BF16)
    su = jnp.dot(up, wu_ref[...], preferred_element_type=F32)
    sub = su.astype(BF16)

    ys = []
    for q in range(nq):
        hs_scr[q] = jnp.dot(sub[:, q * LANES:(q + 1) * LANES], bmat_scr[q], preferred_element_type=F32)
        ar = jnp.broadcast_to(abar_scr[q, :, :ns], (SUBLANES, ns))
        ai = jnp.broadcast_to(abar_scr[q, :, ns:], (SUBLANES, ns))

        def scan_step(i, carry, q=q, ar=ar, ai=ai):
            hr, hi = carry
            rows = pl.ds(pl.multiple_of(i * SUBLANES, SUBLANES), SUBLANES)
            pr, pi = _cmul(ar, ai, hr, hi)
            hr = pr + hs_scr[q, rows, :ns]
            hi = pi + hs_scr[q, rows, ns:]
            hs_scr[q, rows, :ns] = hr
            hs_scr[q, rows, ns:] = hi
            return hr, hi

        zero = jnp.zeros((SUBLANES, ns), F32)
        er, ei = lax.fori_loop(0, ls, scan_step, (zero, zero), unroll=4)

        alr, ali = pow_scr[q, ls - 1:ls, :ns], pow_scr[q, ls - 1:ls, ns:]
        cr, ci = state_scr[q, :, :ns], state_scr[q, :, ns:]
        crs, cis = [], []
        for j in range(SUBLANES):
            crs.append(cr)
            cis.append(ci)
            pr, pi = _cmul(alr, ali, cr, ci)
            cr, ci = er[j:j + 1] + pr, ei[j:j + 1] + pi
        state_scr[q, :, :ns] = cr
        state_scr[q, :, ns:] = ci
        cmr, cmi = jnp.concatenate(crs, axis=0), jnp.concatenate(cis, axis=0)

        def fix_step(i, _, q=q, cmr=cmr, cmi=cmi):
            rows = pl.ds(pl.multiple_of(i * SUBLANES, SUBLANES), SUBLANES)
            pr, pi = _cmul(pow_scr[q, pl.ds(i, 1), :ns], pow_scr[q, pl.ds(i, 1), ns:], cmr, cmi)
            hs_scr[q, rows, :ns] += pr
            hs_scr[q, rows, ns:] += pi
            return 0

        lax.fori_loop(0, ls, fix_step, 0, unroll=4)
        ys.append(_mm(hs_scr[q, :, :ns], cre_ref[q]) - _mm(hs_scr[q, :, ns:], cim_ref[q]))

    y = jnp.concatenate(ys, axis=1) + d_ref[...] * su
    z = jax.nn.gelu(y)
    out = z * jax.nn.sigmoid(_mm(z, wglu_ref[...]) + bglu_ref[...])
    o_ref[...] = jnp.dot(pt_ref[...], out.astype(BF16), preferred_element_type=F32).astype(BF16)


def _s5_place_b(b, nq):
    g, p, c = b.shape
    gb = g // nq
    bt = b.reshape(nq, gb, p, c).transpose(0, 1, 3, 2)
    return jnp.einsum('qgcp,gh->qgchp', bt, jnp.eye(gb, dtype=b.dtype)).reshape(nq, gb * c, gb * p)


def _s5_place_c(cm, nq):
    g, c, p = cm.shape
    gb = g // nq
    ct = cm.reshape(nq, gb, c, p).transpose(0, 1, 3, 2)
    return jnp.einsum('qgpc,gh->qgphc', ct, jnp.eye(gb, dtype=cm.dtype)).reshape(nq, gb * p, gb * c)


def _s5_mixer(h, norm_g, w_u, log_step, a_re, a_im, b_re, b_im, c_re, c_im, d_skip, w_glu, b_glu, *, bsz, tl):
    w = w_u.shape[1]
    nq = w // LANES
    ns = S5_BLOCK_STATES
    ls = tl // SUBLANES
    perm = _segment_perm(tl)
    consts = (_row(norm_g), perm, perm.T, w_u,
              jnp.repeat(log_step, S5_STATE).reshape(nq, 1, ns), a_re.reshape(nq, 1, ns), a_im.reshape(nq, 1, ns),
              _s5_place_b(b_re, nq), _s5_place_b(b_im, nq),
              _s5_place_c(c_re, nq).astype(BF16), _s5_place_c(c_im, nq).astype(BF16),
              _row(d_skip), w_glu.astype(BF16), _row(b_glu))
    scratch = [pltpu.VMEM((nq, tl, 2 * ns), F32), pltpu.VMEM((nq, LANES, 2 * ns), BF16),
               pltpu.VMEM((nq, 1, 2 * ns), F32), pltpu.VMEM((nq, ls, 2 * ns), F32),
               pltpu.VMEM((nq, 1, 2 * ns), F32)]
    return _mixer_call(_s5_body, "s5", h, consts, scratch, bsz=bsz, tl=tl, width=w)


def _causal_conv(x, xpad_scr, tail_scr, cw_ref, first):
    l = x.shape[0]
    k = cw_ref.shape[0]

    @pl.when(first)
    def _():
        tail_scr[...] = jnp.zeros_like(tail_scr)

    xpad_scr[:SUBLANES, :] = tail_scr[...]
    xpad_scr[SUBLANES:, :] = x
    tail_scr[...] = x[l - SUBLANES:, :]
    off = SUBLANES - (k - 1)
    return sum(cw_ref[j:j + 1, :] * xpad_scr[off + j:off + j + l, :] for j in range(k))


def _split_dot(tri, x):
    hi = x.astype(BF16)
    lo = (x - hi.astype(F32)).astype(BF16)
    return jnp.dot(tri, hi, preferred_element_type=F32) + jnp.dot(tri, lo, preferred_element_type=F32)


def _lower_tri(q):
    return lax.broadcasted_iota(jnp.int32, (q, q), 0) >= lax.broadcasted_iota(jnp.int32, (q, q), 1)


def _decay_matrix(cs, cst, lane, causal):
    diff = cs[:, lane:lane + 1] - cst[lane:lane + 1, :]
    return jnp.where(causal, jnp.exp(jnp.where(causal, diff, 0.0)), 0.0)


def _ssd_body(h_ref, g_ref, wz_ref, wx_ref, wdt_ref, cw_ref, cb_ref, dtb_ref, alog_ref, dsk_ref, ng_ref, o_ref,
              xpad_scr, tail_scr, state_scr):
    l, w = o_ref.shape
    ngroups, nstate, gw = state_scr.shape
    q = M2_CHUNK
    first = pl.program_id(1) == 0

    @pl.when(first)
    def _():
        state_scr[...] = jnp.zeros_like(state_scr)

    u = _rms(h_ref[...], g_ref[...]).astype(BF16)
    z = jnp.dot(u, wz_ref[...], preferred_element_type=F32)
    xbc_raw = jnp.dot(u, wx_ref[...], preferred_element_type=F32)
    dt_raw = jnp.dot(u, wdt_ref[...], preferred_element_type=F32)
    xbc = _silu(_causal_conv(xbc_raw, xpad_scr, tail_scr, cw_ref, first) + cb_ref[...])
    xs = xbc[:, :w]
    bm = xbc[:, w:w + ngroups * nstate]
    cm = xbc[:, w + ngroups * nstate:]
    dt = jax.nn.softplus(dt_raw + dtb_ref[...])
    da = dt * (-jnp.exp(alog_ref[...]))
    xdt = xs * dt

    causal = _lower_tri(q)
    tri = causal.astype(BF16)
    head_of_lane = lax.broadcasted_iota(jnp.int32, (1, gw), 1) // M2_HEAD_DIM

    states = [state_scr[g] for g in range(ngroups)]
    for c in range(l // q):
        rows = slice(c * q, (c + 1) * q)
        cs = _split_dot(tri, da[rows])
        cs_last = cs[q - 1:q, :]
        dte = jnp.exp(cs_last - cs)
        ecs = jnp.exp(cs)
        cdec = jnp.exp(cs_last)
        cst = cs.T
        ychunk = []
        for g in range(ngroups):
            gl = slice(g * gw, (g + 1) * gw)
            bg = bm[rows, g * nstate:(g + 1) * nstate]
            cg = cm[rows, g * nstate:(g + 1) * nstate]
            cb = _mm_nt(cg, bg)
            xg = xdt[rows, gl]
            yg = _mm(cg, states[g]) * ecs[:, gl]
            for hh in range(gw // M2_HEAD_DIM):
                dec = _decay_matrix(cs, cst, g * gw + hh * M2_HEAD_DIM, causal)
                yg = yg + _mm(cb * dec, jnp.where(head_of_lane == hh, xg, 0.0))
            states[g] = states[g] * cdec[:, gl] + _mm(bg.T, xg * dte[:, gl])
            ychunk.append(yg)
        y = jnp.concatenate(ychunk, axis=1) + dsk_ref[...] * xs[rows]
        y = y * _silu(z[rows])
        o_ref[rows, :] = _rms(y, ng_ref[...]).astype(BF16)
    for g in range(ngroups):
        state_scr[g] = states[g]


def _ssd_mixer(h, norm_g, w_z, w_xbc, w_dt, conv_w, conv_b, dt_bias, a_log, d_skip, out_norm_g, *, bsz, tl):
    w = w_z.shape[1]
    cdim = w_xbc.shape[1]
    rep = lambda v: jnp.repeat(v, M2_HEAD_DIM).reshape(1, w)
    consts = (_row(norm_g), w_z, w_xbc, jnp.repeat(w_dt, M2_HEAD_DIM, axis=1), conv_w, _row(conv_b), rep(dt_bias),
              rep(a_log), rep(d_skip), _row(out_norm_g))
    scratch = [pltpu.VMEM((tl + SUBLANES, cdim), F32), pltpu.VMEM((SUBLANES, cdim), F32),
               pltpu.VMEM((M2_GROUPS, M2_STATE, w // M2_GROUPS), F32)]
    return _mixer_call(_ssd_body, "ssd", h, consts, scratch, bsz=bsz, tl=tl, width=w)


def _l2norm(x):
    return x * lax.rsqrt(jnp.sum(x * x, axis=-1, keepdims=True) + EPS)


def _gdn_body(h_ref, g_ref, wqkv_ref, wg_ref, wb_ref, wa_ref, cw_ref, dtb_ref, alog_ref, ng_ref, o_ref,
              xpad_scr, tail_scr, state_scr):
    l, w = o_ref.shape
    nheads, dk, dv = state_scr.shape
    q = GDN_CHUNK
    sq = nheads * q
    nchunks = l // q
    first = pl.program_id(1) == 0

    @pl.when(first)
    def _():
        state_scr[...] = jnp.zeros_like(state_scr)

    u = _rms(h_ref[...], g_ref[...]).astype(BF16)
    qkv_raw = jnp.dot(u, wqkv_ref[...], preferred_element_type=F32)
    gate = jnp.dot(u, wg_ref[...], preferred_element_type=F32)
    beta = jax.nn.sigmoid(jnp.dot(u, wb_ref[...], preferred_element_type=F32))
    a_raw = jnp.dot(u, wa_ref[...], preferred_element_type=F32)
    glog = -jnp.exp(alog_ref[...]) * jax.nn.softplus(a_raw + dtb_ref[...])
    qkv = _silu(_causal_conv(qkv_raw, xpad_scr, tail_scr, cw_ref, first))

    tri = _lower_tri(q).astype(BF16)
    gcs_c = [_split_dot(tri, glog[c * q:(c + 1) * q]) for c in range(nchunks)]
    gcs = jnp.concatenate(gcs_c, axis=0)
    gtot = jnp.concatenate([jnp.broadcast_to(g_[q - 1:q, :], (q, w)) for g_ in gcs_c], axis=0)
    egcs = jnp.exp(gcs)

    def stack(x, c):
        return jnp.concatenate([x[c * q:(c + 1) * q, hd * dk:(hd + 1) * dk] for hd in range(nheads)], axis=0)

    qn = jnp.concatenate([_l2norm(qkv[:, hd * dk:(hd + 1) * dk]) * (dk ** -0.5) for hd in range(nheads)], axis=1)
    kn = jnp.concatenate([_l2norm(qkv[:, w + hd * dk:w + (hd + 1) * dk]) for hd in range(nheads)], axis=1)
    kb = kn * beta
    qd = qn * egcs
    kd = kn * jnp.exp(gtot - gcs)
    kbe = kb * egcs
    vb = qkv[:, 2 * w:] * beta

    rs = lax.broadcasted_iota(jnp.int32, (sq, sq), 0)
    cs = lax.broadcasted_iota(jnp.int32, (sq, sq), 1)
    same_head = (rs // q) == (cs // q)
    causal = same_head & (rs >= cs)
    strict = same_head & (rs > cs)
    eye = (rs == cs).astype(F32)
    head_block = (lax.broadcasted_iota(jnp.int32, (sq, nheads * dv), 0) // q
                  == lax.broadcasted_iota(jnp.int32, (sq, nheads * dv), 1) // dv)

    states = [state_scr[hd] for hd in range(nheads)]
    for c0 in range(0, nchunks, GDN_INTERLEAVE):
        cg = list(range(c0, min(c0 + GDN_INTERLEAVE, nchunks)))
        k_s = [stack(kn, c) for c in cg]
        g_s = [stack(gcs, c) for c in cg]
        g_t = jnp.concatenate(g_s, axis=0).T
        decay = []
        for i in range(len(cg)):
            diff = jnp.concatenate([g_s[i]] * (sq // dk), axis=1) - g_t[0:1, i * sq:(i + 1) * sq]
            decay.append(jnp.where(causal, jnp.exp(jnp.where(causal, diff, 0.0)), 0.0))
        kk = [_mm_nt(stack(kb, c), k_s[i]) for i, c in enumerate(cg)]
        qk = [_mm_nt(stack(qn, c), k_s[i]) * decay[i] for i, c in enumerate(cg)]
        pw = [jnp.where(strict, -(kk[i] * decay[i]), 0.0) for i in range(len(cg))]
        inv = [eye + p_ for p_ in pw]
        span = 2
        while span < q:
            pw = [_mm(p_, p_) for p_ in pw]
            inv = [a_ + _mm(a_, p_) for a_, p_ in zip(inv, pw)]
            span *= 2
        sol = [_mm(inv[i], jnp.concatenate([stack(vb, c), stack(kbe, c)], axis=1)) for i, c in enumerate(cg)]
        kd_t = jnp.concatenate([stack(kd, c) for c in cg], axis=0).T

        for i, c in enumerate(cg):
            rows = slice(c * q, (c + 1) * q)
            qd_s = stack(qd, c)
            r = [_mm(jnp.concatenate([sol[i][hd * q:(hd + 1) * q, dv:], qd_s[hd * q:(hd + 1) * q]], axis=0), states[hd])
                 for hd in range(nheads)]
            v_new = jnp.concatenate([sol[i][hd * q:(hd + 1) * q, :dv] - r[hd][:q] for hd in range(nheads)], axis=0)
            o_s = jnp.concatenate([r[hd][q:] for hd in range(nheads)], axis=0) + _mm(qk[i], v_new)
            v_bd = jnp.where(head_block, jnp.concatenate([v_new] * nheads, axis=1), 0.0)
            upd = _mm(kd_t[:, i * sq:(i + 1) * sq], v_bd)
            cdec = jnp.exp(gtot[c * q:c * q + 1, :])
            for hd in range(nheads):
                hl = slice(hd * dk, (hd + 1) * dk)
                states[hd] = states[hd] * cdec[:, hl] + upd[:, hl]
                o_ref[rows, hl] = (_rms(o_s[hd * q:(hd + 1) * q], ng_ref[...]) * _silu(gate[rows, hl])).astype(BF16)
    for hd in range(nheads):
        state_scr[hd] = states[hd]


def _gdn_mixer(h, norm_g, w_qkv, w_g, w_beta, w_a, conv_w, dt_bias, a_log, out_norm_g, *, bsz, tl):
    w = w_g.shape[1]
    cdim = w_qkv.shape[1]
    rep = lambda v: jnp.repeat(v, GDN_HEAD_DIM).reshape(1, w)
    consts = (_row(norm_g), w_qkv, w_g, jnp.repeat(w_beta, GDN_HEAD_DIM, axis=1),
              jnp.repeat(w_a, GDN_HEAD_DIM, axis=1), conv_w, rep(dt_bias), rep(a_log), _row(out_norm_g))
    scratch = [pltpu.VMEM((tl + SUBLANES, cdim), F32), pltpu.VMEM((SUBLANES, cdim), F32),
               pltpu.VMEM((w // GDN_HEAD_DIM, GDN_HEAD_DIM, GDN_HEAD_DIM), F32)]
    return _mixer_call(_gdn_body, "gdn", h, consts, scratch, bsz=bsz, tl=tl, width=w)


def _split_cols(t, sizes):
    out, start = [], 0
    for s in sizes:
        out.append(t[:, start:start + s])
        start += s
    return out


def kernel(x, p, ffn1_norm, ffn1_w_in, ffn1_w_out, mix_norm, w_in, w_gate, b_gate, s5_log_step, s5_a_re, s5_a_im, s5_b_re, s5_b_im, s5_c_re, s5_c_im, s5_d, s5_w_glu, s5_b_glu, lru_conv_w, lru_conv_b, lru_w_r, lru_b_r, lru_w_i, lru_b_i, lru_lambda, m2_conv_w, m2_conv_b, m2_dt_bias, m2_a_log, m2_d, m2_norm, gdn_conv_w, gdn_dt_bias, gdn_a_log, gdn_norm, w_branch, w_out, ffn2_norm, ffn2_w_in, ffn2_w_out, ple_norm, ple_w_gate, ple_w_proj, final_norm):
    bsz, s, d = x.shape
    depth = p.shape[0]
    t = bsz * s
    h = x.reshape(t, d)
    tm, tl = 512, 512
    for i in range(depth):
        h = _ffn(h, ffn1_norm[i], ffn1_w_in[i].astype(BF16), ffn1_w_out[i].astype(BF16), tm=tm)
        (w_s5, w_lx, w_lg, w_mz, w_mx, w_mdt, w_gqkv, w_gg, w_gb, w_ga) = _split_cols(w_in[i].astype(BF16), IN_SPLITS)
        y_a = _s5_mixer(h, mix_norm[i], w_s5, s5_log_step[i], s5_a_re[i], s5_a_im[i], s5_b_re[i], s5_b_im[i],
                        s5_c_re[i], s5_c_im[i], s5_d[i], s5_w_glu[i], s5_b_glu[i], bsz=bsz, tl=tl)
        y_b = _lru_mixer(h, mix_norm[i], w_lx, w_lg, lru_conv_w[i], lru_conv_b[i], lru_w_r[i], lru_b_r[i],
                         lru_w_i[i], lru_b_i[i], lru_lambda[i], bsz=bsz, tl=tl)
        y_c = _ssd_mixer(h, mix_norm[i], w_mz, w_mx, w_mdt, m2_conv_w[i], m2_conv_b[i], m2_dt_bias[i],
                         m2_a_log[i], m2_d[i], m2_norm[i], bsz=bsz, tl=tl)
        y_d = _gdn_mixer(h, mix_norm[i], w_gqkv, w_gg, w_gb, w_ga, gdn_conv_w[i], gdn_dt_bias[i], gdn_a_log[i],
                         gdn_norm[i], bsz=bsz, tl=tl)
        h = _merge(h, mix_norm[i], (y_a, y_b, y_c, y_d), w_gate[i].astype(BF16), b_gate[i],
                   w_branch[i].astype(BF16), w_out[i].astype(BF16), tm=tm)
        h = _ffn(h, ffn2_norm[i], ffn2_w_in[i].astype(BF16), ffn2_w_out[i].astype(BF16), tm=tm)
        h = _ple(h, ple_norm[i], p[i].reshape(t, -1), ple_w_gate[i].astype(BF16), ple_w_proj[i].astype(BF16),
                 final_norm, tm=tm, final_norm=(i == depth - 1))
    return h.reshape(bsz, s, d)
```

```python
import functools

import jax
import jax.numpy as jnp
from jax import lax
from jax.experimental import pallas as pl
from jax.experimental.pallas import tpu as pltpu

EPS = 1e-6
BF16 = jnp.bfloat16
F32 = jnp.float32

V7X_VMEM_BYTES = 64 * 1024 * 1024
VMEM_LIMIT_BYTES = V7X_VMEM_BYTES - 8 * 1024 * 1024
SUBLANES = 8
LANES = 128

BRANCH_WIDTH = 512
S5_GROUP_CH = 16
S5_STATE = 64
S5_CHUNK = 16
S5_PAIR = 2
S5_PREP_GROUPS = 8
LRU_C = 8.0
M2_HEAD_DIM = 64
M2_HEADS = BRANCH_WIDTH // M2_HEAD_DIM
M2_GROUPS = 2
M2_STATE = 128
M2_CONV_DIM = BRANCH_WIDTH + 2 * M2_GROUPS * M2_STATE
M2_CHUNK = 128
GDN_HEAD_DIM = 128
GDN_HEADS = BRANCH_WIDTH // GDN_HEAD_DIM
GDN_CHUNK = 64
GDN_INTERLEAVE = 4
MXU_WIDTH = 256
IN_SPLITS = (BRANCH_WIDTH, BRANCH_WIDTH, BRANCH_WIDTH, BRANCH_WIDTH, M2_CONV_DIM, M2_HEADS, 3 * BRANCH_WIDTH, BRANCH_WIDTH, GDN_HEADS, GDN_HEADS)


def _rms(x, g):
    return x * lax.rsqrt(jnp.mean(x * x, axis=-1, keepdims=True) + EPS) * g


def _silu(x):
    return x * jax.nn.sigmoid(x)


def _mm(a, b):
    return jnp.dot(a.astype(BF16), b.astype(BF16), preferred_element_type=F32)


def _mm_nt(a, b):
    return lax.dot_general(a.astype(BF16), b.astype(BF16), (((1,), (1,)), ((), ())), preferred_element_type=F32)


def _const_spec(shape):
    return pl.BlockSpec(shape, lambda *_: (0,) * len(shape))


def _row(v):
    return v.reshape(1, -1)


def _seq_params():
    return pltpu.CompilerParams(dimension_semantics=("parallel", "arbitrary"), vmem_limit_bytes=VMEM_LIMIT_BYTES)


def _ffn_body(h_ref, g_ref, wi_ref, wo_ref, o_ref):
    f = wo_ref.shape[0]
    h = h_ref[...]
    xn = _rms(h, g_ref[...]).astype(BF16)
    acts = []
    for s in range(0, f, MXU_WIDTH):
        gate = jnp.dot(xn, wi_ref[:, s:s + MXU_WIDTH], preferred_element_type=F32)
        up = jnp.dot(xn, wi_ref[:, f + s:f + s + MXU_WIDTH], preferred_element_type=F32)
        acts.append((_silu(gate) * up).astype(BF16))
    o_ref[...] = h + 0.5 * jnp.dot(jnp.concatenate(acts, axis=1), wo_ref[...], preferred_element_type=F32)


def _layer_spec(stacked, layer):
    shape = stacked.shape[1:]
    return pl.BlockSpec((None,) + shape, lambda *_: (layer,) + (0,) * len(shape), pipeline_mode=pl.Buffered(1))


def _tile_params():
    return pltpu.CompilerParams(dimension_semantics=("parallel",), vmem_limit_bytes=VMEM_LIMIT_BYTES)


def _ffn(h, norm_g, w_in, w_out, *, layer, tm):
    t, d = h.shape
    return pl.pallas_call(
        _ffn_body,
        out_shape=jax.ShapeDtypeStruct((t, d), F32),
        grid=(t // tm,),
        in_specs=[pl.BlockSpec((tm, d), lambda i: (i, 0)), _const_spec((1, d)),
                  _layer_spec(w_in, layer), _layer_spec(w_out, layer)],
        out_specs=pl.BlockSpec((tm, d), lambda i: (i, 0)),
        compiler_params=_tile_params(),
        name="ffn",
    )(h, _row(norm_g), w_in, w_out)


def _merge_body(h_ref, g_ref, ys5_ref, su_ref, dsk_ref, wglu_ref, bglu_ref, yb_ref, yc_ref, yd_ref, wg_ref, bg_ref,
                wb_ref, wo_ref, o_ref):
    h = h_ref[...]
    d = h.shape[-1]
    u = _rms(h, g_ref[...]).astype(BF16)
    z = jax.nn.gelu(ys5_ref[...] + dsk_ref[...] * su_ref[...])
    ya = (z * jax.nn.sigmoid(_mm(z, wglu_ref[...]) + bglu_ref[...])).astype(BF16)
    mixed = jnp.zeros(h.shape, F32)
    for n, y in enumerate((ya, yb_ref[...], yc_ref[...], yd_ref[...])):
        gate = jax.nn.sigmoid(
            jnp.dot(u, wg_ref[:, n * d:(n + 1) * d], preferred_element_type=F32) + bg_ref[:, n * d:(n + 1) * d])
        mixed = mixed + gate * jnp.dot(y, wb_ref[n], preferred_element_type=F32)
    o_ref[...] = h + _mm(mixed, wo_ref[...])


def _merge(h, norm_g, y_s5, su, d_skip, w_glu, b_glu, ys, w_gate, b_gate, w_branch, w_out, *, layer, tm):
    t, d = h.shape
    w = y_s5.shape[1]
    y_spec = pl.BlockSpec((tm, w), lambda i: (i, 0))
    return pl.pallas_call(
        _merge_body,
        out_shape=jax.ShapeDtypeStruct((t, d), F32),
        grid=(t // tm,),
        in_specs=[
            pl.BlockSpec((tm, d), lambda i: (i, 0)), _const_spec((1, d)),
            y_spec, y_spec, _const_spec((1, w)), _layer_spec(w_glu, layer), _const_spec((1, w)),
            y_spec, y_spec, y_spec,
            _layer_spec(w_gate, layer), _const_spec((1, b_gate.shape[-1])),
            _layer_spec(w_branch, layer), _layer_spec(w_out, layer),
        ],
        out_specs=pl.BlockSpec((tm, d), lambda i: (i, 0)),
        compiler_params=_tile_params(),
        name="merge",
    )(h, _row(norm_g), y_s5, su, _row(d_skip), w_glu, _row(b_glu), *ys, w_gate, _row(b_gate), w_branch, w_out)


def _ple_body(h_ref, g_ref, p_ref, wg_ref, wp_ref, fg_ref, o_ref, *, final_norm):
    h = h_ref[...]
    gate = jax.nn.sigmoid(_mm(_rms(h, g_ref[...]), wg_ref[...]))
    out = h + gate * _mm(p_ref[...], wp_ref[...])
    if final_norm:
        out = _rms(out, fg_ref[...])
    o_ref[...] = out


def _ple(h, norm_g, p, w_gate, w_proj, final_g, *, layer, tm, final_norm):
    t, d = h.shape
    pd = p.shape[-1]
    return pl.pallas_call(
        functools.partial(_ple_body, final_norm=final_norm),
        out_shape=jax.ShapeDtypeStruct((t, d), F32),
        grid=(t // tm,),
        in_specs=[
            pl.BlockSpec((tm, d), lambda i: (i, 0)), _const_spec((1, d)),
            pl.BlockSpec((None, tm, pd), lambda i: (layer, i, 0)),
            _layer_spec(w_gate, layer), _layer_spec(w_proj, layer), _const_spec((1, d)),
        ],
        out_specs=pl.BlockSpec((tm, d), lambda i: (i, 0)),
        compiler_params=_tile_params(),
        name="ple",
    )(h, _row(norm_g), p, w_gate, w_proj, _row(final_g))


def _segment_perm(l):
    r = jnp.arange(l)
    src = (r % SUBLANES) * (l // SUBLANES) + r // SUBLANES
    return (src[:, None] == jnp.arange(l)[None, :]).astype(BF16)


def _mixer_call(body, name, h, consts, scratch_shapes, *, bsz, tl, width, out_dtypes=(BF16,)):
    t, d = h.shape
    nt = t // bsz // tl
    outs = pl.pallas_call(
        body,
        out_shape=tuple(jax.ShapeDtypeStruct((t, width), dt) for dt in out_dtypes),
        grid=(bsz, nt),
        in_specs=[pl.BlockSpec((tl, d), lambda b, i: (b * nt + i, 0))] + [_const_spec(c.shape) for c in consts],
        out_specs=tuple(pl.BlockSpec((tl, width), lambda b, i: (b * nt + i, 0)) for _ in out_dtypes),
        scratch_shapes=scratch_shapes,
        compiler_params=_seq_params(),
        name=name,
    )(h, *consts)
    return outs[0] if len(out_dtypes) == 1 else outs


def _conv_halo(cur_tail, prev_tail):
    row = lax.broadcasted_iota(jnp.int32, cur_tail.shape, 0) % SUBLANES
    slab = jnp.where(row == SUBLANES - 1, prev_tail, cur_tail)
    n = cur_tail.shape[0] // SUBLANES
    return jnp.concatenate(
        [pltpu.roll(slab[g * SUBLANES:(g + 1) * SUBLANES], 1, 0) for g in range(n)], axis=0)


def _lru_body(h_ref, g_ref, p_ref, pt_ref, wx_ref, wg_ref, ws5_ref, cw_ref, cb_ref, wr_ref, br_ref, wi_ref, bi_ref,
              lam_ref, o_ref, su_ref, a_scr, b_scr, state_scr, tail_scr):
    l, w = a_scr.shape
    ls = l // SUBLANES
    kconv = cw_ref.shape[0]
    halo = (kconv - 1) * SUBLANES

    @pl.when(pl.program_id(1) == 0)
    def _():
        state_scr[...] = jnp.zeros_like(state_scr)
        tail_scr[...] = jnp.zeros_like(tail_scr)

    u = _rms(h_ref[...], g_ref[...]).astype(BF16)
    su_ref[...] = jnp.dot(u, ws5_ref[...], preferred_element_type=F32)
    up = jnp.dot(p_ref[...], u, preferred_element_type=F32).astype(BF16)
    x = jnp.dot(up, wx_ref[...], preferred_element_type=F32)
    gate = jnp.dot(up, wg_ref[...], preferred_element_type=F32)

    cur_tail = x[l - halo:, :]
    xpad = jnp.concatenate([_conv_halo(cur_tail, tail_scr[...]), x], axis=0)
    tail_scr[...] = cur_tail
    xc = cb_ref[...] + sum(cw_ref[k:k + 1, :] * xpad[k * SUBLANES:k * SUBLANES + l] for k in range(kconv))

    r = jax.nn.sigmoid(_mm(xc, wr_ref[...]) + br_ref[...])
    ig = jax.nn.sigmoid(_mm(xc, wi_ref[...]) + bi_ref[...])
    log_a = (-LRU_C * jax.nn.softplus(-lam_ref[...])) * r
    a_scr[...] = jnp.exp(log_a)
    b_scr[...] = jnp.sqrt(-jnp.tanh(log_a) * (jnp.exp(2.0 * log_a) + 1.0)) * (ig * xc)

    def step(i, carry):
        hh, pp = carry
        rows = pl.ds(pl.multiple_of(i * SUBLANES, SUBLANES), SUBLANES)
        a_i = a_scr[rows, :]
        hh = a_i * hh + b_scr[rows, :]
        pp = a_i * pp
        b_scr[rows, :] = hh
        a_scr[rows, :] = pp
        return hh, pp

    h_end, p_end = lax.fori_loop(0, ls, step, (jnp.zeros((SUBLANES, w), F32), jnp.ones((SUBLANES, w), F32)),
                                 unroll=4)
    c = state_scr[...]
    carries = []
    for j in range(SUBLANES):
        carries.append(c)
        c = h_end[j:j + 1, :] + p_end[j:j + 1, :] * c
    state_scr[...] = c
    cmat = jnp.concatenate(carries, axis=0)
    hfull = b_scr[...].reshape(ls, SUBLANES, w) + a_scr[...].reshape(ls, SUBLANES, w) * cmat[None]
    out = (hfull.reshape(l, w) * jax.nn.gelu(gate)).astype(BF16)
    o_ref[...] = jnp.dot(pt_ref[...], out, preferred_element_type=F32).astype(BF16)


def _lru_mixer(h, norm_g, w_x, w_g, w_s5, conv_w, conv_b, w_r, b_r, w_i, b_i, lam, *, bsz, tl):
    w = w_x.shape[1]
    perm = _segment_perm(tl)
    blockdiag = lambda m: jax.scipy.linalg.block_diag(*m).astype(BF16)
    consts = (_row(norm_g), perm, perm.T, w_x, w_g, w_s5, conv_w, _row(conv_b), blockdiag(w_r), _row(b_r),
              blockdiag(w_i), _row(b_i), _row(lam))
    scratch = [pltpu.VMEM((tl, w), F32), pltpu.VMEM((tl, w), F32), pltpu.VMEM((1, w), F32),
               pltpu.VMEM(((conv_w.shape[0] - 1) * SUBLANES, w), F32)]
    return _mixer_call(_lru_body, "rglru", h, consts, scratch, bsz=bsz, tl=tl, width=w, out_dtypes=(BF16, F32))


def _cmul(ar, ai, br, bi):
    return ar * br - ai * bi, ar * bi + ai * br


def _split3(x):
    hi = x.astype(BF16)
    r1 = x - hi.astype(F32)
    mid = r1.astype(BF16)
    lo = (r1 - mid.astype(F32)).astype(BF16)
    return hi, mid, lo


def _mm_exact_rhs(x, e):
    return sum(jnp.dot(part, e, preferred_element_type=F32) for part in _split3(x))


def _bmm_f32(a, b):
    a3, b3 = _split3(a), _split3(b)
    acc = None
    for i, ap in enumerate(a3):
        for j, bp in enumerate(b3):
            if i + j <= 2:
                t = jnp.einsum('gij,gjk->gik', ap, bp, preferred_element_type=F32)
                acc = t if acc is None else acc + t
    return acc


def _s5_prep_body(ls_ref, arow_ref, acol_ref, bt_ref, crep_ref, kt_ref, bst_ref, cst_ref, pw_ref):
    gb, _, p = arow_ref.shape
    ch = S5_GROUP_CH
    q = S5_CHUNK
    npw = pw_ref.shape[2]
    step = jnp.exp(ls_ref[...])
    a_re, a_im = arow_ref[:, 0:1, :], arow_ref[:, 1:2, :]
    sa_re, sa_im = a_re * step, a_im * step
    mag = jnp.exp(sa_re)
    ab_re, ab_im = mag * jnp.cos(sa_im), mag * jnp.sin(sa_im)
    den = a_re * a_re + a_im * a_im
    num_re = ab_re - 1.0
    f_re = (num_re * a_re + ab_im * a_im) / den
    f_im = (ab_im * a_re - num_re * a_im) / den
    bt_re, bt_im = bt_ref[:, 0], bt_ref[:, 1]
    bb_re = f_re * bt_re - f_im * bt_im
    bb_im = f_re * bt_im + f_im * bt_re
    ns = (q - 1 - lax.broadcasted_iota(jnp.int32, (1, q, 1), 1)).astype(F32)
    pmag = jnp.exp(ns * sa_re)
    pb_re, pb_im = pmag * jnp.cos(ns * sa_im), pmag * jnp.sin(ns * sa_im)
    pb_re = jnp.broadcast_to(pb_re[:, :, None, :], (gb, q, ch, p)).reshape(gb, q * ch, p)
    pb_im = jnp.broadcast_to(pb_im[:, :, None, :], (gb, q, ch, p)).reshape(gb, q * ch, p)
    bbr = jnp.concatenate([bb_re] * q, axis=1)
    bbi = jnp.concatenate([bb_im] * q, axis=1)
    sr, si = _cmul(bbr, bbi, pb_re, pb_im)
    bst_ref[:, 0] = sr
    bst_ref[:, 1] = si
    sc_re, sc_im = acol_ref[:, 0] * step, acol_ref[:, 1] * step
    nd = lax.broadcasted_iota(jnp.int32, (1, 1, q), 2).astype(F32)
    cmag = jnp.exp(sc_re * nd)
    pc_re, pc_im = cmag * jnp.cos(sc_im * nd), cmag * jnp.sin(sc_im * nd)
    expand = (lax.broadcasted_iota(jnp.int32, (q, q * ch), 0)
              == lax.broadcasted_iota(jnp.int32, (q, q * ch), 1) // ch).astype(BF16)
    pc_re = _mm_exact_rhs(pc_re.reshape(gb * p, q), expand).reshape(gb, p, q * ch)
    pc_im = _mm_exact_rhs(pc_im.reshape(gb * p, q), expand).reshape(gb, p, q * ch)
    ca_re, ca_im = _cmul(crep_ref[:, 0], crep_ref[:, 1], pc_re, pc_im)
    kt_ref[...] = _bmm_f32(bb_re, ca_re) - _bmm_f32(bb_im, ca_im)
    abc_mag = jnp.exp(sc_re)
    o_re, o_im = _cmul(ca_re, ca_im, abc_mag * jnp.cos(sc_im), abc_mag * jnp.sin(sc_im))
    cst_ref[:, 0] = o_re
    cst_ref[:, 1] = -o_im
    nn = lax.broadcasted_iota(jnp.int32, (1, npw, 1), 1).astype(F32) * float(q)
    wmag = jnp.exp(nn * sa_re)
    pw_ref[:, 0] = wmag * jnp.cos(nn * sa_im)
    pw_ref[:, 1] = wmag * jnp.sin(nn * sa_im)


def _s5_prep(log_step, a_re, a_im, b_re, b_im, c_re, c_im, *, nslab):
    g, p = a_re.shape
    ch, q = S5_GROUP_CH, S5_CHUNK
    gb = S5_PREP_GROUPS
    arow = jnp.stack([a_re, a_im], axis=1)
    acol = arow[..., None]
    bt = jnp.stack([b_re, b_im], axis=1).transpose(0, 1, 3, 2)
    crep = jnp.tile(jnp.stack([c_re, c_im], axis=1).transpose(0, 1, 3, 2), (1, 1, 1, q))
    blk = lambda *shape: pl.BlockSpec((gb,) + shape, lambda i: (i,) + (0,) * len(shape))
    npw = nslab + SUBLANES
    return pl.pallas_call(
        _s5_prep_body,
        out_shape=(jax.ShapeDtypeStruct((g, ch, q * ch), F32), jax.ShapeDtypeStruct((g, 2, q * ch, p), F32),
                   jax.ShapeDtypeStruct((g, 2, p, q * ch), F32), jax.ShapeDtypeStruct((g, 2, npw, p), F32)),
        grid=(g // gb,),
        in_specs=[blk(1, 1), blk(2, p), blk(2, p, 1), blk(2, ch, p), blk(2, p, q * ch)],
        out_specs=(blk(ch, q * ch), blk(2, q * ch, p), blk(2, p, q * ch), blk(2, npw, p)),
        compiler_params=_tile_params(),
        name="s5_prep",
    )(log_step.reshape(g, 1, 1), arow, acol, bt, crep)


def _toeplitz(kt):
    g = kt.shape[0]
    ch, q = S5_GROUP_CH, S5_CHUNK
    k4 = kt.reshape(g, ch, q, ch)
    s = jnp.arange(q)[:, None]
    t = jnp.arange(q)[None, :]
    d = jnp.clip(t - s, 0, q - 1)
    full = k4[:, :, d, :]
    full = jnp.where((t >= s)[None, None, :, :, None], full, 0.0)
    return full.transpose(0, 2, 1, 3, 4).reshape(g, q * ch, q * ch)


def _pair_rows(x):
    g, two, r, p = x.shape
    xp = x.reshape(g // S5_PAIR, S5_PAIR, two, r, p)
    eye = jnp.eye(S5_PAIR, dtype=x.dtype)
    return jnp.einsum('nakrp,ab->nkarbp', xp, eye).reshape(g // S5_PAIR, two, S5_PAIR * r, S5_PAIR * p)


def _s5_core_body(z_ref, toep_ref, bst_ref, cst_ref, pw_ref, o_ref, gre_scr, gim_scr, *, bsz):
    rows, lanes = o_ref.shape[1:]
    w = lanes // S5_PAIR
    ns = gre_scr.shape[1]
    rb = rows // bsz
    nslab = rb // SUBLANES
    z = z_ref[0]
    y = jnp.concatenate([jnp.dot(z[:, k * w:(k + 1) * w], toep_ref[k], preferred_element_type=F32)
                         for k in range(S5_PAIR)], axis=1)
    gre_scr[...] = jnp.dot(z, bst_ref[0, 0], preferred_element_type=F32)
    gim_scr[...] = jnp.dot(z, bst_ref[0, 1], preferred_element_type=F32)
    ar = jnp.broadcast_to(pw_ref[0, 0, 1:2, :], (SUBLANES, ns))
    ai = jnp.broadcast_to(pw_ref[0, 1, 1:2, :], (SUBLANES, ns))

    def step(i, carry):
        out = []
        for b in range(bsz):
            er, ei = carry[b]
            slab = pl.ds(pl.multiple_of(b * rb + i * SUBLANES, SUBLANES), SUBLANES)
            pr, pi = _cmul(ar, ai, er, ei)
            er = pr + gre_scr[slab, :]
            ei = pi + gim_scr[slab, :]
            gre_scr[slab, :] = er
            gim_scr[slab, :] = ei
            out.append((er, ei))
        return tuple(out)

    zero = jnp.zeros((SUBLANES, ns), F32)
    ends = lax.fori_loop(0, nslab, step, tuple((zero, zero) for _ in range(bsz)), unroll=8)

    alr, ali = pw_ref[0, 0, nslab:nslab + 1, :], pw_ref[0, 1, nslab:nslab + 1, :]
    pwr = pw_ref[0, 0, :nslab, :][:, None, :]
    pwi = pw_ref[0, 1, :nslab, :][:, None, :]
    sprev_re, sprev_im = [], []
    for b in range(bsz):
        er, ei = ends[b]
        cr = jnp.zeros((1, ns), F32)
        ci = jnp.zeros((1, ns), F32)
        crs, cis = [], []
        for j in range(SUBLANES):
            crs.append(cr)
            cis.append(ci)
            pr, pi = _cmul(alr, ali, cr, ci)
            cr, ci = er[j:j + 1] + pr, ei[j:j + 1] + pi
        cmr = jnp.concatenate(crs, axis=0)[None]
        cmi = jnp.concatenate(cis, axis=0)[None]
        lr = jnp.concatenate([zero, gre_scr[b * rb:(b + 1) * rb - SUBLANES, :]], axis=0).reshape(nslab, SUBLANES, ns)
        li = jnp.concatenate([zero, gim_scr[b * rb:(b + 1) * rb - SUBLANES, :]], axis=0).reshape(nslab, SUBLANES, ns)
        fr, fi = _cmul(pwr, pwi, cmr, cmi)
        sprev_re.append((lr + fr).reshape(rb, ns))
        sprev_im.append((li + fi).reshape(rb, ns))
    sre = jnp.concatenate(sprev_re, axis=0).astype(BF16)
    sim = jnp.concatenate(sprev_im, axis=0).astype(BF16)
    o_ref[0] = (y + jnp.dot(sre, cst_ref[0, 0], preferred_element_type=F32)
                + jnp.dot(sim, cst_ref[0, 1], preferred_element_type=F32))


def _s5_core(su, log_step, a_re, a_im, b_re, b_im, c_re, c_im, *, bsz):
    t, wtot = su.shape
    g = a_re.shape[0]
    ch, q = S5_GROUP_CH, S5_CHUNK
    npair = g // S5_PAIR
    s = t // bsz
    nslab = s // q // SUBLANES
    rows = bsz * nslab * SUBLANES
    kt, bst, cst, pw = _s5_prep(log_step, a_re, a_im, b_re, b_im, c_re, c_im, nslab=nslab)
    toep = _toeplitz(kt).astype(BF16)
    bst_p = _pair_rows(bst).astype(BF16)
    cst_p = _pair_rows(cst).astype(BF16)
    pw_p = pw.reshape(npair, S5_PAIR, 2, -1, S5_STATE).transpose(0, 2, 3, 1, 4).reshape(npair, 2, -1, S5_PAIR * S5_STATE)
    z = su.reshape(bsz, SUBLANES, nslab, q, npair, S5_PAIR, ch).transpose(4, 0, 2, 1, 5, 3, 6)
    z = z.reshape(npair, rows, S5_PAIR * q * ch).astype(BF16)
    lanes = S5_PAIR * q * ch
    blk = lambda *shape: pl.BlockSpec((1,) + shape, lambda i: (i,) + (0,) * len(shape))
    y = pl.pallas_call(
        functools.partial(_s5_core_body, bsz=bsz),
        out_shape=jax.ShapeDtypeStruct((npair, rows, lanes), F32),
        grid=(npair,),
        in_specs=[blk(rows, lanes),
                  pl.BlockSpec((S5_PAIR, q * ch, q * ch), lambda i: (i, 0, 0)),
                  blk(2, lanes, S5_PAIR * S5_STATE), blk(2, S5_PAIR * S5_STATE, lanes),
                  blk(2, nslab + SUBLANES, S5_PAIR * S5_STATE)],
        out_specs=blk(rows, lanes),
        scratch_shapes=[pltpu.VMEM((rows, S5_PAIR * S5_STATE), F32), pltpu.VMEM((rows, S5_PAIR * S5_STATE), F32)],
        compiler_params=_tile_params(),
        name="s5_core",
    )(z, toep, bst_p, cst_p, pw_p)
    y = y.reshape(npair, bsz, nslab, SUBLANES, S5_PAIR, q, ch).transpose(1, 3, 2, 5, 0, 4, 6)
    return y.reshape(t, wtot)


def _causal_conv(x, xpad_scr, tail_scr, cw_ref, first):
    l = x.shape[0]
    k = cw_ref.shape[0]

    @pl.when(first)
    def _():
        tail_scr[...] = jnp.zeros_like(tail_scr)

    xpad_scr[:SUBLANES, :] = tail_scr[...]
    xpad_scr[SUBLANES:, :] = x
    tail_scr[...] = x[l - SUBLANES:, :]
    off = SUBLANES - (k - 1)
    return sum(cw_ref[j:j + 1, :] * xpad_scr[off + j:off + j + l, :] for j in range(k))


def _split_dot(tri, x):
    hi = x.astype(BF16)
    lo = (x - hi.astype(F32)).astype(BF16)
    return jnp.dot(tri, hi, preferred_element_type=F32) + jnp.dot(tri, lo, preferred_element_type=F32)


def _lower_tri(q):
    return lax.broadcasted_iota(jnp.int32, (q, q), 0) >= lax.broadcasted_iota(jnp.int32, (q, q), 1)


def _decay_matrix(cs, cst, lane, causal):
    diff = cs[:, lane:lane + 1] - cst[lane:lane + 1, :]
    return jnp.where(causal, jnp.exp(jnp.where(causal, diff, 0.0)), 0.0)


def _ssd_body(h_ref, g_ref, wz_ref, wx_ref, wdt_ref, cw_ref, cb_ref, dtb_ref, alog_ref, dsk_ref, ng_ref, o_ref,
              xpad_scr, tail_scr, state_scr):
    l, w = o_ref.shape
    ngroups, nstate, gw = state_scr.shape
    q = M2_CHUNK
    first = pl.program_id(1) == 0

    @pl.when(first)
    def _():
        state_scr[...] = jnp.zeros_like(state_scr)

    u = _rms(h_ref[...], g_ref[...]).astype(BF16)
    z = jnp.dot(u, wz_ref[...], preferred_element_type=F32)
    xbc_raw = jnp.dot(u, wx_ref[...], preferred_element_type=F32)
    dt_raw = jnp.dot(u, wdt_ref[...], preferred_element_type=F32)
    xbc = _silu(_causal_conv(xbc_raw, xpad_scr, tail_scr, cw_ref, first) + cb_ref[...])
    xs = xbc[:, :w]
    bm = xbc[:, w:w + ngroups * nstate]
    cm = xbc[:, w + ngroups * nstate:]
    dt = jax.nn.softplus(dt_raw + dtb_ref[...])
    da = dt * (-jnp.exp(alog_ref[...]))
    xdt = xs * dt

    causal = _lower_tri(q)
    tri = causal.astype(BF16)
    head_of_lane = lax.broadcasted_iota(jnp.int32, (1, gw), 1) // M2_HEAD_DIM

    states = [state_scr[g] for g in range(ngroups)]
    for c in range(l // q):
        rows = slice(c * q, (c + 1) * q)
        cs = _split_dot(tri, da[rows])
        cs_last = cs[q - 1:q, :]
        dte = jnp.exp(cs_last - cs)
        ecs = jnp.exp(cs)
        cdec = jnp.exp(cs_last)
        cst = cs.T
        ychunk = []
        for g in range(ngroups):
            gl = slice(g * gw, (g + 1) * gw)
            bg = bm[rows, g * nstate:(g + 1) * nstate]
            cg = cm[rows, g * nstate:(g + 1) * nstate]
            cb = _mm_nt(cg, bg)
            xg = xdt[rows, gl]
            yg = _mm(cg, states[g]) * ecs[:, gl]
            for hh in range(gw // M2_HEAD_DIM):
                dec = _decay_matrix(cs, cst, g * gw + hh * M2_HEAD_DIM, causal)
                yg = yg + _mm(cb * dec, jnp.where(head_of_lane == hh, xg, 0.0))
            states[g] = states[g] * cdec[:, gl] + _mm(bg.T, xg * dte[:, gl])
            ychunk.append(yg)
        y = jnp.concatenate(ychunk, axis=1) + dsk_ref[...] * xs[rows]
        y = y * _silu(z[rows])
        o_ref[rows, :] = _rms(y, ng_ref[...]).astype(BF16)
    for g in range(ngroups):
        state_scr[g] = states[g]


def _ssd_mixer(h, norm_g, w_z, w_xbc, w_dt, conv_w, conv_b, dt_bias, a_log, d_skip, out_norm_g, *, bsz, tl):
    w = w_z.shape[1]
    cdim = w_xbc.shape[1]
    rep = lambda v: jnp.repeat(v, M2_HEAD_DIM).reshape(1, w)
    consts = (_row(norm_g), w_z, w_xbc, jnp.repeat(w_dt, M2_HEAD_DIM, axis=1), conv_w, _row(conv_b), rep(dt_bias),
              rep(a_log), rep(d_skip), _row(out_norm_g))
    scratch = [pltpu.VMEM((tl + SUBLANES, cdim), F32), pltpu.VMEM((SUBLANES, cdim), F32),
               pltpu.VMEM((M2_GROUPS, M2_STATE, w // M2_GROUPS), F32)]
    return _mixer_call(_ssd_body, "ssd", h, consts, scratch, bsz=bsz, tl=tl, width=w)


def _l2norm(x):
    return x * lax.rsqrt(jnp.sum(x * x, axis=-1, keepdims=True) + EPS)


def _gdn_body(h_ref, g_ref, wqkv_ref, wg_ref, wb_ref, wa_ref, cw_ref, dtb_ref, alog_ref, ng_ref, o_ref,
              xpad_scr, tail_scr, state_scr):
    l, w = o_ref.shape
    nheads, dk, dv = state_scr.shape
    q = GDN_CHUNK
    sq = nheads * q
    nchunks = l // q
    first = pl.program_id(1) == 0

    @pl.when(first)
    def _():
        state_scr[...] = jnp.zeros_like(state_scr)

    u = _rms(h_ref[...], g_ref[...]).astype(BF16)
    qkv_raw = jnp.dot(u, wqkv_ref[...], preferred_element_type=F32)
    gate = jnp.dot(u, wg_ref[...], preferred_element_type=F32)
    beta = jax.nn.sigmoid(jnp.dot(u, wb_ref[...], preferred_element_type=F32))
    a_raw = jnp.dot(u, wa_ref[...], preferred_element_type=F32)
    glog = -jnp.exp(alog_ref[...]) * jax.nn.softplus(a_raw + dtb_ref[...])
    qkv = _silu(_causal_conv(qkv_raw, xpad_scr, tail_scr, cw_ref, first))

    tri = _lower_tri(q).astype(BF16)
    gcs_c = [_split_dot(tri, glog[c * q:(c + 1) * q]) for c in range(nchunks)]
    gcs = jnp.concatenate(gcs_c, axis=0)
    gtot = jnp.concatenate([jnp.broadcast_to(g_[q - 1:q, :], (q, w)) for g_ in gcs_c], axis=0)
    egcs = jnp.exp(gcs)

    def stack(x, c):
        return jnp.concatenate([x[c * q:(c + 1) * q, hd * dk:(hd + 1) * dk] for hd in range(nheads)], axis=0)

    qn = jnp.concatenate([_l2norm(qkv[:, hd * dk:(hd + 1) * dk]) * (dk ** -0.5) for hd in range(nheads)], axis=1)
    kn = jnp.concatenate([_l2norm(qkv[:, w + hd * dk:w + (hd + 1) * dk]) for hd in range(nheads)], axis=1)
    kb = kn * beta
    qd = qn * egcs
    kd = kn * jnp.exp(gtot - gcs)
    kbe = kb * egcs
    vb = qkv[:, 2 * w:] * beta

    rs = lax.broadcasted_iota(jnp.int32, (sq, sq), 0)
    cs = lax.broadcasted_iota(jnp.int32, (sq, sq), 1)
    same_head = (rs // q) == (cs // q)
    causal = same_head & (rs >= cs)
    strict = same_head & (rs > cs)
    eye = (rs == cs).astype(F32)
    head_block = (lax.broadcasted_iota(jnp.int32, (sq, nheads * dv), 0) // q
                  == lax.broadcasted_iota(jnp.int32, (sq, nheads * dv), 1) // dv)

    states = [state_scr[hd] for hd in range(nheads)]
    for c0 in range(0, nchunks, GDN_INTERLEAVE):
        cg = list(range(c0, min(c0 + GDN_INTERLEAVE, nchunks)))
        k_s = [stack(kn, c) for c in cg]
        g_s = [stack(gcs, c) for c in cg]
        g_t = jnp.concatenate(g_s, axis=0).T
        decay = []
        for i in range(len(cg)):
            diff = jnp.concatenate([g_s[i]] * (sq // dk), axis=1) - g_t[0:1, i * sq:(i + 1) * sq]
            decay.append(jnp.where(causal, jnp.exp(jnp.where(causal, diff, 0.0)), 0.0))
        kk = [_mm_nt(stack(kb, c), k_s[i]) for i, c in enumerate(cg)]
        qk = [_mm_nt(stack(qn, c), k_s[i]) * decay[i] for i, c in enumerate(cg)]
        pw = [jnp.where(strict, -(kk[i] * decay[i]), 0.0) for i in range(len(cg))]
        inv = [eye + p_ for p_ in pw]
        span = 2
        while span < q:
            pw = [_mm(p_, p_) for p_ in pw]
            inv = [a_ + _mm(a_, p_) for a_, p_ in zip(inv, pw)]
            span *= 2
        sol = [_mm(inv[i], jnp.concatenate([stack(vb, c), stack(kbe, c)], axis=1)) for i, c in enumerate(cg)]
        kd_t = jnp.concatenate([stack(kd, c) for c in cg], axis=0).T

        for i, c in enumerate(cg):
            rows = slice(c * q, (c + 1) * q)
            qd_s = stack(qd, c)
            r = [_mm(jnp.concatenate([sol[i][hd * q:(hd + 1) * q, dv:], qd_s[hd * q:(hd + 1) * q]], axis=0), states[hd])
                 for hd in range(nheads)]
            v_new = jnp.concatenate([sol[i][hd * q:(hd + 1) * q, :dv] - r[hd][:q] for hd in range(nheads)], axis=0)
            o_s = jnp.concatenate([r[hd][q:] for hd in range(nheads)], axis=0) + _mm(qk[i], v_new)
            v_bd = jnp.where(head_block, jnp.concatenate([v_new] * nheads, axis=1), 0.0)
            upd = _mm(kd_t[:, i * sq:(i + 1) * sq], v_bd)
            cdec = jnp.exp(gtot[c * q:c * q + 1, :])
            for hd in range(nheads):
                hl = slice(hd * dk, (hd + 1) * dk)
                states[hd] = states[hd] * cdec[:, hl] + upd[:, hl]
                o_ref[rows, hl] = (_rms(o_s[hd * q:(hd + 1) * q], ng_ref[...]) * _silu(gate[rows, hl])).astype(BF16)
    for hd in range(nheads):
        state_scr[hd] = states[hd]


def _gdn_mixer(h, norm_g, w_qkv, w_g, w_beta, w_a, conv_w, dt_bias, a_log, out_norm_g, *, bsz, tl):
    w = w_g.shape[1]
    cdim = w_qkv.shape[1]
    rep = lambda v: jnp.repeat(v, GDN_HEAD_DIM).reshape(1, w)
    consts = (_row(norm_g), w_qkv, w_g, jnp.repeat(w_beta, GDN_HEAD_DIM, axis=1),
              jnp.repeat(w_a, GDN_HEAD_DIM, axis=1), conv_w, rep(dt_bias), rep(a_log), _row(out_norm_g))
    scratch = [pltpu.VMEM((tl + SUBLANES, cdim), F32), pltpu.VMEM((SUBLANES, cdim), F32),
               pltpu.VMEM((w // GDN_HEAD_DIM, GDN_HEAD_DIM, GDN_HEAD_DIM), F32)]
    return _mixer_call(_gdn_body, "gdn", h, consts, scratch, bsz=bsz, tl=tl, width=w)


def _col_slices(sizes):
    out, start = [], 0
    for size in sizes:
        out.append(slice(start, start + size))
        start += size
    return out


def kernel(x, p, ffn1_norm, ffn1_w_in, ffn1_w_out, mix_norm, w_in, w_gate, b_gate, s5_log_step, s5_a_re, s5_a_im, s5_b_re, s5_b_im, s5_c_re, s5_c_im, s5_d, s5_w_glu, s5_b_glu, lru_conv_w, lru_conv_b, lru_w_r, lru_b_r, lru_w_i, lru_b_i, lru_lambda, m2_conv_w, m2_conv_b, m2_dt_bias, m2_a_log, m2_d, m2_norm, gdn_conv_w, gdn_dt_bias, gdn_a_log, gdn_norm, w_branch, w_out, ffn2_norm, ffn2_w_in, ffn2_w_out, ple_norm, ple_w_gate, ple_w_proj, final_norm):
    bsz, s, d = x.shape
    depth = p.shape[0]
    t = bsz * s
    h = x.reshape(t, d)
    p = p.reshape(depth, t, -1)
    tm, tl = 512, 512
    bf = lambda a: a.astype(BF16)
    ffn1_w_in, ffn1_w_out, ffn2_w_in, ffn2_w_out = bf(ffn1_w_in), bf(ffn1_w_out), bf(ffn2_w_in), bf(ffn2_w_out)
    w_gate, w_branch, w_out, s5_w_glu = bf(w_gate), bf(w_branch), bf(w_out), bf(s5_w_glu)
    ple_w_gate, ple_w_proj = bf(ple_w_gate), bf(ple_w_proj)
    for i in range(depth):
        h = _ffn(h, ffn1_norm[i], ffn1_w_in, ffn1_w_out, layer=i, tm=tm)
        (w_s5, w_lx, w_lg, w_mz, w_mx, w_mdt, w_gqkv, w_gg, w_gb, w_ga) = [bf(w_in[i][:, c]) for c in _col_slices(IN_SPLITS)]
        y_b, su = _lru_mixer(h, mix_norm[i], w_lx, w_lg, w_s5, lru_conv_w[i], lru_conv_b[i], lru_w_r[i], lru_b_r[i],
                             lru_w_i[i], lru_b_i[i], lru_lambda[i], bsz=bsz, tl=tl)
        y_s5 = _s5_core(su, s5_log_step[i], s5_a_re[i], s5_a_im[i], s5_b_re[i], s5_b_im[i], s5_c_re[i], s5_c_im[i],
                        bsz=bsz)
        y_c = _ssd_mixer(h, mix_norm[i], w_mz, w_mx, w_mdt, m2_conv_w[i], m2_conv_b[i], m2_dt_bias[i],
                         m2_a_log[i], m2_d[i], m2_norm[i], bsz=bsz, tl=tl)
        y_d = _gdn_mixer(h, mix_norm[i], w_gqkv, w_gg, w_gb, w_ga, gdn_conv_w[i], gdn_dt_bias[i], gdn_a_log[i],
                         gdn_norm[i], bsz=bsz, tl=tl)
        h = _merge(h, mix_norm[i], y_s5, su, s5_d[i], s5_w_glu, s5_b_glu[i], (y_b, y_c, y_d), w_gate, b_gate[i],
                   w_branch, w_out, layer=i, tm=tm)
        h = _ffn(h, ffn2_norm[i], ffn2_w_in, ffn2_w_out, layer=i, tm=tm)
        h = _ple(h, ple_norm[i], p, ple_w_gate, ple_w_proj, final_norm, layer=i, tm=tm,
                 final_norm=(i == depth - 1))
    return h.reshape(bsz, s, d)
```

```python
import functools

import jax
import jax.numpy as jnp
from jax import lax
from jax.experimental import pallas as pl
from jax.experimental.pallas import tpu as pltpu

EPS = 1e-6
BF16 = jnp.bfloat16
F32 = jnp.float32

V7X_VMEM_BYTES = 64 * 1024 * 1024
VMEM_LIMIT_BYTES = V7X_VMEM_BYTES - 8 * 1024 * 1024
SUBLANES = 8
LANES = 128

BRANCH_WIDTH = 512
S5_GROUP_CH = 16
S5_STATE = 64
S5_CHUNK = 16
S5_OCT = LANES // S5_GROUP_CH
S5_PREP_GROUPS = 8
S5_NPOW = 16
LRU_C = 8.0
M2_HEAD_DIM = 64
M2_HEADS = BRANCH_WIDTH // M2_HEAD_DIM
M2_GROUPS = 2
M2_STATE = 128
M2_CONV_DIM = BRANCH_WIDTH + 2 * M2_GROUPS * M2_STATE
M2_CHUNK = 128
GDN_HEAD_DIM = 128
GDN_HEADS = BRANCH_WIDTH // GDN_HEAD_DIM
GDN_CHUNK = 64
GDN_INTERLEAVE = 4
MXU_WIDTH = 256
IN_SPLITS = (BRANCH_WIDTH, BRANCH_WIDTH, BRANCH_WIDTH, BRANCH_WIDTH, M2_CONV_DIM, M2_HEADS, 3 * BRANCH_WIDTH, BRANCH_WIDTH, GDN_HEADS, GDN_HEADS)


def _rms(x, g):
    return x * lax.rsqrt(jnp.mean(x * x, axis=-1, keepdims=True) + EPS) * g


def _silu(x):
    return x * jax.nn.sigmoid(x)


def _mm(a, b):
    return jnp.dot(a.astype(BF16), b.astype(BF16), preferred_element_type=F32)


def _mm_nt(a, b):
    return lax.dot_general(a.astype(BF16), b.astype(BF16), (((1,), (1,)), ((), ())), preferred_element_type=F32)


def _const_spec(shape):
    return pl.BlockSpec(shape, lambda *_: (0,) * len(shape))


def _row(v):
    return v.reshape(1, -1)


def _seq_params():
    return pltpu.CompilerParams(dimension_semantics=("parallel", "arbitrary"), vmem_limit_bytes=VMEM_LIMIT_BYTES)


def _ffn_body(h_ref, g_ref, wi_ref, wo_ref, o_ref):
    f = wo_ref.shape[0]
    h = h_ref[...]
    xn = _rms(h, g_ref[...]).astype(BF16)
    acts = []
    for s in range(0, f, MXU_WIDTH):
        gate = jnp.dot(xn, wi_ref[:, s:s + MXU_WIDTH], preferred_element_type=F32)
        up = jnp.dot(xn, wi_ref[:, f + s:f + s + MXU_WIDTH], preferred_element_type=F32)
        acts.append((_silu(gate) * up).astype(BF16))
    o_ref[...] = h + 0.5 * jnp.dot(jnp.concatenate(acts, axis=1), wo_ref[...], preferred_element_type=F32)


def _layer_spec(stacked, layer):
    shape = stacked.shape[1:]
    return pl.BlockSpec((None,) + shape, lambda *_: (layer,) + (0,) * len(shape), pipeline_mode=pl.Buffered(1))


def _tile_params():
    return pltpu.CompilerParams(dimension_semantics=("parallel",), vmem_limit_bytes=VMEM_LIMIT_BYTES)


def _ffn(h, norm_g, w_in, w_out, *, layer, tm):
    t, d = h.shape
    return pl.pallas_call(
        _ffn_body,
        out_shape=jax.ShapeDtypeStruct((t, d), F32),
        grid=(t // tm,),
        in_specs=[pl.BlockSpec((tm, d), lambda i: (i, 0)), _const_spec((1, d)),
                  _layer_spec(w_in, layer), _layer_spec(w_out, layer)],
        out_specs=pl.BlockSpec((tm, d), lambda i: (i, 0)),
        compiler_params=_tile_params(),
        name="ffn",
    )(h, _row(norm_g), w_in, w_out)


def _merge_body(h_ref, g_ref, ys5_ref, su_ref, dsk_ref, wglu_ref, bglu_ref, yb_ref, yc_ref, yd_ref, wg_ref, bg_ref,
                wb_ref, wo_ref, o_ref, stage_scr):
    h = h_ref[...]
    d = h.shape[-1]
    u = _rms(h, g_ref[...]).astype(BF16)
    z = jax.nn.gelu(_unfold_chunks(ys5_ref, stage_scr) + dsk_ref[...] * su_ref[...])
    ya = (z * jax.nn.sigmoid(_mm(z, wglu_ref[...]) + bglu_ref[...])).astype(BF16)
    mixed = jnp.zeros(h.shape, F32)
    for n, y in enumerate((ya, yb_ref[...], yc_ref[...], yd_ref[...])):
        gate = jax.nn.sigmoid(
            jnp.dot(u, wg_ref[:, n * d:(n + 1) * d], preferred_element_type=F32) + bg_ref[:, n * d:(n + 1) * d])
        mixed = mixed + gate * jnp.dot(y, wb_ref[n], preferred_element_type=F32)
    o_ref[...] = h + _mm(mixed, wo_ref[...])


def _merge(h, norm_g, y_s5, su, d_skip, w_glu, b_glu, ys, w_gate, b_gate, w_branch, w_out, *, layer, tm):
    t, d = h.shape
    nblk, _, fold = y_s5.shape
    w = nblk * LANES
    y_spec = pl.BlockSpec((tm, w), lambda i: (i, 0))
    return pl.pallas_call(
        _merge_body,
        out_shape=jax.ShapeDtypeStruct((t, d), F32),
        grid=(t // tm,),
        in_specs=[
            pl.BlockSpec((tm, d), lambda i: (i, 0)), _const_spec((1, d)),
            pl.BlockSpec((nblk, tm // S5_CHUNK, fold), lambda i: (0, i, 0)), y_spec,
            _const_spec((1, w)), _layer_spec(w_glu, layer), _const_spec((1, w)),
            y_spec, y_spec, y_spec,
            _layer_spec(w_gate, layer), _const_spec((1, b_gate.shape[-1])),
            _layer_spec(w_branch, layer), _layer_spec(w_out, layer),
        ],
        out_specs=pl.BlockSpec((tm, d), lambda i: (i, 0)),
        scratch_shapes=[pltpu.VMEM((nblk, tm, LANES), F32)],
        compiler_params=_tile_params(),
        name="merge",
    )(h, _row(norm_g), y_s5, su, _row(d_skip), w_glu, _row(b_glu), *ys, w_gate, _row(b_gate), w_branch, w_out)


def _ple_body(h_ref, g_ref, p_ref, wg_ref, wp_ref, fg_ref, o_ref, *, final_norm):
    h = h_ref[...]
    gate = jax.nn.sigmoid(_mm(_rms(h, g_ref[...]), wg_ref[...]))
    out = h + gate * _mm(p_ref[...], wp_ref[...])
    if final_norm:
        out = _rms(out, fg_ref[...])
    o_ref[...] = out


def _ple(h, norm_g, p, w_gate, w_proj, final_g, *, layer, tm, final_norm):
    t, d = h.shape
    pd = p.shape[-1]
    return pl.pallas_call(
        functools.partial(_ple_body, final_norm=final_norm),
        out_shape=jax.ShapeDtypeStruct((t, d), F32),
        grid=(t // tm,),
        in_specs=[
            pl.BlockSpec((tm, d), lambda i: (i, 0)), _const_spec((1, d)),
            pl.BlockSpec((None, tm, pd), lambda i: (layer, i, 0)),
            _layer_spec(w_gate, layer), _layer_spec(w_proj, layer), _const_spec((1, d)),
        ],
        out_specs=pl.BlockSpec((tm, d), lambda i: (i, 0)),
        compiler_params=_tile_params(),
        name="ple",
    )(h, _row(norm_g), p, w_gate, w_proj, _row(final_g))


def _segment_perm(l):
    r = jnp.arange(l)
    src = (r % SUBLANES) * (l // SUBLANES) + r // SUBLANES
    return (src[:, None] == jnp.arange(l)[None, :]).astype(BF16)


def _mixer_call(body, name, h, consts, scratch_shapes, *, bsz, tl, width, s5_input_outs=False):
    t, d = h.shape
    nt = t // bsz // tl
    out_shape = [jax.ShapeDtypeStruct((t, width), BF16)]
    out_specs = [pl.BlockSpec((tl, width), lambda b, i: (b * nt + i, 0))]
    if s5_input_outs:
        nblk, fold = width // LANES, S5_CHUNK * LANES
        out_shape += [jax.ShapeDtypeStruct((t, width), F32), jax.ShapeDtypeStruct((nblk, t // S5_CHUNK, fold), BF16)]
        out_specs += [pl.BlockSpec((tl, width), lambda b, i: (b * nt + i, 0)),
                      pl.BlockSpec((nblk, tl // S5_CHUNK, fold), lambda b, i: (0, b * nt + i, 0))]
    outs = pl.pallas_call(
        body,
        out_shape=tuple(out_shape),
        grid=(bsz, nt),
        in_specs=[pl.BlockSpec((tl, d), lambda b, i: (b * nt + i, 0))] + [_const_spec(c.shape) for c in consts],
        out_specs=tuple(out_specs),
        scratch_shapes=scratch_shapes,
        compiler_params=_seq_params(),
        name=name,
    )(h, *consts)
    return outs if s5_input_outs else outs[0]


def _fold_chunks(x, stage_scr, out_ref):
    rows = x.shape[0]
    for j in range(stage_scr.shape[0]):
        stage_scr[j] = x[:, j * LANES:(j + 1) * LANES]
    for j in range(stage_scr.shape[0]):
        for s in range(S5_CHUNK):
            out_ref[j, :, s * LANES:(s + 1) * LANES] = (
                stage_scr[j, pl.ds(s, rows // S5_CHUNK, stride=S5_CHUNK), :].astype(out_ref.dtype))


def _unfold_chunks(in_ref, stage_scr):
    rows = stage_scr.shape[1]
    for j in range(stage_scr.shape[0]):
        for s in range(S5_CHUNK):
            stage_scr[j, pl.ds(s, rows // S5_CHUNK, stride=S5_CHUNK), :] = in_ref[j, :, s * LANES:(s + 1) * LANES]
    return jnp.concatenate([stage_scr[j] for j in range(stage_scr.shape[0])], axis=1)


def _conv_halo(cur_tail, prev_tail):
    row = lax.broadcasted_iota(jnp.int32, cur_tail.shape, 0) % SUBLANES
    slab = jnp.where(row == SUBLANES - 1, prev_tail, cur_tail)
    n = cur_tail.shape[0] // SUBLANES
    return jnp.concatenate(
        [pltpu.roll(slab[g * SUBLANES:(g + 1) * SUBLANES], 1, 0) for g in range(n)], axis=0)


def _lru_body(h_ref, g_ref, p_ref, pt_ref, wx_ref, wg_ref, ws5_ref, cw_ref, cb_ref, wr_ref, br_ref, wi_ref, bi_ref,
              lam_ref, o_ref, su_ref, suf_ref, a_scr, b_scr, state_scr, tail_scr, stage_scr):
    l, w = a_scr.shape
    ls = l // SUBLANES
    kconv = cw_ref.shape[0]
    halo = (kconv - 1) * SUBLANES

    @pl.when(pl.program_id(1) == 0)
    def _():
        state_scr[...] = jnp.zeros_like(state_scr)
        tail_scr[...] = jnp.zeros_like(tail_scr)

    u = _rms(h_ref[...], g_ref[...]).astype(BF16)
    su = jnp.dot(u, ws5_ref[...], preferred_element_type=F32)
    su_ref[...] = su
    _fold_chunks(su, stage_scr, suf_ref)
    up = jnp.dot(p_ref[...], u, preferred_element_type=F32).astype(BF16)
    x = jnp.dot(up, wx_ref[...], preferred_element_type=F32)
    gate = jnp.dot(up, wg_ref[...], preferred_element_type=F32)

    cur_tail = x[l - halo:, :]
    xpad = jnp.concatenate([_conv_halo(cur_tail, tail_scr[...]), x], axis=0)
    tail_scr[...] = cur_tail
    xc = cb_ref[...] + sum(cw_ref[k:k + 1, :] * xpad[k * SUBLANES:k * SUBLANES + l] for k in range(kconv))

    r = jax.nn.sigmoid(_mm(xc, wr_ref[...]) + br_ref[...])
    ig = jax.nn.sigmoid(_mm(xc, wi_ref[...]) + bi_ref[...])
    log_a = (-LRU_C * jax.nn.softplus(-lam_ref[...])) * r
    a_scr[...] = jnp.exp(log_a)
    b_scr[...] = jnp.sqrt(-jnp.tanh(log_a) * (jnp.exp(2.0 * log_a) + 1.0)) * (ig * xc)

    def step(i, carry):
        hh, pp = carry
        rows = pl.ds(pl.multiple_of(i * SUBLANES, SUBLANES), SUBLANES)
        a_i = a_scr[rows, :]
        hh = a_i * hh + b_scr[rows, :]
        pp = a_i * pp
        b_scr[rows, :] = hh
        a_scr[rows, :] = pp
        return hh, pp

    h_end, p_end = lax.fori_loop(0, ls, step, (jnp.zeros((SUBLANES, w), F32), jnp.ones((SUBLANES, w), F32)),
                                 unroll=4)
    c = state_scr[...]
    carries = []
    for j in range(SUBLANES):
        carries.append(c)
        c = h_end[j:j + 1, :] + p_end[j:j + 1, :] * c
    state_scr[...] = c
    cmat = jnp.concatenate(carries, axis=0)
    hfull = b_scr[...].reshape(ls, SUBLANES, w) + a_scr[...].reshape(ls, SUBLANES, w) * cmat[None]
    out = (hfull.reshape(l, w) * jax.nn.gelu(gate)).astype(BF16)
    o_ref[...] = jnp.dot(pt_ref[...], out, preferred_element_type=F32).astype(BF16)


def _lru_mixer(h, norm_g, w_x, w_g, w_s5, conv_w, conv_b, w_r, b_r, w_i, b_i, lam, *, bsz, tl):
    w = w_x.shape[1]
    perm = _segment_perm(tl)
    blockdiag = lambda m: jax.scipy.linalg.block_diag(*m).astype(BF16)
    consts = (_row(norm_g), perm, perm.T, w_x, w_g, w_s5, conv_w, _row(conv_b), blockdiag(w_r), _row(b_r),
              blockdiag(w_i), _row(b_i), _row(lam))
    scratch = [pltpu.VMEM((tl, w), F32), pltpu.VMEM((tl, w), F32), pltpu.VMEM((1, w), F32),
               pltpu.VMEM(((conv_w.shape[0] - 1) * SUBLANES, w), F32), pltpu.VMEM((w // LANES, tl, LANES), F32)]
    return _mixer_call(_lru_body, "rglru", h, consts, scratch, bsz=bsz, tl=tl, width=w, s5_input_outs=True)


def _cmul(ar, ai, br, bi):
    return ar * br - ai * bi, ar * bi + ai * br


def _split3(x):
    hi = x.astype(BF16)
    r1 = x - hi.astype(F32)
    mid = r1.astype(BF16)
    lo = (r1 - mid.astype(F32)).astype(BF16)
    return hi, mid, lo


def _mm_exact_rhs(x, e):
    return sum(jnp.dot(part, e, preferred_element_type=F32) for part in _split3(x))


def _bmm_f32(a, b):
    a3, b3 = _split3(a), _split3(b)
    acc = None
    for i, ap in enumerate(a3):
        for j, bp in enumerate(b3):
            if i + j <= 2:
                t = jnp.einsum('gij,gjk->gik', ap, bp, preferred_element_type=F32)
                acc = t if acc is None else acc + t
    return acc


def _s5_prep_body(ls_ref, arow_ref, acol_ref, bt_ref, crep_ref, kt_ref, bst_ref, cst_ref, pw_ref):
    gb, _, p = arow_ref.shape
    ch = S5_GROUP_CH
    q = S5_CHUNK
    npw = pw_ref.shape[2]
    step = jnp.exp(ls_ref[...])
    a_re, a_im = arow_ref[:, 0:1, :], arow_ref[:, 1:2, :]
    sa_re, sa_im = a_re * step, a_im * step
    mag = jnp.exp(sa_re)
    ab_re, ab_im = mag * jnp.cos(sa_im), mag * jnp.sin(sa_im)
    den = a_re * a_re + a_im * a_im
    num_re = ab_re - 1.0
    f_re = (num_re * a_re + ab_im * a_im) / den
    f_im = (ab_im * a_re - num_re * a_im) / den
    bt_re, bt_im = bt_ref[:, 0], bt_ref[:, 1]
    bb_re = f_re * bt_re - f_im * bt_im
    bb_im = f_re * bt_im + f_im * bt_re
    ns = (q - 1 - lax.broadcasted_iota(jnp.int32, (1, q, 1), 1)).astype(F32)
    pmag = jnp.exp(ns * sa_re)
    pb_re, pb_im = pmag * jnp.cos(ns * sa_im), pmag * jnp.sin(ns * sa_im)
    pb_re = jnp.broadcast_to(pb_re[:, :, None, :], (gb, q, ch, p)).reshape(gb, q * ch, p)
    pb_im = jnp.broadcast_to(pb_im[:, :, None, :], (gb, q, ch, p)).reshape(gb, q * ch, p)
    bbr = jnp.concatenate([bb_re] * q, axis=1)
    bbi = jnp.concatenate([bb_im] * q, axis=1)
    sr, si = _cmul(bbr, bbi, pb_re, pb_im)
    bst_ref[:, 0] = sr
    bst_ref[:, 1] = si
    sc_re, sc_im = acol_ref[:, 0] * step, acol_ref[:, 1] * step
    nd = lax.broadcasted_iota(jnp.int32, (1, 1, q), 2).astype(F32)
    cmag = jnp.exp(sc_re * nd)
    pc_re, pc_im = cmag * jnp.cos(sc_im * nd), cmag * jnp.sin(sc_im * nd)
    expand = (lax.broadcasted_iota(jnp.int32, (q, q * ch), 0)
              == lax.broadcasted_iota(jnp.int32, (q, q * ch), 1) // ch).astype(BF16)
    pc_re = _mm_exact_rhs(pc_re.reshape(gb * p, q), expand).reshape(gb, p, q * ch)
    pc_im = _mm_exact_rhs(pc_im.reshape(gb * p, q), expand).reshape(gb, p, q * ch)
    ca_re, ca_im = _cmul(crep_ref[:, 0], crep_ref[:, 1], pc_re, pc_im)
    kt_ref[...] = _bmm_f32(bb_re, ca_re) - _bmm_f32(bb_im, ca_im)
    abc_mag = jnp.exp(sc_re)
    o_re, o_im = _cmul(ca_re, ca_im, abc_mag * jnp.cos(sc_im), abc_mag * jnp.sin(sc_im))
    cst_ref[:, 0] = o_re
    cst_ref[:, 1] = -o_im
    nn = jnp.left_shift(1, lax.broadcasted_iota(jnp.int32, (1, npw, 1), 1)).astype(F32) * float(q)
    wmag = jnp.exp(nn * sa_re)
    pw_ref[:, 0] = wmag * jnp.cos(nn * sa_im)
    pw_ref[:, 1] = wmag * jnp.sin(nn * sa_im)


def _s5_prep(log_step, a_re, a_im, b_re, b_im, c_re, c_im):
    g, p = a_re.shape
    ch, q = S5_GROUP_CH, S5_CHUNK
    gb = S5_PREP_GROUPS
    arow = jnp.stack([a_re, a_im], axis=1)
    acol = arow[..., None]
    bt = jnp.stack([b_re, b_im], axis=1).transpose(0, 1, 3, 2)
    crep = jnp.tile(jnp.stack([c_re, c_im], axis=1).transpose(0, 1, 3, 2), (1, 1, 1, q))
    blk = lambda *shape: pl.BlockSpec((gb,) + shape, lambda i: (i,) + (0,) * len(shape))
    npw = S5_NPOW
    return pl.pallas_call(
        _s5_prep_body,
        out_shape=(jax.ShapeDtypeStruct((g, ch, q * ch), F32), jax.ShapeDtypeStruct((g, 2, q * ch, p), F32),
                   jax.ShapeDtypeStruct((g, 2, p, q * ch), F32), jax.ShapeDtypeStruct((g, 2, npw, p), F32)),
        grid=(g // gb,),
        in_specs=[blk(1, 1), blk(2, p), blk(2, p, 1), blk(2, ch, p), blk(2, p, q * ch)],
        out_specs=(blk(ch, q * ch), blk(2, q * ch, p), blk(2, p, q * ch), blk(2, npw, p)),
        compiler_params=_tile_params(),
        name="s5_prep",
    )(log_step.reshape(g, 1, 1), arow, acol, bt, crep)


def _shift_rows(x, sh):
    if sh % SUBLANES == 0:
        return jnp.concatenate([jnp.zeros((sh, x.shape[1]), x.dtype), x[:x.shape[0] - sh]], axis=0)
    row = lax.broadcasted_iota(jnp.int32, x.shape, 0)
    return jnp.where(row >= sh, pltpu.roll(x, sh, 0), 0.0)


def _s5_core_body(x_ref, kd_ref, bst_ref, cst_ref, pw_ref, o_ref, toep_scr):
    rb, lanes = o_ref.shape
    q = S5_CHUNK
    ns = bst_ref.shape[1] // 2

    @pl.when(pl.program_id(1) == 0)
    def _():
        toep_scr[...] = jnp.zeros_like(toep_scr)
        for s in range(q):
            for t in range(s, q):
                toep_scr[s * LANES:(s + 1) * LANES, t * LANES:(t + 1) * LANES] = kd_ref[t - s]

    z = x_ref[...]
    ys = []
    for tp in range(0, q, 2):
        k = (tp + 2) * LANES
        ys.append(jnp.dot(z[:, :k], toep_scr[:k, tp * LANES:(tp + 2) * LANES], preferred_element_type=F32))
    g = jnp.dot(z, bst_ref[...], preferred_element_type=F32)
    e_re, e_im = g[:, :ns], g[:, ns:]
    sh, m = 1, 0
    while sh < rb:
        sr, si = _shift_rows(e_re, sh), _shift_rows(e_im, sh)
        pr, pi = _cmul(pw_ref[0, m:m + 1, :], pw_ref[1, m:m + 1, :], sr, si)
        e_re, e_im = e_re + pr, e_im + pi
        sh, m = sh * 2, m + 1
    s_re = _shift_rows(e_re, 1).astype(BF16)
    s_im = _shift_rows(e_im, 1).astype(BF16)
    o_ref[...] = (jnp.concatenate(ys, axis=1) + jnp.dot(s_re, cst_ref[:ns, :], preferred_element_type=F32)
                  + jnp.dot(s_im, cst_ref[ns:, :], preferred_element_type=F32))


def _s5_core(x, log_step, a_re, a_im, b_re, b_im, c_re, c_im, *, bsz):
    nblk, nrows, lanes = x.shape
    ch, q, p = S5_GROUP_CH, S5_CHUNK, S5_STATE
    o8 = S5_OCT
    rb = nrows // bsz
    kt, bst, cst, pw = _s5_prep(log_step, a_re, a_im, b_re, b_im, c_re, c_im)
    eye = jnp.eye(o8, dtype=F32)
    kd = jnp.einsum('ogadc,gh->odgahc', kt.reshape(nblk, o8, ch, q, ch), eye).reshape(nblk, q, LANES, LANES)
    bst_o = jnp.einsum('ogrscp,gh->osgcrhp', bst.reshape(nblk, o8, 2, q, ch, p), eye).reshape(nblk, lanes, 2 * o8 * p)
    cst_o = jnp.einsum('ogrptc,gh->orhptgc', cst.reshape(nblk, o8, 2, p, q, ch), eye).reshape(nblk, 2 * o8 * p, lanes)
    pw_o = pw.reshape(nblk, o8, 2, S5_NPOW, p).transpose(0, 2, 3, 1, 4).reshape(nblk, 2, S5_NPOW, o8 * p)
    wspec = lambda *shape: pl.BlockSpec((None,) + shape, lambda o, b: (o,) + (0,) * len(shape),
                                        pipeline_mode=pl.Buffered(1))
    y = pl.pallas_call(
        _s5_core_body,
        out_shape=jax.ShapeDtypeStruct((nblk, nrows, lanes), F32),
        grid=(nblk, bsz),
        in_specs=[pl.BlockSpec((None, rb, lanes), lambda o, b: (o, b, 0)),
                  wspec(q, LANES, LANES), wspec(lanes, 2 * o8 * p), wspec(2 * o8 * p, lanes),
                  wspec(2, S5_NPOW, o8 * p)],
        out_specs=pl.BlockSpec((None, rb, lanes), lambda o, b: (o, b, 0)),
        scratch_shapes=[pltpu.VMEM((lanes, lanes), BF16)],
        compiler_params=_seq_params(),
        name="s5_core",
    )(x, kd.astype(BF16), bst_o.astype(BF16), cst_o.astype(BF16), pw_o)
    return y


def _causal_conv(x, xpad_scr, tail_scr, cw_ref, first):
    l = x.shape[0]
    k = cw_ref.shape[0]

    @pl.when(first)
    def _():
        tail_scr[...] = jnp.zeros_like(tail_scr)

    xpad_scr[:SUBLANES, :] = tail_scr[...]
    xpad_scr[SUBLANES:, :] = x
    tail_scr[...] = x[l - SUBLANES:, :]
    off = SUBLANES - (k - 1)
    return sum(cw_ref[j:j + 1, :] * xpad_scr[off + j:off + j + l, :] for j in range(k))


def _split_dot(tri, x):
    hi = x.astype(BF16)
    lo = (x - hi.astype(F32)).astype(BF16)
    return jnp.dot(tri, hi, preferred_element_type=F32) + jnp.dot(tri, lo, preferred_element_type=F32)


def _lower_tri(q):
    return lax.broadcasted_iota(jnp.int32, (q, q), 0) >= lax.broadcasted_iota(jnp.int32, (q, q), 1)


def _decay_matrix(cs, cst, lane, causal):
    diff = cs[:, lane:lane + 1] - cst[lane:lane + 1, :]
    return jnp.where(causal, jnp.exp(jnp.where(causal, diff, 0.0)), 0.0)


def _ssd_body(h_ref, g_ref, wz_ref, wx_ref, wdt_ref, cw_ref, cb_ref, dtb_ref, alog_ref, dsk_ref, ng_ref, o_ref,
              xpad_scr, tail_scr, state_scr):
    l, w = o_ref.shape
    ngroups, nstate, gw = state_scr.shape
    q = M2_CHUNK
    first = pl.program_id(1) == 0

    @pl.when(first)
    def _():
        state_scr[...] = jnp.zeros_like(state_scr)

    u = _rms(h_ref[...], g_ref[...]).astype(BF16)
    z = jnp.dot(u, wz_ref[...], preferred_element_type=F32)
    xbc_raw = jnp.dot(u, wx_ref[...], preferred_element_type=F32)
    dt_raw = jnp.dot(u, wdt_ref[...], preferred_element_type=F32)
    xbc = _silu(_causal_conv(xbc_raw, xpad_scr, tail_scr, cw_ref, first) + cb_ref[...])
    xs = xbc[:, :w]
    bm = xbc[:, w:w + ngroups * nstate]
    cm = xbc[:, w + ngroups * nstate:]
    dt = jax.nn.softplus(dt_raw + dtb_ref[...])
    da = dt * (-jnp.exp(alog_ref[...]))
    xdt = xs * dt

    causal = _lower_tri(q)
    tri = causal.astype(BF16)
    head_of_lane = lax.broadcasted_iota(jnp.int32, (1, gw), 1) // M2_HEAD_DIM

    states = [state_scr[g] for g in range(ngroups)]
    for c in range(l // q):
        rows = slice(c * q, (c + 1) * q)
        cs = _split_dot(tri, da[rows])
        cs_last = cs[q - 1:q, :]
        dte = jnp.exp(cs_last - cs)
        ecs = jnp.exp(cs)
        cdec = jnp.exp(cs_last)
        cst = cs.T
        ychunk = []
        for g in range(ngroups):
            gl = slice(g * gw, (g + 1) * gw)
            bg = bm[rows, g * nstate:(g + 1) * nstate]
            cg = cm[rows, g * nstate:(g + 1) * nstate]
            cb = _mm_nt(cg, bg)
            xg = xdt[rows, gl]
            yg = _mm(cg, states[g]) * ecs[:, gl]
            for hh in range(gw // M2_HEAD_DIM):
                dec = _decay_matrix(cs, cst, g * gw + hh * M2_HEAD_DIM, causal)
                yg = yg + _mm(cb * dec, jnp.where(head_of_lane == hh, xg, 0.0))
            states[g] = states[g] * cdec[:, gl] + _mm(bg.T, xg * dte[:, gl])
            ychunk.append(yg)
        y = jnp.concatenate(ychunk, axis=1) + dsk_ref[...] * xs[rows]
        y = y * _silu(z[rows])
        o_ref[rows, :] = _rms(y, ng_ref[...]).astype(BF16)
    for g in range(ngroups):
        state_scr[g] = states[g]


def _ssd_mixer(h, norm_g, w_z, w_xbc, w_dt, conv_w, conv_b, dt_bias, a_log, d_skip, out_norm_g, *, bsz, tl):
    w = w_z.shape[1]
    cdim = w_xbc.shape[1]
    rep = lambda v: jnp.repeat(v, M2_HEAD_DIM).reshape(1, w)
    consts = (_row(norm_g), w_z, w_xbc, jnp.repeat(w_dt, M2_HEAD_DIM, axis=1), conv_w, _row(conv_b), rep(dt_bias),
              rep(a_log), rep(d_skip), _row(out_norm_g))
    scratch = [pltpu.VMEM((tl + SUBLANES, cdim), F32), pltpu.VMEM((SUBLANES, cdim), F32),
               pltpu.VMEM((M2_GROUPS, M2_STATE, w // M2_GROUPS), F32)]
    return _mixer_call(_ssd_body, "ssd", h, consts, scratch, bsz=bsz, tl=tl, width=w)


def _l2norm(x):
    return x * lax.rsqrt(jnp.sum(x * x, axis=-1, keepdims=True) + EPS)


def _gdn_body(h_ref, g_ref, wqkv_ref, wg_ref, wb_ref, wa_ref, cw_ref, dtb_ref, alog_ref, ng_ref, o_ref,
              xpad_scr, tail_scr, state_scr):
    l, w = o_ref.shape
    nheads, dk, dv = state_scr.shape
    q = GDN_CHUNK
    sq = nheads * q
    nchunks = l // q
    first = pl.program_id(1) == 0

    @pl.when(first)
    def _():
        state_scr[...] = jnp.zeros_like(state_scr)

    u = _rms(h_ref[...], g_ref[...]).astype(BF16)
    qkv_raw = jnp.dot(u, wqkv_ref[...], preferred_element_type=F32)
    gate = jnp.dot(u, wg_ref[...], preferred_element_type=F32)
    beta = jax.nn.sigmoid(jnp.dot(u, wb_ref[...], preferred_element_type=F32))
    a_raw = jnp.dot(u, wa_ref[...], preferred_element_type=F32)
    glog = -jnp.exp(alog_ref[...]) * jax.nn.softplus(a_raw + dtb_ref[...])
    qkv = _silu(_causal_conv(qkv_raw, xpad_scr, tail_scr, cw_ref, first))

    tri = _lower_tri(q).astype(BF16)
    gcs_c = [_split_dot(tri, glog[c * q:(c + 1) * q]) for c in range(nchunks)]
    gcs = jnp.concatenate(gcs_c, axis=0)
    gtot = jnp.concatenate([jnp.broadcast_to(g_[q - 1:q, :], (q, w)) for g_ in gcs_c], axis=0)
    egcs = jnp.exp(gcs)

    def stack(x, c):
        return jnp.concatenate([x[c * q:(c + 1) * q, hd * dk:(hd + 1) * dk] for hd in range(nheads)], axis=0)

    qn = jnp.concatenate([_l2norm(qkv[:, hd * dk:(hd + 1) * dk]) * (dk ** -0.5) for hd in range(nheads)], axis=1)
    kn = jnp.concatenate([_l2norm(qkv[:, w + hd * dk:w + (hd + 1) * dk]) for hd in range(nheads)], axis=1)
    kb = kn * beta
    qd = qn * egcs
    kd = kn * jnp.exp(gtot - gcs)
    kbe = kb * egcs
    vb = qkv[:, 2 * w:] * beta

    rs = lax.broadcasted_iota(jnp.int32, (sq, sq), 0)
    cs = lax.broadcasted_iota(jnp.int32, (sq, sq), 1)
    same_head = (rs // q) == (cs // q)
    causal = same_head & (rs >= cs)
    strict = same_head & (rs > cs)
    eye = (rs == cs).astype(F32)
    head_block = (lax.broadcasted_iota(jnp.int32, (sq, nheads * dv), 0) // q
                  == lax.broadcasted_iota(jnp.int32, (sq, nheads * dv), 1) // dv)

    states = [state_scr[hd] for hd in range(nheads)]
    for c0 in range(0, nchunks, GDN_INTERLEAVE):
        cg = list(range(c0, min(c0 + GDN_INTERLEAVE, nchunks)))
        k_s = [stack(kn, c) for c in cg]
        g_s = [stack(gcs, c) for c in cg]
        g_t = jnp.concatenate(g_s, axis=0).T
        decay = []
        for i in range(len(cg)):
            diff = jnp.concatenate([g_s[i]] * (sq // dk), axis=1) - g_t[0:1, i * sq:(i + 1) * sq]
            decay.append(jnp.where(causal, jnp.exp(jnp.where(causal, diff, 0.0)), 0.0))
        kk = [_mm_nt(stack(kb, c), k_s[i]) for i, c in enumerate(cg)]
        qk = [_mm_nt(stack(qn, c), k_s[i]) * decay[i] for i, c in enumerate(cg)]
        pw = [jnp.where(strict, -(kk[i] * decay[i]), 0.0) for i in range(len(cg))]
        inv = [eye + p_ for p_ in pw]
        span = 2
        while span < q:
            pw = [_mm(p_, p_) for p_ in pw]
            inv = [a_ + _mm(a_, p_) for a_, p_ in zip(inv, pw)]
            span *= 2
        sol = [_mm(inv[i], jnp.concatenate([stack(vb, c), stack(kbe, c)], axis=1)) for i, c in enumerate(cg)]
        kd_t = jnp.concatenate([stack(kd, c) for c in cg], axis=0).T

        for i, c in enumerate(cg):
            rows = slice(c * q, (c + 1) * q)
            qd_s = stack(qd, c)
            r = [_mm(jnp.concatenate([sol[i][hd * q:(hd + 1) * q, dv:], qd_s[hd * q:(hd + 1) * q]], axis=0), states[hd])
                 for hd in range(nheads)]
            v_new = jnp.concatenate([sol[i][hd * q:(hd + 1) * q, :dv] - r[hd][:q] for hd in range(nheads)], axis=0)
            o_s = jnp.concatenate([r[hd][q:] for hd in range(nheads)], axis=0) + _mm(qk[i], v_new)
            v_bd = jnp.where(head_block, jnp.concatenate([v_new] * nheads, axis=1), 0.0)
            upd = _mm(kd_t[:, i * sq:(i + 1) * sq], v_bd)
            cdec = jnp.exp(gtot[c * q:c * q + 1, :])
            for hd in range(nheads):
                hl = slice(hd * dk, (hd + 1) * dk)
                states[hd] = states[hd] * cdec[:, hl] + upd[:, hl]
                o_ref[rows, hl] = (_rms(o_s[hd * q:(hd + 1) * q], ng_ref[...]) * _silu(gate[rows, hl])).astype(BF16)
    for hd in range(nheads):
        state_scr[hd] = states[hd]


def _gdn_mixer(h, norm_g, w_qkv, w_g, w_beta, w_a, conv_w, dt_bias, a_log, out_norm_g, *, bsz, tl):
    w = w_g.shape[1]
    cdim = w_qkv.shape[1]
    rep = lambda v: jnp.repeat(v, GDN_HEAD_DIM).reshape(1, w)
    consts = (_row(norm_g), w_qkv, w_g, jnp.repeat(w_beta, GDN_HEAD_DIM, axis=1),
              jnp.repeat(w_a, GDN_HEAD_DIM, axis=1), conv_w, rep(dt_bias), rep(a_log), _row(out_norm_g))
    scratch = [pltpu.VMEM((tl + SUBLANES, cdim), F32), pltpu.VMEM((SUBLANES, cdim), F32),
               pltpu.VMEM((w // GDN_HEAD_DIM, GDN_HEAD_DIM, GDN_HEAD_DIM), F32)]
    return _mixer_call(_gdn_body, "gdn", h, consts, scratch, bsz=bsz, tl=tl, width=w)


def _col_slices(sizes):
    out, start = [], 0
    for size in sizes:
        out.append(slice(start, start + size))
        start += size
    return out


def kernel(x, p, ffn1_norm, ffn1_w_in, ffn1_w_out, mix_norm, w_in, w_gate, b_gate, s5_log_step, s5_a_re, s5_a_im, s5_b_re, s5_b_im, s5_c_re, s5_c_im, s5_d, s5_w_glu, s5_b_glu, lru_conv_w, lru_conv_b, lru_w_r, lru_b_r, lru_w_i, lru_b_i, lru_lambda, m2_conv_w, m2_conv_b, m2_dt_bias, m2_a_log, m2_d, m2_norm, gdn_conv_w, gdn_dt_bias, gdn_a_log, gdn_norm, w_branch, w_out, ffn2_norm, ffn2_w_in, ffn2_w_out, ple_norm, ple_w_gate, ple_w_proj, final_norm):
    bsz, s, d = x.shape
    depth = p.shape[0]
    t = bsz * s
    h = x.reshape(t, d)
    p = p.reshape(depth, t, -1)
    tm, tl = 512, 512
    bf = lambda a: a.astype(BF16)
    ffn1_w_in, ffn1_w_out, ffn2_w_in, ffn2_w_out = bf(ffn1_w_in), bf(ffn1_w_out), bf(ffn2_w_in), bf(ffn2_w_out)
    w_gate, w_branch, w_out, s5_w_glu = bf(w_gate), bf(w_branch), bf(w_out), bf(s5_w_glu)
    ple_w_gate, ple_w_proj = bf(ple_w_gate), bf(ple_w_proj)
    for i in range(depth):
        h = _ffn(h, ffn1_norm[i], ffn1_w_in, ffn1_w_out, layer=i, tm=tm)
        (w_s5, w_lx, w_lg, w_mz, w_mx, w_mdt, w_gqkv, w_gg, w_gb, w_ga) = [bf(w_in[i][:, c]) for c in _col_slices(IN_SPLITS)]
        y_b, su, su_folded = _lru_mixer(h, mix_norm[i], w_lx, w_lg, w_s5, lru_conv_w[i], lru_conv_b[i], lru_w_r[i],
                                        lru_b_r[i], lru_w_i[i], lru_b_i[i], lru_lambda[i], bsz=bsz, tl=tl)
        y_s5 = _s5_core(su_folded, s5_log_step[i], s5_a_re[i], s5_a_im[i], s5_b_re[i], s5_b_im[i], s5_c_re[i],
                        s5_c_im[i], bsz=bsz)
        y_c = _ssd_mixer(h, mix_norm[i], w_mz, w_mx, w_mdt, m2_conv_w[i], m2_conv_b[i], m2_dt_bias[i],
                         m2_a_log[i], m2_d[i], m2_norm[i], bsz=bsz, tl=tl)
        y_d = _gdn_mixer(h, mix_norm[i], w_gqkv, w_gg, w_gb, w_ga, gdn_conv_w[i], gdn_dt_bias[i], gdn_a_log[i],
                         gdn_norm[i], bsz=bsz, tl=tl)
        h = _merge(h, mix_norm[i], y_s5, su, s5_d[i], s5_w_glu, s5_b_glu[i], (y_b, y_c, y_d), w_gate, b_gate[i],
                   w_branch, w_out, layer=i, tm=tm)
        h = _ffn(h, ffn2_norm[i], ffn2_w_in, ffn2_w_out, layer=i, tm=tm)
        h = _ple(h, ple_norm[i], p, ple_w_gate, ple_w_proj, final_norm, layer=i, tm=tm,
                 final_norm=(i == depth - 1))
    return h.reshape(bsz, s, d)
```

```python
import functools

import jax
import jax.numpy as jnp
from jax import lax
from jax.experimental import pallas as pl
from jax.experimental.pallas import tpu as pltpu

EPS = 1e-6
BF16 = jnp.bfloat16
F32 = jnp.float32

V7X_VMEM_BYTES = 64 * 1024 * 1024
VMEM_LIMIT_BYTES = V7X_VMEM_BYTES - 8 * 1024 * 1024
SUBLANES = 8
LANES = 128

BRANCH_WIDTH = 512
S5_GROUP_CH = 16
S5_STATE = 64
S5_CHUNK = 16
S5_OCT = LANES // S5_GROUP_CH
S5_NPOW = 16
LRU_C = 8.0
M2_HEAD_DIM = 64
M2_HEADS = BRANCH_WIDTH // M2_HEAD_DIM
M2_GROUPS = 2
M2_STATE = 128
M2_CONV_DIM = BRANCH_WIDTH + 2 * M2_GROUPS * M2_STATE
M2_CHUNK = 128
GDN_HEAD_DIM = 128
GDN_HEADS = BRANCH_WIDTH // GDN_HEAD_DIM
GDN_CHUNK = 64
GDN_INTERLEAVE = 4
MXU_WIDTH = 256
IN_SPLITS = (BRANCH_WIDTH, BRANCH_WIDTH, BRANCH_WIDTH, BRANCH_WIDTH, M2_CONV_DIM, M2_HEADS, 3 * BRANCH_WIDTH, BRANCH_WIDTH, GDN_HEADS, GDN_HEADS)


def _rms(x, g):
    return x * lax.rsqrt(jnp.mean(x * x, axis=-1, keepdims=True) + EPS) * g


def _silu(x):
    return x * jax.nn.sigmoid(x)


def _mm(a, b):
    return jnp.dot(a.astype(BF16), b.astype(BF16), preferred_element_type=F32)


def _mm_nt(a, b):
    return lax.dot_general(a.astype(BF16), b.astype(BF16), (((1,), (1,)), ((), ())), preferred_element_type=F32)


def _const_spec(shape):
    return pl.BlockSpec(shape, lambda *_: (0,) * len(shape))


def _row(v):
    return v.reshape(1, -1)


def _seq_params():
    return pltpu.CompilerParams(dimension_semantics=("parallel", "arbitrary"), vmem_limit_bytes=VMEM_LIMIT_BYTES)


def _ffn_body(h_ref, g_ref, wi_ref, wo_ref, o_ref):
    f = wo_ref.shape[0]
    h = h_ref[...]
    xn = _rms(h, g_ref[...]).astype(BF16)
    acts = []
    for s in range(0, f, MXU_WIDTH):
        gate = jnp.dot(xn, wi_ref[:, s:s + MXU_WIDTH], preferred_element_type=F32)
        up = jnp.dot(xn, wi_ref[:, f + s:f + s + MXU_WIDTH], preferred_element_type=F32)
        acts.append((_silu(gate) * up).astype(BF16))
    o_ref[...] = h + 0.5 * jnp.dot(jnp.concatenate(acts, axis=1), wo_ref[...], preferred_element_type=F32)


def _layer_spec(stacked, layer):
    shape = stacked.shape[1:]
    return pl.BlockSpec((None,) + shape, lambda *_: (layer,) + (0,) * len(shape), pipeline_mode=pl.Buffered(1))


def _tile_params():
    return pltpu.CompilerParams(dimension_semantics=("parallel",), vmem_limit_bytes=VMEM_LIMIT_BYTES)


def _ffn(h, norm_g, w_in, w_out, *, layer, tm):
    t, d = h.shape
    return pl.pallas_call(
        _ffn_body,
        out_shape=jax.ShapeDtypeStruct((t, d), F32),
        grid=(t // tm,),
        in_specs=[pl.BlockSpec((tm, d), lambda i: (i, 0)), _const_spec((1, d)),
                  _layer_spec(w_in, layer), _layer_spec(w_out, layer)],
        out_specs=pl.BlockSpec((tm, d), lambda i: (i, 0)),
        compiler_params=_tile_params(),
        name="ffn",
    )(h, _row(norm_g), w_in, w_out)


def _merge_body(h_ref, g_ref, ys5_ref, su_ref, dsk_ref, wglu_ref, bglu_ref, yb_ref, yc_ref, yd_ref, wg_ref, bg_ref,
                wb_ref, wo_ref, o_ref, stage_scr):
    h = h_ref[...]
    d = h.shape[-1]
    u = _rms(h, g_ref[...]).astype(BF16)
    z = jax.nn.gelu(_unfold_chunks(ys5_ref, stage_scr) + dsk_ref[...] * su_ref[...])
    ya = (z * jax.nn.sigmoid(_mm(z, wglu_ref[...]) + bglu_ref[...])).astype(BF16)
    mixed = jnp.zeros(h.shape, F32)
    for n, y in enumerate((ya, yb_ref[...], yc_ref[...], yd_ref[...])):
        gate = jax.nn.sigmoid(
            jnp.dot(u, wg_ref[:, n * d:(n + 1) * d], preferred_element_type=F32) + bg_ref[:, n * d:(n + 1) * d])
        mixed = mixed + gate * jnp.dot(y, wb_ref[n], preferred_element_type=F32)
    o_ref[...] = h + _mm(mixed, wo_ref[...])


def _merge(h, norm_g, y_s5, su, d_skip, w_glu, b_glu, ys, w_gate, b_gate, w_branch, w_out, *, layer, tm):
    t, d = h.shape
    nblk, _, fold = y_s5.shape
    w = nblk * LANES
    y_spec = pl.BlockSpec((tm, w), lambda i: (i, 0))
    return pl.pallas_call(
        _merge_body,
        out_shape=jax.ShapeDtypeStruct((t, d), F32),
        grid=(t // tm,),
        in_specs=[
            pl.BlockSpec((tm, d), lambda i: (i, 0)), _const_spec((1, d)),
            pl.BlockSpec((nblk, tm // S5_CHUNK, fold), lambda i: (0, i, 0)), y_spec,
            _const_spec((1, w)), _layer_spec(w_glu, layer), _const_spec((1, w)),
            y_spec, y_spec, y_spec,
            _layer_spec(w_gate, layer), _const_spec((1, b_gate.shape[-1])),
            _layer_spec(w_branch, layer), _layer_spec(w_out, layer),
        ],
        out_specs=pl.BlockSpec((tm, d), lambda i: (i, 0)),
        scratch_shapes=[pltpu.VMEM((nblk, tm, LANES), F32)],
        compiler_params=_tile_params(),
        name="merge",
    )(h, _row(norm_g), y_s5, su, _row(d_skip), w_glu, _row(b_glu), *ys, w_gate, _row(b_gate), w_branch, w_out)


def _ple_body(h_ref, g_ref, p_ref, wg_ref, wp_ref, fg_ref, o_ref, *, final_norm):
    h = h_ref[...]
    gate = jax.nn.sigmoid(_mm(_rms(h, g_ref[...]), wg_ref[...]))
    out = h + gate * _mm(p_ref[...], wp_ref[...])
    if final_norm:
        out = _rms(out, fg_ref[...])
    o_ref[...] = out


def _ple(h, norm_g, p, w_gate, w_proj, final_g, *, layer, tm, final_norm):
    t, d = h.shape
    pd = p.shape[-1]
    return pl.pallas_call(
        functools.partial(_ple_body, final_norm=final_norm),
        out_shape=jax.ShapeDtypeStruct((t, d), F32),
        grid=(t // tm,),
        in_specs=[
            pl.BlockSpec((tm, d), lambda i: (i, 0)), _const_spec((1, d)),
            pl.BlockSpec((None, tm, pd), lambda i: (layer, i, 0)),
            _layer_spec(w_gate, layer), _layer_spec(w_proj, layer), _const_spec((1, d)),
        ],
        out_specs=pl.BlockSpec((tm, d), lambda i: (i, 0)),
        compiler_params=_tile_params(),
        name="ple",
    )(h, _row(norm_g), p, w_gate, w_proj, _row(final_g))


def _segment_perm(l):
    r = jnp.arange(l)
    src = (r % SUBLANES) * (l // SUBLANES) + r // SUBLANES
    return (src[:, None] == jnp.arange(l)[None, :]).astype(BF16)


def _mixer_call(body, name, h, consts, scratch_shapes, *, bsz, tl, width, s5_input_outs=False):
    t, d = h.shape
    nt = t // bsz // tl
    out_shape = [jax.ShapeDtypeStruct((t, width), BF16)]
    out_specs = [pl.BlockSpec((tl, width), lambda b, i: (b * nt + i, 0))]
    if s5_input_outs:
        nblk, fold = width // LANES, S5_CHUNK * LANES
        out_shape += [jax.ShapeDtypeStruct((t, width), F32), jax.ShapeDtypeStruct((nblk, t // S5_CHUNK, fold), BF16)]
        out_specs += [pl.BlockSpec((tl, width), lambda b, i: (b * nt + i, 0)),
                      pl.BlockSpec((nblk, tl // S5_CHUNK, fold), lambda b, i: (0, b * nt + i, 0))]
    outs = pl.pallas_call(
        body,
        out_shape=tuple(out_shape),
        grid=(bsz, nt),
        in_specs=[pl.BlockSpec((tl, d), lambda b, i: (b * nt + i, 0))] + [_const_spec(c.shape) for c in consts],
        out_specs=tuple(out_specs),
        scratch_shapes=scratch_shapes,
        compiler_params=_seq_params(),
        name=name,
    )(h, *consts)
    return outs if s5_input_outs else outs[0]


def _fold_chunks(x, stage_scr, out_ref):
    rows = x.shape[0]
    for j in range(stage_scr.shape[0]):
        stage_scr[j] = x[:, j * LANES:(j + 1) * LANES]
    for j in range(stage_scr.shape[0]):
        for s in range(S5_CHUNK):
            out_ref[j, :, s * LANES:(s + 1) * LANES] = (
                stage_scr[j, pl.ds(s, rows // S5_CHUNK, stride=S5_CHUNK), :].astype(out_ref.dtype))


def _unfold_chunks(in_ref, stage_scr):
    rows = stage_scr.shape[1]
    for j in range(stage_scr.shape[0]):
        for s in range(S5_CHUNK):
            stage_scr[j, pl.ds(s, rows // S5_CHUNK, stride=S5_CHUNK), :] = in_ref[j, :, s * LANES:(s + 1) * LANES]
    return jnp.concatenate([stage_scr[j] for j in range(stage_scr.shape[0])], axis=1)


def _conv_halo(cur_tail, prev_tail):
    row = lax.broadcasted_iota(jnp.int32, cur_tail.shape, 0) % SUBLANES
    slab = jnp.where(row == SUBLANES - 1, prev_tail, cur_tail)
    n = cur_tail.shape[0] // SUBLANES
    return jnp.concatenate(
        [pltpu.roll(slab[g * SUBLANES:(g + 1) * SUBLANES], 1, 0) for g in range(n)], axis=0)


def _lru_body(h_ref, g_ref, p_ref, pt_ref, wx_ref, wg_ref, ws5_ref, cw_ref, cb_ref, wr_ref, br_ref, wi_ref, bi_ref,
              lam_ref, o_ref, su_ref, suf_ref, a_scr, b_scr, state_scr, tail_scr, stage_scr):
    l, w = a_scr.shape
    ls = l // SUBLANES
    kconv = cw_ref.shape[0]
    halo = (kconv - 1) * SUBLANES

    @pl.when(pl.program_id(1) == 0)
    def _():
        state_scr[...] = jnp.zeros_like(state_scr)
        tail_scr[...] = jnp.zeros_like(tail_scr)

    u = _rms(h_ref[...], g_ref[...]).astype(BF16)
    su = jnp.dot(u, ws5_ref[...], preferred_element_type=F32)
    su_ref[...] = su
    _fold_chunks(su, stage_scr, suf_ref)
    up = jnp.dot(p_ref[...], u, preferred_element_type=F32).astype(BF16)
    x = jnp.dot(up, wx_ref[...], preferred_element_type=F32)
    gate = jnp.dot(up, wg_ref[...], preferred_element_type=F32)

    cur_tail = x[l - halo:, :]
    xpad = jnp.concatenate([_conv_halo(cur_tail, tail_scr[...]), x], axis=0)
    tail_scr[...] = cur_tail
    xc = cb_ref[...] + sum(cw_ref[k:k + 1, :] * xpad[k * SUBLANES:k * SUBLANES + l] for k in range(kconv))

    r = jax.nn.sigmoid(_mm(xc, wr_ref[...]) + br_ref[...])
    ig = jax.nn.sigmoid(_mm(xc, wi_ref[...]) + bi_ref[...])
    log_a = (-LRU_C * jax.nn.softplus(-lam_ref[...])) * r
    a_scr[...] = jnp.exp(log_a)
    b_scr[...] = jnp.sqrt(-jnp.tanh(log_a) * (jnp.exp(2.0 * log_a) + 1.0)) * (ig * xc)

    def step(i, carry):
        hh, pp = carry
        rows = pl.ds(pl.multiple_of(i * SUBLANES, SUBLANES), SUBLANES)
        a_i = a_scr[rows, :]
        hh = a_i * hh + b_scr[rows, :]
        pp = a_i * pp
        b_scr[rows, :] = hh
        a_scr[rows, :] = pp
        return hh, pp

    h_end, p_end = lax.fori_loop(0, ls, step, (jnp.zeros((SUBLANES, w), F32), jnp.ones((SUBLANES, w), F32)),
                                 unroll=4)
    c = state_scr[...]
    carries = []
    for j in range(SUBLANES):
        carries.append(c)
        c = h_end[j:j + 1, :] + p_end[j:j + 1, :] * c
    state_scr[...] = c
    cmat = jnp.concatenate(carries, axis=0)
    hfull = b_scr[...].reshape(ls, SUBLANES, w) + a_scr[...].reshape(ls, SUBLANES, w) * cmat[None]
    out = (hfull.reshape(l, w) * jax.nn.gelu(gate)).astype(BF16)
    o_ref[...] = jnp.dot(pt_ref[...], out, preferred_element_type=F32).astype(BF16)


def _lru_mixer(h, norm_g, w_x, w_g, w_s5, conv_w, conv_b, w_r, b_r, w_i, b_i, lam, *, bsz, tl):
    w = w_x.shape[1]
    perm = _segment_perm(tl)
    blockdiag = lambda m: jax.scipy.linalg.block_diag(*m).astype(BF16)
    consts = (_row(norm_g), perm, perm.T, w_x, w_g, w_s5, conv_w, _row(conv_b), blockdiag(w_r), _row(b_r),
              blockdiag(w_i), _row(b_i), _row(lam))
    scratch = [pltpu.VMEM((tl, w), F32), pltpu.VMEM((tl, w), F32), pltpu.VMEM((1, w), F32),
               pltpu.VMEM(((conv_w.shape[0] - 1) * SUBLANES, w), F32), pltpu.VMEM((w // LANES, tl, LANES), F32)]
    return _mixer_call(_lru_body, "rglru", h, consts, scratch, bsz=bsz, tl=tl, width=w, s5_input_outs=True)


def _cmul(ar, ai, br, bi):
    return ar * br - ai * bi, ar * bi + ai * br


def _split3(x):
    hi = x.astype(BF16)
    r1 = x - hi.astype(F32)
    mid = r1.astype(BF16)
    lo = (r1 - mid.astype(F32)).astype(BF16)
    return hi, mid, lo


def _mm_exact_rhs(x, e):
    return sum(jnp.dot(part, e, preferred_element_type=F32) for part in _split3(x))


def _bmm_f32(a, b):
    a3, b3 = _split3(a), _split3(b)
    acc = None
    for i, ap in enumerate(a3):
        for j, bp in enumerate(b3):
            if i + j <= 2:
                t = jnp.einsum('gij,gjk->gik', ap, bp, preferred_element_type=F32)
                acc = t if acc is None else acc + t
    return acc


def _place(x, sel):
    return jnp.einsum('gij,gjk->gik', x.astype(BF16), sel, preferred_element_type=F32)


def _lane_sel(shape, target):
    g, r, n = (lax.broadcasted_iota(jnp.int32, shape, k) for k in range(3))
    return (n == target(g, r)).astype(BF16)


def _s5_prep_body(ls_ref, arow_ref, acol_ref, bt_ref, crep_ref, kd_ref, bst_ref, cst_ref, pw_ref):
    gb, _, p = arow_ref.shape
    ch = S5_GROUP_CH
    q = S5_CHUNK
    npw = pw_ref.shape[1]
    step = jnp.exp(ls_ref[...])
    a_re, a_im = arow_ref[:, 0:1, :], arow_ref[:, 1:2, :]
    sa_re, sa_im = a_re * step, a_im * step
    mag = jnp.exp(sa_re)
    ab_re, ab_im = mag * jnp.cos(sa_im), mag * jnp.sin(sa_im)
    den = a_re * a_re + a_im * a_im
    num_re = ab_re - 1.0
    f_re = (num_re * a_re + ab_im * a_im) / den
    f_im = (ab_im * a_re - num_re * a_im) / den
    bt_re, bt_im = bt_ref[:, 0], bt_ref[:, 1]
    bb_re = f_re * bt_re - f_im * bt_im
    bb_im = f_re * bt_im + f_im * bt_re
    ns = (q - 1 - lax.broadcasted_iota(jnp.int32, (1, q, 1), 1)).astype(F32)
    pmag = jnp.exp(ns * sa_re)
    pb_re, pb_im = pmag * jnp.cos(ns * sa_im), pmag * jnp.sin(ns * sa_im)
    pb_re = jnp.broadcast_to(pb_re[:, :, None, :], (gb, q, ch, p)).reshape(gb, q * ch, p)
    pb_im = jnp.broadcast_to(pb_im[:, :, None, :], (gb, q, ch, p)).reshape(gb, q * ch, p)
    bbr = jnp.concatenate([bb_re] * q, axis=1)
    bbi = jnp.concatenate([bb_im] * q, axis=1)
    sr, si = _cmul(bbr, bbi, pb_re, pb_im)
    sin = (_place(sr, _lane_sel((gb, p, 2 * gb * p), lambda g, r: g * p + r))
           + _place(si, _lane_sel((gb, p, 2 * gb * p), lambda g, r: (gb + g) * p + r))).astype(BF16)
    for g in range(gb):
        for s in range(q):
            bst_ref[s * LANES + g * ch:s * LANES + (g + 1) * ch, :] = sin[g, s * ch:(s + 1) * ch, :]
    sc_re, sc_im = acol_ref[:, 0] * step, acol_ref[:, 1] * step
    nd = lax.broadcasted_iota(jnp.int32, (1, 1, q), 2).astype(F32)
    cmag = jnp.exp(sc_re * nd)
    pc_re, pc_im = cmag * jnp.cos(sc_im * nd), cmag * jnp.sin(sc_im * nd)
    expand = (lax.broadcasted_iota(jnp.int32, (q, q * ch), 0)
              == lax.broadcasted_iota(jnp.int32, (q, q * ch), 1) // ch).astype(BF16)
    pc_re = _mm_exact_rhs(pc_re.reshape(gb * p, q), expand).reshape(gb, p, q * ch)
    pc_im = _mm_exact_rhs(pc_im.reshape(gb * p, q), expand).reshape(gb, p, q * ch)
    ca_re, ca_im = _cmul(crep_ref[:, 0], crep_ref[:, 1], pc_re, pc_im)
    kt = _bmm_f32(bb_re, ca_re) - _bmm_f32(bb_im, ca_im)
    for d in range(q):
        sel = _lane_sel((gb, q * ch, LANES), lambda g, r, d=d: jnp.where(r // ch == d, g * ch + r % ch, -1))
        kd_ref[d] = _place(kt, sel).reshape(gb * ch, LANES).astype(BF16)
    abc_mag = jnp.exp(sc_re)
    o_re, o_im = _cmul(ca_re, ca_im, abc_mag * jnp.cos(sc_im), abc_mag * jnp.sin(sc_im))
    sel = _lane_sel((gb, q * ch, q * LANES), lambda g, r: (r // ch) * LANES + g * ch + r % ch)
    out_re, out_im = _place(o_re, sel).astype(BF16), _place(-o_im, sel).astype(BF16)
    for g in range(gb):
        cst_ref[g * p:(g + 1) * p, :] = out_re[g]
        cst_ref[(gb + g) * p:(gb + g + 1) * p, :] = out_im[g]
    nn = jnp.left_shift(1, lax.broadcasted_iota(jnp.int32, (1, npw, 1), 1)).astype(F32) * float(q)
    wmag = jnp.exp(nn * sa_re)
    sel = _lane_sel((gb, p, gb * p), lambda g, r: g * p + r)
    for k, w in enumerate((wmag * jnp.cos(nn * sa_im), wmag * jnp.sin(nn * sa_im))):
        pw_ref[k] = sum(jnp.sum(_place(part, sel), axis=0) for part in _split3(w))


def _s5_prep(log_step, a_re, a_im, b_re, b_im, c_re, c_im):
    g, p = a_re.shape
    ch, q = S5_GROUP_CH, S5_CHUNK
    gb = S5_OCT
    arow = jnp.stack([a_re, a_im], axis=1)
    acol = arow[..., None]
    bt = jnp.stack([b_re, b_im], axis=1).transpose(0, 1, 3, 2)
    crep = jnp.tile(jnp.stack([c_re, c_im], axis=1).transpose(0, 1, 3, 2), (1, 1, 1, q))
    blk = lambda *shape: pl.BlockSpec((gb,) + shape, lambda i: (i,) + (0,) * len(shape))
    out = lambda *shape: pl.BlockSpec((None,) + shape, lambda i: (i,) + (0,) * len(shape))
    nblk, lanes, ns2 = g // gb, q * LANES, 2 * gb * p
    return pl.pallas_call(
        _s5_prep_body,
        out_shape=(jax.ShapeDtypeStruct((nblk, q, LANES, LANES), BF16), jax.ShapeDtypeStruct((nblk, lanes, ns2), BF16),
                   jax.ShapeDtypeStruct((nblk, ns2, lanes), BF16), jax.ShapeDtypeStruct((nblk, 2, S5_NPOW, gb * p), F32)),
        grid=(nblk,),
        in_specs=[blk(1, 1), blk(2, p), blk(2, p, 1), blk(2, ch, p), blk(2, p, q * ch)],
        out_specs=(out(q, LANES, LANES), out(lanes, ns2), out(ns2, lanes), out(2, S5_NPOW, gb * p)),
        compiler_params=_tile_params(),
        name="s5_prep",
    )(log_step.reshape(g, 1, 1), arow, acol, bt, crep)


def _shift_rows(x, sh):
    if sh % SUBLANES == 0:
        return jnp.concatenate([jnp.zeros((sh, x.shape[1]), x.dtype), x[:x.shape[0] - sh]], axis=0)
    row = lax.broadcasted_iota(jnp.int32, x.shape, 0)
    return jnp.where(row >= sh, pltpu.roll(x, sh, 0), 0.0)


def _s5_core_body(x_ref, kd_ref, bst_ref, cst_ref, pw_ref, o_ref, toep_scr):
    rb, lanes = o_ref.shape
    q = S5_CHUNK
    ns = bst_ref.shape[1] // 2

    @pl.when(pl.program_id(1) == 0)
    def _():
        toep_scr[...] = jnp.zeros_like(toep_scr)
        for s in range(q):
            for t in range(s, q):
                toep_scr[s * LANES:(s + 1) * LANES, t * LANES:(t + 1) * LANES] = kd_ref[t - s]

    z = x_ref[...]
    ys = []
    for tp in range(0, q, 2):
        k = (tp + 2) * LANES
        ys.append(jnp.dot(z[:, :k], toep_scr[:k, tp * LANES:(tp + 2) * LANES], preferred_element_type=F32))
    g = jnp.dot(z, bst_ref[...], preferred_element_type=F32)
    e_re, e_im = g[:, :ns], g[:, ns:]
    sh, m = 1, 0
    while sh < rb:
        sr, si = _shift_rows(e_re, sh), _shift_rows(e_im, sh)
        pr, pi = _cmul(pw_ref[0, m:m + 1, :], pw_ref[1, m:m + 1, :], sr, si)
        e_re, e_im = e_re + pr, e_im + pi
        sh, m = sh * 2, m + 1
    s_re = _shift_rows(e_re, 1).astype(BF16)
    s_im = _shift_rows(e_im, 1).astype(BF16)
    o_ref[...] = (jnp.concatenate(ys, axis=1) + jnp.dot(s_re, cst_ref[:ns, :], preferred_element_type=F32)
                  + jnp.dot(s_im, cst_ref[ns:, :], preferred_element_type=F32))


def _s5_core(x, log_step, a_re, a_im, b_re, b_im, c_re, c_im, *, bsz):
    nblk, nrows, lanes = x.shape
    ch, q, p = S5_GROUP_CH, S5_CHUNK, S5_STATE
    o8 = S5_OCT
    rb = nrows // bsz
    kd, bst_o, cst_o, pw_o = _s5_prep(log_step, a_re, a_im, b_re, b_im, c_re, c_im)
    wspec = lambda *shape: pl.BlockSpec((None,) + shape, lambda o, b: (o,) + (0,) * len(shape),
                                        pipeline_mode=pl.Buffered(1))
    y = pl.pallas_call(
        _s5_core_body,
        out_shape=jax.ShapeDtypeStruct((nblk, nrows, lanes), F32),
        grid=(nblk, bsz),
        in_specs=[pl.BlockSpec((None, rb, lanes), lambda o, b: (o, b, 0)),
                  wspec(q, LANES, LANES), wspec(lanes, 2 * o8 * p), wspec(2 * o8 * p, lanes),
                  wspec(2, S5_NPOW, o8 * p)],
        out_specs=pl.BlockSpec((None, rb, lanes), lambda o, b: (o, b, 0)),
        scratch_shapes=[pltpu.VMEM((lanes, lanes), BF16)],
        compiler_params=_seq_params(),
        name="s5_core",
    )(x, kd, bst_o, cst_o, pw_o)
    return y


def _causal_conv(x, xpad_scr, tail_scr, cw_ref, first):
    l = x.shape[0]
    k = cw_ref.shape[0]

    @pl.when(first)
    def _():
        tail_scr[...] = jnp.zeros_like(tail_scr)

    xpad_scr[:SUBLANES, :] = tail_scr[...]
    xpad_scr[SUBLANES:, :] = x
    tail_scr[...] = x[l - SUBLANES:, :]
    off = SUBLANES - (k - 1)
    return sum(cw_ref[j:j + 1, :] * xpad_scr[off + j:off + j + l, :] for j in range(k))


def _split_dot(tri, x):
    hi = x.astype(BF16)
    lo = (x - hi.astype(F32)).astype(BF16)
    return jnp.dot(tri, hi, preferred_element_type=F32) + jnp.dot(tri, lo, preferred_element_type=F32)


def _lower_tri(q):
    return lax.broadcasted_iota(jnp.int32, (q, q), 0) >= lax.broadcasted_iota(jnp.int32, (q, q), 1)


def _decay_matrix(cs, cst, lane, causal):
    diff = cs[:, lane:lane + 1] - cst[lane:lane + 1, :]
    return jnp.where(causal, jnp.exp(jnp.where(causal, diff, 0.0)), 0.0)


def _ssd_body(h_ref, g_ref, wz_ref, wx_ref, wdt_ref, cw_ref, cb_ref, dtb_ref, alog_ref, dsk_ref, ng_ref, o_ref,
              xpad_scr, tail_scr, state_scr):
    l, w = o_ref.shape
    ngroups, nstate, gw = state_scr.shape
    q = M2_CHUNK
    first = pl.program_id(1) == 0

    @pl.when(first)
    def _():
        state_scr[...] = jnp.zeros_like(state_scr)

    u = _rms(h_ref[...], g_ref[...]).astype(BF16)
    z = jnp.dot(u, wz_ref[...], preferred_element_type=F32)
    xbc_raw = jnp.dot(u, wx_ref[...], preferred_element_type=F32)
    dt_raw = jnp.dot(u, wdt_ref[...], preferred_element_type=F32)
    xbc = _silu(_causal_conv(xbc_raw, xpad_scr, tail_scr, cw_ref, first) + cb_ref[...])
    xs = xbc[:, :w]
    bm = xbc[:, w:w + ngroups * nstate]
    cm = xbc[:, w + ngroups * nstate:]
    dt = jax.nn.softplus(dt_raw + dtb_ref[...])
    da = dt * (-jnp.exp(alog_ref[...]))
    xdt = xs * dt

    causal = _lower_tri(q)
    tri = causal.astype(BF16)
    head_of_lane = lax.broadcasted_iota(jnp.int32, (1, gw), 1) // M2_HEAD_DIM

    states = [state_scr[g] for g in range(ngroups)]
    for c in range(l // q):
        rows = slice(c * q, (c + 1) * q)
        cs = _split_dot(tri, da[rows])
        cs_last = cs[q - 1:q, :]
        dte = jnp.exp(cs_last - cs)
        ecs = jnp.exp(cs)
        cdec = jnp.exp(cs_last)
        cst = cs.T
        ychunk = []
        for g in range(ngroups):
            gl = slice(g * gw, (g + 1) * gw)
            bg = bm[rows, g * nstate:(g + 1) * nstate]
            cg = cm[rows, g * nstate:(g + 1) * nstate]
            cb = _mm_nt(cg, bg)
            xg = xdt[rows, gl]
            yg = _mm(cg, states[g]) * ecs[:, gl]
            for hh in range(gw // M2_HEAD_DIM):
                dec = _decay_matrix(cs, cst, g * gw + hh * M2_HEAD_DIM, causal)
                yg = yg + _mm(cb * dec, jnp.where(head_of_lane == hh, xg, 0.0))
            states[g] = states[g] * cdec[:, gl] + _mm(bg.T, xg * dte[:, gl])
            ychunk.append(yg)
        y = jnp.concatenate(ychunk, axis=1) + dsk_ref[...] * xs[rows]
        y = y * _silu(z[rows])
        o_ref[rows, :] = _rms(y, ng_ref[...]).astype(BF16)
    for g in range(ngroups):
        state_scr[g] = states[g]


def _ssd_mixer(h, norm_g, w_z, w_xbc, w_dt, conv_w, conv_b, dt_bias, a_log, d_skip, out_norm_g, *, bsz, tl):
    w = w_z.shape[1]
    cdim = w_xbc.shape[1]
    rep = lambda v: jnp.repeat(v, M2_HEAD_DIM).reshape(1, w)
    consts = (_row(norm_g), w_z, w_xbc, jnp.repeat(w_dt, M2_HEAD_DIM, axis=1), conv_w, _row(conv_b), rep(dt_bias),
              rep(a_log), rep(d_skip), _row(out_norm_g))
    scratch = [pltpu.VMEM((tl + SUBLANES, cdim), F32), pltpu.VMEM((SUBLANES, cdim), F32),
               pltpu.VMEM((M2_GROUPS, M2_STATE, w // M2_GROUPS), F32)]
    return _mixer_call(_ssd_body, "ssd", h, consts, scratch, bsz=bsz, tl=tl, width=w)


def _l2norm(x):
    return x * lax.rsqrt(jnp.sum(x * x, axis=-1, keepdims=True) + EPS)


def _gdn_body(h_ref, g_ref, wqkv_ref, wg_ref, wb_ref, wa_ref, cw_ref, dtb_ref, alog_ref, ng_ref, o_ref,
              xpad_scr, tail_scr, state_scr):
    l, w = o_ref.shape
    nheads, dk, dv = state_scr.shape
    q = GDN_CHUNK
    sq = nheads * q
    nchunks = l // q
    first = pl.program_id(1) == 0

    @pl.when(first)
    def _():
        state_scr[...] = jnp.zeros_like(state_scr)

    u = _rms(h_ref[...], g_ref[...]).astype(BF16)
    qkv_raw = jnp.dot(u, wqkv_ref[...], preferred_element_type=F32)
    gate = jnp.dot(u, wg_ref[...], preferred_element_type=F32)
    beta = jax.nn.sigmoid(jnp.dot(u, wb_ref[...], preferred_element_type=F32))
    a_raw = jnp.dot(u, wa_ref[...], preferred_element_type=F32)
    glog = -jnp.exp(alog_ref[...]) * jax.nn.softplus(a_raw + dtb_ref[...])
    qkv = _silu(_causal_conv(qkv_raw, xpad_scr, tail_scr, cw_ref, first))

    tri = _lower_tri(q).astype(BF16)
    gcs_c = [_split_dot(tri, glog[c * q:(c + 1) * q]) for c in range(nchunks)]
    gcs = jnp.concatenate(gcs_c, axis=0)
    gtot = jnp.concatenate([jnp.broadcast_to(g_[q - 1:q, :], (q, w)) for g_ in gcs_c], axis=0)
    egcs = jnp.exp(gcs)

    def stack(x, c):
        return jnp.concatenate([x[c * q:(c + 1) * q, hd * dk:(hd + 1) * dk] for hd in range(nheads)], axis=0)

    qn = jnp.concatenate([_l2norm(qkv[:, hd * dk:(hd + 1) * dk]) * (dk ** -0.5) for hd in range(nheads)], axis=1)
    kn = jnp.concatenate([_l2norm(qkv[:, w + hd * dk:w + (hd + 1) * dk]) for hd in range(nheads)], axis=1)
    kb = kn * beta
    qd = qn * egcs
    kd = kn * jnp.exp(gtot - gcs)
    kbe = kb * egcs
    vb = qkv[:, 2 * w:] * beta

    rs = lax.broadcasted_iota(jnp.int32, (sq, sq), 0)
    cs = lax.broadcasted_iota(jnp.int32, (sq, sq), 1)
    same_head = (rs // q) == (cs // q)
    causal = same_head & (rs >= cs)
    strict = same_head & (rs > cs)
    eye = (rs == cs).astype(F32)
    head_block = (lax.broadcasted_iota(jnp.int32, (sq, nheads * dv), 0) // q
                  == lax.broadcasted_iota(jnp.int32, (sq, nheads * dv), 1) // dv)

    states = [state_scr[hd] for hd in range(nheads)]
    for c0 in range(0, nchunks, GDN_INTERLEAVE):
        cg = list(range(c0, min(c0 + GDN_INTERLEAVE, nchunks)))
        k_s = [stack(kn, c) for c in cg]
        g_s = [stack(gcs, c) for c in cg]
        g_t = jnp.concatenate(g_s, axis=0).T
        decay = []
        for i in range(len(cg)):
            diff = jnp.concatenate([g_s[i]] * (sq // dk), axis=1) - g_t[0:1, i * sq:(i + 1) * sq]
            decay.append(jnp.where(causal, jnp.exp(jnp.where(causal, diff, 0.0)), 0.0))
        kk = [_mm_nt(stack(kb, c), k_s[i]) for i, c in enumerate(cg)]
        qk = [_mm_nt(stack(qn, c), k_s[i]) * decay[i] for i, c in enumerate(cg)]
        pw = [jnp.where(strict, -(kk[i] * decay[i]), 0.0) for i in range(len(cg))]
        inv = [eye + p_ for p_ in pw]
        span = 2
        while span < q:
            pw = [_mm(p_, p_) for p_ in pw]
            inv = [a_ + _mm(a_, p_) for a_, p_ in zip(inv, pw)]
            span *= 2
        sol = [_mm(inv[i], jnp.concatenate([stack(vb, c), stack(kbe, c)], axis=1)) for i, c in enumerate(cg)]
        kd_t = jnp.concatenate([stack(kd, c) for c in cg], axis=0).T

        for i, c in enumerate(cg):
            rows = slice(c * q, (c + 1) * q)
            qd_s = stack(qd, c)
            r = [_mm(jnp.concatenate([sol[i][hd * q:(hd + 1) * q, dv:], qd_s[hd * q:(hd + 1) * q]], axis=0), states[hd])
                 for hd in range(nheads)]
            v_new = jnp.concatenate([sol[i][hd * q:(hd + 1) * q, :dv] - r[hd][:q] for hd in range(nheads)], axis=0)
            o_s = jnp.concatenate([r[hd][q:] for hd in range(nheads)], axis=0) + _mm(qk[i], v_new)
            v_bd = jnp.where(head_block, jnp.concatenate([v_new] * nheads, axis=1), 0.0)
            upd = _mm(kd_t[:, i * sq:(i + 1) * sq], v_bd)
            cdec = jnp.exp(gtot[c * q:c * q + 1, :])
            for hd in range(nheads):
                hl = slice(hd * dk, (hd + 1) * dk)
                states[hd] = states[hd] * cdec[:, hl] + upd[:, hl]
                o_ref[rows, hl] = (_rms(o_s[hd * q:(hd + 1) * q], ng_ref[...]) * _silu(gate[rows, hl])).astype(BF16)
    for hd in range(nheads):
        state_scr[hd] = states[hd]


def _gdn_mixer(h, norm_g, w_qkv, w_g, w_beta, w_a, conv_w, dt_bias, a_log, out_norm_g, *, bsz, tl):
    w = w_g.shape[1]
    cdim = w_qkv.shape[1]
    rep = lambda v: jnp.repeat(v, GDN_HEAD_DIM).reshape(1, w)
    consts = (_row(norm_g), w_qkv, w_g, jnp.repeat(w_beta, GDN_HEAD_DIM, axis=1),
              jnp.repeat(w_a, GDN_HEAD_DIM, axis=1), conv_w, rep(dt_bias), rep(a_log), _row(out_norm_g))
    scratch = [pltpu.VMEM((tl + SUBLANES, cdim), F32), pltpu.VMEM((SUBLANES, cdim), F32),
               pltpu.VMEM((w // GDN_HEAD_DIM, GDN_HEAD_DIM, GDN_HEAD_DIM), F32)]
    return _mixer_call(_gdn_body, "gdn", h, consts, scratch, bsz=bsz, tl=tl, width=w)


def _col_slices(sizes):
    out, start = [], 0
    for size in sizes:
        out.append(slice(start, start + size))
        start += size
    return out


def kernel(x, p, ffn1_norm, ffn1_w_in, ffn1_w_out, mix_norm, w_in, w_gate, b_gate, s5_log_step, s5_a_re, s5_a_im, s5_b_re, s5_b_im, s5_c_re, s5_c_im, s5_d, s5_w_glu, s5_b_glu, lru_conv_w, lru_conv_b, lru_w_r, lru_b_r, lru_w_i, lru_b_i, lru_lambda, m2_conv_w, m2_conv_b, m2_dt_bias, m2_a_log, m2_d, m2_norm, gdn_conv_w, gdn_dt_bias, gdn_a_log, gdn_norm, w_branch, w_out, ffn2_norm, ffn2_w_in, ffn2_w_out, ple_norm, ple_w_gate, ple_w_proj, final_norm):
    bsz, s, d = x.shape
    depth = p.shape[0]
    t = bsz * s
    h = x.reshape(t, d)
    p = p.reshape(depth, t, -1)
    tm, tl = 512, 512
    bf = lambda a: a.astype(BF16)
    ffn1_w_in, ffn1_w_out, ffn2_w_in, ffn2_w_out = bf(ffn1_w_in), bf(ffn1_w_out), bf(ffn2_w_in), bf(ffn2_w_out)
    w_gate, w_branch, w_out, s5_w_glu = bf(w_gate), bf(w_branch), bf(w_out), bf(s5_w_glu)
    ple_w_gate, ple_w_proj = bf(ple_w_gate), bf(ple_w_proj)
    for i in range(depth):
        h = _ffn(h, ffn1_norm[i], ffn1_w_in, ffn1_w_out, layer=i, tm=tm)
        (w_s5, w_lx, w_lg, w_mz, w_mx, w_mdt, w_gqkv, w_gg, w_gb, w_ga) = [bf(w_in[i][:, c]) for c in _col_slices(IN_SPLITS)]
        y_b, su, su_folded = _lru_mixer(h, mix_norm[i], w_lx, w_lg, w_s5, lru_conv_w[i], lru_conv_b[i], lru_w_r[i],
                                        lru_b_r[i], lru_w_i[i], lru_b_i[i], lru_lambda[i], bsz=bsz, tl=tl)
        y_s5 = _s5_core(su_folded, s5_log_step[i], s5_a_re[i], s5_a_im[i], s5_b_re[i], s5_b_im[i], s5_c_re[i],
                        s5_c_im[i], bsz=bsz)
        y_c = _ssd_mixer(h, mix_norm[i], w_mz, w_mx, w_mdt, m2_conv_w[i], m2_conv_b[i], m2_dt_bias[i],
                         m2_a_log[i], m2_d[i], m2_norm[i], bsz=bsz, tl=tl)
        y_d = _gdn_mixer(h, mix_norm[i], w_gqkv, w_gg, w_gb, w_ga, gdn_conv_w[i], gdn_dt_bias[i], gdn_a_log[i],
                         gdn_norm[i], bsz=bsz, tl=tl)
        h = _merge(h, mix_norm[i], y_s5, su, s5_d[i], s5_w_glu, s5_b_glu[i], (y_b, y_c, y_d), w_gate, b_gate[i],
                   w_branch, w_out, layer=i, tm=tm)
        h = _ffn(h, ffn2_norm[i], ffn2_w_in, ffn2_w_out, layer=i, tm=tm)
        h = _ple(h, ple_norm[i], p, ple_w_gate, ple_w_proj, final_norm, layer=i, tm=tm,
                 final_norm=(i == depth - 1))
    return h.reshape(bsz, s, d)
```

```python
import functools

import jax
import jax.numpy as jnp
from jax import lax
from jax.experimental import pallas as pl
from jax.experimental.pallas import tpu as pltpu

EPS = 1e-6
BF16 = jnp.bfloat16
F32 = jnp.float32

V7X_VMEM_BYTES = 64 * 1024 * 1024
VMEM_LIMIT_BYTES = V7X_VMEM_BYTES - 8 * 1024 * 1024
SUBLANES = 8
LANES = 128

BRANCH_WIDTH = 512
S5_GROUP_CH = 16
S5_STATE = 64
S5_CHUNK = 16
S5_OCT = LANES // S5_GROUP_CH
S5_NPOW = 16
LRU_C = 8.0
M2_HEAD_DIM = 64
M2_HEADS = BRANCH_WIDTH // M2_HEAD_DIM
M2_GROUPS = 2
M2_STATE = 128
M2_CONV_DIM = BRANCH_WIDTH + 2 * M2_GROUPS * M2_STATE
M2_CHUNK = 128
GDN_HEAD_DIM = 128
GDN_HEADS = BRANCH_WIDTH // GDN_HEAD_DIM
GDN_CHUNK = 64
GDN_INTERLEAVE = 4
MXU_WIDTH = 256
IN_SPLITS = (BRANCH_WIDTH, BRANCH_WIDTH, BRANCH_WIDTH, BRANCH_WIDTH, M2_CONV_DIM, M2_HEADS, 3 * BRANCH_WIDTH, BRANCH_WIDTH, GDN_HEADS, GDN_HEADS)


def _rms(x, g):
    return x * lax.rsqrt(jnp.mean(x * x, axis=-1, keepdims=True) + EPS) * g


def _silu(x):
    return x * jax.nn.sigmoid(x)


def _mm(a, b):
    return jnp.dot(a.astype(BF16), b.astype(BF16), preferred_element_type=F32)


def _mm_nt(a, b):
    return lax.dot_general(a.astype(BF16), b.astype(BF16), (((1,), (1,)), ((), ())), preferred_element_type=F32)


def _const_spec(shape):
    return pl.BlockSpec(shape, lambda *_: (0,) * len(shape))


def _row(v):
    return v.reshape(1, -1)


def _seq_params():
    return pltpu.CompilerParams(dimension_semantics=("parallel", "arbitrary"), vmem_limit_bytes=VMEM_LIMIT_BYTES)


def _ffn_body(h_ref, g_ref, wi_ref, wo_ref, *rest, ple, final_norm):
    o_ref = rest[-1]
    f = wo_ref.shape[0]
    h = h_ref[...]
    xn = _rms(h, g_ref[...]).astype(BF16)
    acts = []
    for s in range(0, f, MXU_WIDTH):
        gate = jnp.dot(xn, wi_ref[:, s:s + MXU_WIDTH], preferred_element_type=F32)
        up = jnp.dot(xn, wi_ref[:, f + s:f + s + MXU_WIDTH], preferred_element_type=F32)
        acts.append((_silu(gate) * up).astype(BF16))
    out = h + 0.5 * jnp.dot(jnp.concatenate(acts, axis=1), wo_ref[...], preferred_element_type=F32)
    if ple:
        pg_ref, p_ref, wg_ref, wp_ref, fg_ref = rest[:-1]
        out = out + jax.nn.sigmoid(_mm(_rms(out, pg_ref[...]), wg_ref[...])) * _mm(p_ref[...], wp_ref[...])
        if final_norm:
            out = _rms(out, fg_ref[...])
    o_ref[...] = out


def _layer_spec(stacked, layer):
    shape = stacked.shape[1:]
    return pl.BlockSpec((None,) + shape, lambda *_: (layer,) + (0,) * len(shape), pipeline_mode=pl.Buffered(1))


def _tile_params():
    return pltpu.CompilerParams(dimension_semantics=("parallel",), vmem_limit_bytes=VMEM_LIMIT_BYTES)


def _ffn(h, norm_g, w_in, w_out, *, layer, tm, ple=None, final_norm=False):
    t, d = h.shape
    in_specs = [pl.BlockSpec((tm, d), lambda i: (i, 0)), _const_spec((1, d)),
                _layer_spec(w_in, layer), _layer_spec(w_out, layer)]
    args = [h, _row(norm_g), w_in, w_out]
    if ple is not None:
        pg, p, wg, wp, fg = ple
        in_specs += [_const_spec((1, d)), pl.BlockSpec((None, tm, p.shape[-1]), lambda i: (layer, i, 0)),
                     _layer_spec(wg, layer), _layer_spec(wp, layer), _const_spec((1, d))]
        args += [_row(pg), p, wg, wp, _row(fg)]
    return pl.pallas_call(
        functools.partial(_ffn_body, ple=ple is not None, final_norm=final_norm),
        out_shape=jax.ShapeDtypeStruct((t, d), F32),
        grid=(t // tm,),
        in_specs=in_specs,
        out_specs=pl.BlockSpec((tm, d), lambda i: (i, 0)),
        compiler_params=_tile_params(),
        name="ffn",
    )(*args)


def _merge_body(h_ref, g_ref, ys5_ref, su_ref, dsk_ref, wglu_ref, bglu_ref, yb_ref, yc_ref, yd_ref, wg_ref, bg_ref,
                wb_ref, wo_ref, o_ref, stage_scr):
    h = h_ref[...]
    d = h.shape[-1]
    u = _rms(h, g_ref[...]).astype(BF16)
    z = jax.nn.gelu(_unfold_chunks(ys5_ref, stage_scr) + dsk_ref[...] * su_ref[...])
    ya = (z * jax.nn.sigmoid(_mm(z, wglu_ref[...]) + bglu_ref[...])).astype(BF16)
    mixed = jnp.zeros(h.shape, F32)
    for n, y in enumerate((ya, yb_ref[...], yc_ref[...], yd_ref[...])):
        gate = jax.nn.sigmoid(
            jnp.dot(u, wg_ref[:, n * d:(n + 1) * d], preferred_element_type=F32) + bg_ref[:, n * d:(n + 1) * d])
        mixed = mixed + gate * jnp.dot(y, wb_ref[n], preferred_element_type=F32)
    o_ref[...] = h + _mm(mixed, wo_ref[...])


def _merge(h, norm_g, y_s5, su, d_skip, w_glu, b_glu, ys, w_gate, b_gate, w_branch, w_out, *, layer, tm):
    t, d = h.shape
    nblk, _, fold = y_s5.shape
    w = nblk * LANES
    y_spec = pl.BlockSpec((tm, w), lambda i: (i, 0))
    return pl.pallas_call(
        _merge_body,
        out_shape=jax.ShapeDtypeStruct((t, d), F32),
        grid=(t // tm,),
        in_specs=[
            pl.BlockSpec((tm, d), lambda i: (i, 0)), _const_spec((1, d)),
            pl.BlockSpec((nblk, tm // S5_CHUNK, fold), lambda i: (0, i, 0)), y_spec,
            _const_spec((1, w)), _layer_spec(w_glu, layer), _const_spec((1, w)),
            y_spec, y_spec, y_spec,
            _layer_spec(w_gate, layer), _const_spec((1, b_gate.shape[-1])),
            _layer_spec(w_branch, layer), _layer_spec(w_out, layer),
        ],
        out_specs=pl.BlockSpec((tm, d), lambda i: (i, 0)),
        scratch_shapes=[pltpu.VMEM((nblk, tm, LANES), F32)],
        compiler_params=_tile_params(),
        name="merge",
    )(h, _row(norm_g), y_s5, su, _row(d_skip), w_glu, _row(b_glu), *ys, w_gate, _row(b_gate), w_branch, w_out)


def _segment_perm(l):
    r = jnp.arange(l)
    src = (r % SUBLANES) * (l // SUBLANES) + r // SUBLANES
    return (src[:, None] == jnp.arange(l)[None, :]).astype(BF16)


def _mixer_call(body, name, h, consts, scratch_shapes, *, bsz, tl, width, s5_input_outs=False):
    t, d = h.shape
    nt = t // bsz // tl
    out_shape = [jax.ShapeDtypeStruct((t, width), BF16)]
    out_specs = [pl.BlockSpec((tl, width), lambda b, i: (b * nt + i, 0))]
    if s5_input_outs:
        nblk, fold = width // LANES, S5_CHUNK * LANES
        out_shape += [jax.ShapeDtypeStruct((t, width), F32), jax.ShapeDtypeStruct((nblk, t // S5_CHUNK, fold), BF16)]
        out_specs += [pl.BlockSpec((tl, width), lambda b, i: (b * nt + i, 0)),
                      pl.BlockSpec((nblk, tl // S5_CHUNK, fold), lambda b, i: (0, b * nt + i, 0))]
    outs = pl.pallas_call(
        body,
        out_shape=tuple(out_shape),
        grid=(bsz, nt),
        in_specs=[pl.BlockSpec((tl, d), lambda b, i: (b * nt + i, 0))] + [_const_spec(c.shape) for c in consts],
        out_specs=tuple(out_specs),
        scratch_shapes=scratch_shapes,
        compiler_params=_seq_params(),
        name=name,
    )(h, *consts)
    return outs if s5_input_outs else outs[0]


def _fold_chunks(x, stage_scr, out_ref):
    rows = x.shape[0]
    for j in range(stage_scr.shape[0]):
        stage_scr[j] = x[:, j * LANES:(j + 1) * LANES]
    for j in range(stage_scr.shape[0]):
        for s in range(S5_CHUNK):
            out_ref[j, :, s * LANES:(s + 1) * LANES] = (
                stage_scr[j, pl.ds(s, rows // S5_CHUNK, stride=S5_CHUNK), :].astype(out_ref.dtype))


def _unfold_chunks(in_ref, stage_scr):
    rows = stage_scr.shape[1]
    for j in range(stage_scr.shape[0]):
        for s in range(S5_CHUNK):
            stage_scr[j, pl.ds(s, rows // S5_CHUNK, stride=S5_CHUNK), :] = in_ref[j, :, s * LANES:(s + 1) * LANES]
    return jnp.concatenate([stage_scr[j] for j in range(stage_scr.shape[0])], axis=1)


def _conv_halo(cur_tail, prev_tail):
    row = lax.broadcasted_iota(jnp.int32, cur_tail.shape, 0) % SUBLANES
    slab = jnp.where(row == SUBLANES - 1, prev_tail, cur_tail)
    n = cur_tail.shape[0] // SUBLANES
    return jnp.concatenate(
        [pltpu.roll(slab[g * SUBLANES:(g + 1) * SUBLANES], 1, 0) for g in range(n)], axis=0)


def _lru_body(h_ref, g_ref, p_ref, pt_ref, wx_ref, wg_ref, ws5_ref, cw_ref, cb_ref, wr_ref, br_ref, wi_ref, bi_ref,
              lam_ref, o_ref, su_ref, suf_ref, a_scr, b_scr, state_scr, tail_scr, stage_scr):
    l, w = a_scr.shape
    ls = l // SUBLANES
    kconv = cw_ref.shape[0]
    halo = (kconv - 1) * SUBLANES

    @pl.when(pl.program_id(1) == 0)
    def _():
        state_scr[...] = jnp.zeros_like(state_scr)
        tail_scr[...] = jnp.zeros_like(tail_scr)

    u = _rms(h_ref[...], g_ref[...]).astype(BF16)
    su = jnp.dot(u, ws5_ref[...], preferred_element_type=F32)
    su_ref[...] = su
    _fold_chunks(su, stage_scr, suf_ref)
    up = jnp.dot(p_ref[...], u, preferred_element_type=F32).astype(BF16)
    x = jnp.dot(up, wx_ref[...], preferred_element_type=F32)
    gate = jnp.dot(up, wg_ref[...], preferred_element_type=F32)

    cur_tail = x[l - halo:, :]
    xpad = jnp.concatenate([_conv_halo(cur_tail, tail_scr[...]), x], axis=0)
    tail_scr[...] = cur_tail
    xc = cb_ref[...] + sum(cw_ref[k:k + 1, :] * xpad[k * SUBLANES:k * SUBLANES + l] for k in range(kconv))

    r = jax.nn.sigmoid(_mm(xc, wr_ref[...]) + br_ref[...])
    ig = jax.nn.sigmoid(_mm(xc, wi_ref[...]) + bi_ref[...])
    log_a = (-LRU_C * jax.nn.softplus(-lam_ref[...])) * r
    a_scr[...] = jnp.exp(log_a)
    b_scr[...] = jnp.sqrt(-jnp.tanh(log_a) * (jnp.exp(2.0 * log_a) + 1.0)) * (ig * xc)

    def step(i, carry):
        hh, pp = carry
        rows = pl.ds(pl.multiple_of(i * SUBLANES, SUBLANES), SUBLANES)
        a_i = a_scr[rows, :]
        hh = a_i * hh + b_scr[rows, :]
        pp = a_i * pp
        b_scr[rows, :] = hh
        a_scr[rows, :] = pp
        return hh, pp

    h_end, p_end = lax.fori_loop(0, ls, step, (jnp.zeros((SUBLANES, w), F32), jnp.ones((SUBLANES, w), F32)),
                                 unroll=4)
    c = state_scr[...]
    carries = []
    for j in range(SUBLANES):
        carries.append(c)
        c = h_end[j:j + 1, :] + p_end[j:j + 1, :] * c
    state_scr[...] = c
    cmat = jnp.concatenate(carries, axis=0)
    hfull = b_scr[...].reshape(ls, SUBLANES, w) + a_scr[...].reshape(ls, SUBLANES, w) * cmat[None]
    out = (hfull.reshape(l, w) * jax.nn.gelu(gate)).astype(BF16)
    o_ref[...] = jnp.dot(pt_ref[...], out, preferred_element_type=F32).astype(BF16)


def _lru_mixer(h, norm_g, w_x, w_g, w_s5, conv_w, conv_b, w_r, b_r, w_i, b_i, lam, *, bsz, tl):
    w = w_x.shape[1]
    perm = _segment_perm(tl)
    blockdiag = lambda m: jax.scipy.linalg.block_diag(*m).astype(BF16)
    consts = (_row(norm_g), perm, perm.T, w_x, w_g, w_s5, conv_w, _row(conv_b), blockdiag(w_r), _row(b_r),
              blockdiag(w_i), _row(b_i), _row(lam))
    scratch = [pltpu.VMEM((tl, w), F32), pltpu.VMEM((tl, w), F32), pltpu.VMEM((1, w), F32),
               pltpu.VMEM(((conv_w.shape[0] - 1) * SUBLANES, w), F32), pltpu.VMEM((w // LANES, tl, LANES), F32)]
    return _mixer_call(_lru_body, "rglru", h, consts, scratch, bsz=bsz, tl=tl, width=w, s5_input_outs=True)


def _cmul(ar, ai, br, bi):
    return ar * br - ai * bi, ar * bi + ai * br


def _split3(x):
    hi = x.astype(BF16)
    r1 = x - hi.astype(F32)
    mid = r1.astype(BF16)
    lo = (r1 - mid.astype(F32)).astype(BF16)
    return hi, mid, lo


def _mm_exact_rhs(x, e):
    return sum(jnp.dot(part, e, preferred_element_type=F32) for part in _split3(x))


def _bmm_f32(a, b):
    a3, b3 = _split3(a), _split3(b)
    acc = None
    for i, ap in enumerate(a3):
        for j, bp in enumerate(b3):
            if i + j <= 2:
                t = jnp.einsum('gij,gjk->gik', ap, bp, preferred_element_type=F32)
                acc = t if acc is None else acc + t
    return acc


def _place(x, sel):
    return jnp.einsum('gij,gjk->gik', x.astype(BF16), sel, preferred_element_type=F32)


def _lane_sel(shape, target):
    g, r, n = (lax.broadcasted_iota(jnp.int32, shape, k) for k in range(3))
    return (n == target(g, r)).astype(BF16)


def _s5_prep_body(ls_ref, arow_ref, acol_ref, bt_ref, crep_ref, kd_ref, bst_ref, cst_ref, pw_ref):
    gb, _, p = arow_ref.shape
    ch = S5_GROUP_CH
    q = S5_CHUNK
    npw = pw_ref.shape[1]
    step = jnp.exp(ls_ref[...])
    a_re, a_im = arow_ref[:, 0:1, :], arow_ref[:, 1:2, :]
    sa_re, sa_im = a_re * step, a_im * step
    mag = jnp.exp(sa_re)
    ab_re, ab_im = mag * jnp.cos(sa_im), mag * jnp.sin(sa_im)
    den = a_re * a_re + a_im * a_im
    num_re = ab_re - 1.0
    f_re = (num_re * a_re + ab_im * a_im) / den
    f_im = (ab_im * a_re - num_re * a_im) / den
    bt_re, bt_im = bt_ref[:, 0], bt_ref[:, 1]
    bb_re = f_re * bt_re - f_im * bt_im
    bb_im = f_re * bt_im + f_im * bt_re
    ns = (q - 1 - lax.broadcasted_iota(jnp.int32, (1, q, 1), 1)).astype(F32)
    pmag = jnp.exp(ns * sa_re)
    pb_re, pb_im = pmag * jnp.cos(ns * sa_im), pmag * jnp.sin(ns * sa_im)
    pb_re = jnp.broadcast_to(pb_re[:, :, None, :], (gb, q, ch, p)).reshape(gb, q * ch, p)
    pb_im = jnp.broadcast_to(pb_im[:, :, None, :], (gb, q, ch, p)).reshape(gb, q * ch, p)
    bbr = jnp.concatenate([bb_re] * q, axis=1)
    bbi = jnp.concatenate([bb_im] * q, axis=1)
    sr, si = _cmul(bbr, bbi, pb_re, pb_im)
    sin = (_place(sr, _lane_sel((gb, p, 2 * gb * p), lambda g, r: g * p + r))
           + _place(si, _lane_sel((gb, p, 2 * gb * p), lambda g, r: (gb + g) * p + r))).astype(BF16)
    for g in range(gb):
        for s in range(q):
            bst_ref[s * LANES + g * ch:s * LANES + (g + 1) * ch, :] = sin[g, s * ch:(s + 1) * ch, :]
    sc_re, sc_im = acol_ref[:, 0] * step, acol_ref[:, 1] * step
    nd = lax.broadcasted_iota(jnp.int32, (1, 1, q), 2).astype(F32)
    cmag = jnp.exp(sc_re * nd)
    pc_re, pc_im = cmag * jnp.cos(sc_im * nd), cmag * jnp.sin(sc_im * nd)
    expand = (lax.broadcasted_iota(jnp.int32, (q, q * ch), 0)
              == lax.broadcasted_iota(jnp.int32, (q, q * ch), 1) // ch).astype(BF16)
    pc_re = _mm_exact_rhs(pc_re.reshape(gb * p, q), expand).reshape(gb, p, q * ch)
    pc_im = _mm_exact_rhs(pc_im.reshape(gb * p, q), expand).reshape(gb, p, q * ch)
    ca_re, ca_im = _cmul(crep_ref[:, 0], crep_ref[:, 1], pc_re, pc_im)
    kt = _bmm_f32(bb_re, ca_re) - _bmm_f32(bb_im, ca_im)
    for d in range(q):
        sel = _lane_sel((gb, q * ch, LANES), lambda g, r, d=d: jnp.where(r // ch == d, g * ch + r % ch, -1))
        kd_ref[d] = _place(kt, sel).reshape(gb * ch, LANES).astype(BF16)
    abc_mag = jnp.exp(sc_re)
    o_re, o_im = _cmul(ca_re, ca_im, abc_mag * jnp.cos(sc_im), abc_mag * jnp.sin(sc_im))
    sel = _lane_sel((gb, q * ch, q * LANES), lambda g, r: (r // ch) * LANES + g * ch + r % ch)
    out_re, out_im = _place(o_re, sel).astype(BF16), _place(-o_im, sel).astype(BF16)
    for g in range(gb):
        cst_ref[g * p:(g + 1) * p, :] = out_re[g]
        cst_ref[(gb + g) * p:(gb + g + 1) * p, :] = out_im[g]
    nn = jnp.left_shift(1, lax.broadcasted_iota(jnp.int32, (1, npw, 1), 1)).astype(F32) * float(q)
    wmag = jnp.exp(nn * sa_re)
    sel = _lane_sel((gb, p, gb * p), lambda g, r: g * p + r)
    for k, w in enumerate((wmag * jnp.cos(nn * sa_im), wmag * jnp.sin(nn * sa_im))):
        pw_ref[k] = sum(jnp.sum(_place(part, sel), axis=0) for part in _split3(w))


def _s5_prep(log_step, a_re, a_im, b_re, b_im, c_re, c_im):
    g, p = a_re.shape
    ch, q = S5_GROUP_CH, S5_CHUNK
    gb = S5_OCT
    arow = jnp.stack([a_re, a_im], axis=1)
    acol = arow[..., None]
    bt = jnp.stack([b_re, b_im], axis=1).transpose(0, 1, 3, 2)
    crep = jnp.tile(jnp.stack([c_re, c_im], axis=1).transpose(0, 1, 3, 2), (1, 1, 1, q))
    blk = lambda *shape: pl.BlockSpec((gb,) + shape, lambda i: (i,) + (0,) * len(shape))
    out = lambda *shape: pl.BlockSpec((None,) + shape, lambda i: (i,) + (0,) * len(shape))
    nblk, lanes, ns2 = g // gb, q * LANES, 2 * gb * p
    return pl.pallas_call(
        _s5_prep_body,
        out_shape=(jax.ShapeDtypeStruct((nblk, q, LANES, LANES), BF16), jax.ShapeDtypeStruct((nblk, lanes, ns2), BF16),
                   jax.ShapeDtypeStruct((nblk, ns2, lanes), BF16), jax.ShapeDtypeStruct((nblk, 2, S5_NPOW, gb * p), F32)),
        grid=(nblk,),
        in_specs=[blk(1, 1), blk(2, p), blk(2, p, 1), blk(2, ch, p), blk(2, p, q * ch)],
        out_specs=(out(q, LANES, LANES), out(lanes, ns2), out(ns2, lanes), out(2, S5_NPOW, gb * p)),
        compiler_params=_tile_params(),
        name="s5_prep",
    )(log_step.reshape(g, 1, 1), arow, acol, bt, crep)


def _shift_rows(x, sh):
    if sh % SUBLANES == 0:
        return jnp.concatenate([jnp.zeros((sh, x.shape[1]), x.dtype), x[:x.shape[0] - sh]], axis=0)
    row = lax.broadcasted_iota(jnp.int32, x.shape, 0)
    return jnp.where(row >= sh, pltpu.roll(x, sh, 0), 0.0)


def _s5_core_body(x_ref, kd_ref, bst_ref, cst_ref, pw_ref, o_ref, toep_scr):
    rb, lanes = o_ref.shape
    q = S5_CHUNK
    ns = bst_ref.shape[1] // 2

    @pl.when(pl.program_id(1) == 0)
    def _():
        toep_scr[...] = jnp.zeros_like(toep_scr)
        for s in range(q):
            for t in range(s, q):
                toep_scr[s * LANES:(s + 1) * LANES, t * LANES:(t + 1) * LANES] = kd_ref[t - s]

    z = x_ref[...]
    ys = []
    for tp in range(0, q, 2):
        k = (tp + 2) * LANES
        ys.append(jnp.dot(z[:, :k], toep_scr[:k, tp * LANES:(tp + 2) * LANES], preferred_element_type=F32))
    g = jnp.dot(z, bst_ref[...], preferred_element_type=F32)
    e_re, e_im = g[:, :ns], g[:, ns:]
    sh, m = 1, 0
    while sh < rb:
        sr, si = _shift_rows(e_re, sh), _shift_rows(e_im, sh)
        pr, pi = _cmul(pw_ref[0, m:m + 1, :], pw_ref[1, m:m + 1, :], sr, si)
        e_re, e_im = e_re + pr, e_im + pi
        sh, m = sh * 2, m + 1
    s_re = _shift_rows(e_re, 1).astype(BF16)
    s_im = _shift_rows(e_im, 1).astype(BF16)
    o_ref[...] = (jnp.concatenate(ys, axis=1) + jnp.dot(s_re, cst_ref[:ns, :], preferred_element_type=F32)
                  + jnp.dot(s_im, cst_ref[ns:, :], preferred_element_type=F32))


def _s5_core(x, log_step, a_re, a_im, b_re, b_im, c_re, c_im, *, bsz):
    nblk, nrows, lanes = x.shape
    ch, q, p = S5_GROUP_CH, S5_CHUNK, S5_STATE
    o8 = S5_OCT
    rb = nrows // bsz
    kd, bst_o, cst_o, pw_o = _s5_prep(log_step, a_re, a_im, b_re, b_im, c_re, c_im)
    wspec = lambda *shape: pl.BlockSpec((None,) + shape, lambda o, b: (o,) + (0,) * len(shape),
                                        pipeline_mode=pl.Buffered(1))
    y = pl.pallas_call(
        _s5_core_body,
        out_shape=jax.ShapeDtypeStruct((nblk, nrows, lanes), F32),
        grid=(nblk, bsz),
        in_specs=[pl.BlockSpec((None, rb, lanes), lambda o, b: (o, b, 0)),
                  wspec(q, LANES, LANES), wspec(lanes, 2 * o8 * p), wspec(2 * o8 * p, lanes),
                  wspec(2, S5_NPOW, o8 * p)],
        out_specs=pl.BlockSpec((None, rb, lanes), lambda o, b: (o, b, 0)),
        scratch_shapes=[pltpu.VMEM((lanes, lanes), BF16)],
        compiler_params=_seq_params(),
        name="s5_core",
    )(x, kd, bst_o, cst_o, pw_o)
    return y


def _causal_conv(x, tail_scr, cw_ref, first):
    l = x.shape[0]
    k = cw_ref.shape[0]

    @pl.when(first)
    def _():
        tail_scr[...] = jnp.zeros_like(tail_scr)

    xpad = jnp.concatenate([tail_scr[...], x], axis=0)
    tail_scr[...] = x[l - SUBLANES:, :]
    acc = cw_ref[k - 1:k, :] * x
    for j in range(k - 1):
        acc = acc + cw_ref[j:j + 1, :] * pltpu.roll(xpad, k - 1 - j, 0)[SUBLANES:]
    return acc


def _split_dot(tri, x):
    hi = x.astype(BF16)
    lo = (x - hi.astype(F32)).astype(BF16)
    return jnp.dot(tri, hi, preferred_element_type=F32) + jnp.dot(tri, lo, preferred_element_type=F32)


def _lower_tri(q):
    return lax.broadcasted_iota(jnp.int32, (q, q), 0) >= lax.broadcasted_iota(jnp.int32, (q, q), 1)


def _decay_matrix(cs, cst, lane, causal):
    diff = cs[:, lane:lane + 1] - cst[lane:lane + 1, :]
    return jnp.where(causal, jnp.exp(jnp.where(causal, diff, 0.0)), 0.0)


def _ssd_body(h_ref, g_ref, wz_ref, wx_ref, wdt_ref, cw_ref, cb_ref, dtb_ref, alog_ref, dsk_ref, ng_ref, o_ref,
              tail_scr, state_scr):
    l, w = o_ref.shape
    ngroups, nstate, gw = state_scr.shape
    q = M2_CHUNK
    first = pl.program_id(1) == 0

    @pl.when(first)
    def _():
        state_scr[...] = jnp.zeros_like(state_scr)

    u = _rms(h_ref[...], g_ref[...]).astype(BF16)
    z = jnp.dot(u, wz_ref[...], preferred_element_type=F32)
    xbc_raw = jnp.dot(u, wx_ref[...], preferred_element_type=F32)
    dt_raw = jnp.dot(u, wdt_ref[...], preferred_element_type=F32)
    xbc = _silu(_causal_conv(xbc_raw, tail_scr, cw_ref, first) + cb_ref[...])
    xs = xbc[:, :w]
    bm = xbc[:, w:w + ngroups * nstate]
    cm = xbc[:, w + ngroups * nstate:]
    dt = jax.nn.softplus(dt_raw + dtb_ref[...])
    da = dt * (-jnp.exp(alog_ref[...]))
    xdt = xs * dt

    causal = _lower_tri(q)
    tri = causal.astype(BF16)
    head_of_lane = lax.broadcasted_iota(jnp.int32, (1, gw), 1) // M2_HEAD_DIM

    states = [state_scr[g] for g in range(ngroups)]
    for c in range(l // q):
        rows = slice(c * q, (c + 1) * q)
        cs = _split_dot(tri, da[rows])
        cs_last = cs[q - 1:q, :]
        dte = jnp.exp(cs_last - cs)
        ecs = jnp.exp(cs)
        cdec = jnp.exp(cs_last)
        cst = cs.T
        ychunk = []
        for g in range(ngroups):
            gl = slice(g * gw, (g + 1) * gw)
            bg = bm[rows, g * nstate:(g + 1) * nstate]
            cg = cm[rows, g * nstate:(g + 1) * nstate]
            cb = _mm_nt(cg, bg)
            xg = xdt[rows, gl]
            yg = _mm(cg, states[g]) * ecs[:, gl]
            for hh in range(gw // M2_HEAD_DIM):
                dec = _decay_matrix(cs, cst, g * gw + hh * M2_HEAD_DIM, causal)
                yg = yg + _mm(cb * dec, jnp.where(head_of_lane == hh, xg, 0.0))
            states[g] = states[g] * cdec[:, gl] + _mm(bg.T, xg * dte[:, gl])
            ychunk.append(yg)
        y = jnp.concatenate(ychunk, axis=1) + dsk_ref[...] * xs[rows]
        y = y * _silu(z[rows])
        o_ref[rows, :] = _rms(y, ng_ref[...]).astype(BF16)
    for g in range(ngroups):
        state_scr[g] = states[g]


def _ssd_mixer(h, norm_g, w_z, w_xbc, w_dt, conv_w, conv_b, dt_bias, a_log, d_skip, out_norm_g, *, bsz, tl):
    w = w_z.shape[1]
    cdim = w_xbc.shape[1]
    rep = lambda v: jnp.repeat(v, M2_HEAD_DIM).reshape(1, w)
    consts = (_row(norm_g), w_z, w_xbc, jnp.repeat(w_dt, M2_HEAD_DIM, axis=1), conv_w, _row(conv_b), rep(dt_bias),
              rep(a_log), rep(d_skip), _row(out_norm_g))
    scratch = [pltpu.VMEM((SUBLANES, cdim), F32),
               pltpu.VMEM((M2_GROUPS, M2_STATE, w // M2_GROUPS), F32)]
    return _mixer_call(_ssd_body, "ssd", h, consts, scratch, bsz=bsz, tl=tl, width=w)


def _l2norm(x):
    return x * lax.rsqrt(jnp.sum(x * x, axis=-1, keepdims=True) + EPS)


def _gdn_body(h_ref, g_ref, wqkv_ref, wg_ref, wb_ref, wa_ref, cw_ref, dtb_ref, alog_ref, ng_ref, o_ref,
              tail_scr, state_scr):
    l, w = o_ref.shape
    nheads, dk, dv = state_scr.shape
    q = GDN_CHUNK
    sq = nheads * q
    nchunks = l // q
    first = pl.program_id(1) == 0

    @pl.when(first)
    def _():
        state_scr[...] = jnp.zeros_like(state_scr)

    u = _rms(h_ref[...], g_ref[...]).astype(BF16)
    qkv_raw = jnp.dot(u, wqkv_ref[...], preferred_element_type=F32)
    gate = jnp.dot(u, wg_ref[...], preferred_element_type=F32)
    beta = jax.nn.sigmoid(jnp.dot(u, wb_ref[...], preferred_element_type=F32))
    a_raw = jnp.dot(u, wa_ref[...], preferred_element_type=F32)
    glog = -jnp.exp(alog_ref[...]) * jax.nn.softplus(a_raw + dtb_ref[...])
    qkv = _silu(_causal_conv(qkv_raw, tail_scr, cw_ref, first))

    tri = _lower_tri(q).astype(BF16)
    gcs_c = [_split_dot(tri, glog[c * q:(c + 1) * q]) for c in range(nchunks)]
    gcs = jnp.concatenate(gcs_c, axis=0)
    gtot = jnp.concatenate([jnp.broadcast_to(g_[q - 1:q, :], (q, w)) for g_ in gcs_c], axis=0)
    egcs = jnp.exp(gcs)

    def stack(x, c):
        return jnp.concatenate([x[c * q:(c + 1) * q, hd * dk:(hd + 1) * dk] for hd in range(nheads)], axis=0)

    qn = jnp.concatenate([_l2norm(qkv[:, hd * dk:(hd + 1) * dk]) * (dk ** -0.5) for hd in range(nheads)], axis=1)
    kn = jnp.concatenate([_l2norm(qkv[:, w + hd * dk:w + (hd + 1) * dk]) for hd in range(nheads)], axis=1)
    kb = kn * beta
    qd = qn * egcs
    kd = kn * jnp.exp(gtot - gcs)
    kbe = kb * egcs
    vb = qkv[:, 2 * w:] * beta

    rs = lax.broadcasted_iota(jnp.int32, (sq, sq), 0)
    cs = lax.broadcasted_iota(jnp.int32, (sq, sq), 1)
    same_head = (rs // q) == (cs // q)
    causal = same_head & (rs >= cs)
    strict = same_head & (rs > cs)
    eye = (rs == cs).astype(F32)
    head_block = (lax.broadcasted_iota(jnp.int32, (sq, nheads * dv), 0) // q
                  == lax.broadcasted_iota(jnp.int32, (sq, nheads * dv), 1) // dv)

    states = [state_scr[hd] for hd in range(nheads)]
    for c0 in range(0, nchunks, GDN_INTERLEAVE):
        cg = list(range(c0, min(c0 + GDN_INTERLEAVE, nchunks)))
        k_s = [stack(kn, c) for c in cg]
        g_s = [stack(gcs, c) for c in cg]
        g_t = jnp.concatenate(g_s, axis=0).T
        decay = []
        for i in range(len(cg)):
            diff = jnp.concatenate([g_s[i]] * (sq // dk), axis=1) - g_t[0:1, i * sq:(i + 1) * sq]
            decay.append(jnp.where(causal, jnp.exp(jnp.where(causal, diff, 0.0)), 0.0))
        kk = [_mm_nt(stack(kb, c), k_s[i]) for i, c in enumerate(cg)]
        qk = [_mm_nt(stack(qn, c), k_s[i]) * decay[i] for i, c in enumerate(cg)]
        pw = [jnp.where(strict, -(kk[i] * decay[i]), 0.0) for i in range(len(cg))]
        inv = [eye + p_ for p_ in pw]
        span = 2
        while span < q:
            pw = [_mm(p_, p_) for p_ in pw]
            inv = [a_ + _mm(a_, p_) for a_, p_ in zip(inv, pw)]
            span *= 2
        sol = [_mm(inv[i], jnp.concatenate([stack(vb, c), stack(kbe, c)], axis=1)) for i, c in enumerate(cg)]
        kd_t = jnp.concatenate([stack(kd, c) for c in cg], axis=0).T

        for i, c in enumerate(cg):
            rows = slice(c * q, (c + 1) * q)
            qd_s = stack(qd, c)
            r = [_mm(jnp.concatenate([sol[i][hd * q:(hd + 1) * q, dv:], qd_s[hd * q:(hd + 1) * q]], axis=0), states[hd])
                 for hd in range(nheads)]
            v_new = jnp.concatenate([sol[i][hd * q:(hd + 1) * q, :dv] - r[hd][:q] for hd in range(nheads)], axis=0)
            o_s = jnp.concatenate([r[hd][q:] for hd in range(nheads)], axis=0) + _mm(qk[i], v_new)
            v_bd = jnp.where(head_block, jnp.concatenate([v_new] * nheads, axis=1), 0.0)
            upd = _mm(kd_t[:, i * sq:(i + 1) * sq], v_bd)
            cdec = jnp.exp(gtot[c * q:c * q + 1, :])
            for hd in range(nheads):
                hl = slice(hd * dk, (hd + 1) * dk)
                states[hd] = states[hd] * cdec[:, hl] + upd[:, hl]
                o_ref[rows, hl] = (_rms(o_s[hd * q:(hd + 1) * q], ng_ref[...]) * _silu(gate[rows, hl])).astype(BF16)
    for hd in range(nheads):
        state_scr[hd] = states[hd]


def _gdn_mixer(h, norm_g, w_qkv, w_g, w_beta, w_a, conv_w, dt_bias, a_log, out_norm_g, *, bsz, tl):
    w = w_g.shape[1]
    cdim = w_qkv.shape[1]
    rep = lambda v: jnp.repeat(v, GDN_HEAD_DIM).reshape(1, w)
    consts = (_row(norm_g), w_qkv, w_g, jnp.repeat(w_beta, GDN_HEAD_DIM, axis=1),
              jnp.repeat(w_a, GDN_HEAD_DIM, axis=1), conv_w, rep(dt_bias), rep(a_log), _row(out_norm_g))
    scratch = [pltpu.VMEM((SUBLANES, cdim), F32),
               pltpu.VMEM((w // GDN_HEAD_DIM, GDN_HEAD_DIM, GDN_HEAD_DIM), F32)]
    return _mixer_call(_gdn_body, "gdn", h, consts, scratch, bsz=bsz, tl=tl, width=w)


def _col_slices(sizes):
    out, start = [], 0
    for size in sizes:
        out.append(slice(start, start + size))
        start += size
    return out


def kernel(x, p, ffn1_norm, ffn1_w_in, ffn1_w_out, mix_norm, w_in, w_gate, b_gate, s5_log_step, s5_a_re, s5_a_im, s5_b_re, s5_b_im, s5_c_re, s5_c_im, s5_d, s5_w_glu, s5_b_glu, lru_conv_w, lru_conv_b, lru_w_r, lru_b_r, lru_w_i, lru_b_i, lru_lambda, m2_conv_w, m2_conv_b, m2_dt_bias, m2_a_log, m2_d, m2_norm, gdn_conv_w, gdn_dt_bias, gdn_a_log, gdn_norm, w_branch, w_out, ffn2_norm, ffn2_w_in, ffn2_w_out, ple_norm, ple_w_gate, ple_w_proj, final_norm):
    bsz, s, d = x.shape
    depth = p.shape[0]
    t = bsz * s
    h = x.reshape(t, d)
    p = p.reshape(depth, t, -1)
    tm, tl = 512, 512
    bf = lambda a: a.astype(BF16)
    ffn1_w_in, ffn1_w_out, ffn2_w_in, ffn2_w_out = bf(ffn1_w_in), bf(ffn1_w_out), bf(ffn2_w_in), bf(ffn2_w_out)
    w_gate, w_branch, w_out, s5_w_glu = bf(w_gate), bf(w_branch), bf(w_out), bf(s5_w_glu)
    ple_w_gate, ple_w_proj = bf(ple_w_gate), bf(ple_w_proj)
    for i in range(depth):
        h = _ffn(h, ffn1_norm[i], ffn1_w_in, ffn1_w_out, layer=i, tm=tm)
        (w_s5, w_lx, w_lg, w_mz, w_mx, w_mdt, w_gqkv, w_gg, w_gb, w_ga) = [bf(w_in[i][:, c]) for c in _col_slices(IN_SPLITS)]
        y_b, su, su_folded = _lru_mixer(h, mix_norm[i], w_lx, w_lg, w_s5, lru_conv_w[i], lru_conv_b[i], lru_w_r[i],
                                        lru_b_r[i], lru_w_i[i], lru_b_i[i], lru_lambda[i], bsz=bsz, tl=tl)
        y_s5 = _s5_core(su_folded, s5_log_step[i], s5_a_re[i], s5_a_im[i], s5_b_re[i], s5_b_im[i], s5_c_re[i],
                        s5_c_im[i], bsz=bsz)
        y_c = _ssd_mixer(h, mix_norm[i], w_mz, w_mx, w_mdt, m2_conv_w[i], m2_conv_b[i], m2_dt_bias[i],
                         m2_a_log[i], m2_d[i], m2_norm[i], bsz=bsz, tl=tl)
        y_d = _gdn_mixer(h, mix_norm[i], w_gqkv, w_gg, w_gb, w_ga, gdn_conv_w[i], gdn_dt_bias[i], gdn_a_log[i],
                         gdn_norm[i], bsz=bsz, tl=tl)
        h = _merge(h, mix_norm[i], y_s5, su, s5_d[i], s5_w_glu, s5_b_glu[i], (y_b, y_c, y_d), w_gate, b_gate[i],
                   w_branch, w_out, layer=i, tm=tm)
        h = _ffn(h, ffn2_norm[i], ffn2_w_in, ffn2_w_out, layer=i, tm=tm,
                 ple=(ple_norm[i], p, ple_w_gate, ple_w_proj, final_norm), final_norm=(i == depth - 1))
    return h.reshape(bsz, s, d)
```

```python
import functools

import jax
import jax.numpy as jnp
from jax import lax
from jax.experimental import pallas as pl
from jax.experimental.pallas import tpu as pltpu

EPS = 1e-6
BF16 = jnp.bfloat16
F32 = jnp.float32

V7X_VMEM_BYTES = 64 * 1024 * 1024
VMEM_LIMIT_BYTES = V7X_VMEM_BYTES - 8 * 1024 * 1024
SUBLANES = 8
LANES = 128

BRANCH_WIDTH = 512
S5_GROUP_CH = 16
S5_STATE = 64
S5_CHUNK = 16
S5_OCT = LANES // S5_GROUP_CH
S5_NPOW = 16
LRU_C = 8.0
M2_HEAD_DIM = 64
M2_HEADS = BRANCH_WIDTH // M2_HEAD_DIM
M2_GROUPS = 2
M2_STATE = 128
M2_CONV_DIM = BRANCH_WIDTH + 2 * M2_GROUPS * M2_STATE
M2_CHUNK = 128
GDN_HEAD_DIM = 128
GDN_HEADS = BRANCH_WIDTH // GDN_HEAD_DIM
GDN_CHUNK = 64
GDN_INTERLEAVE = 4
MXU_WIDTH = 256
IN_SPLITS = (BRANCH_WIDTH, BRANCH_WIDTH, BRANCH_WIDTH, BRANCH_WIDTH, M2_CONV_DIM, M2_HEADS, 3 * BRANCH_WIDTH, BRANCH_WIDTH, GDN_HEADS, GDN_HEADS)


def _rms(x, g):
    return x * lax.rsqrt(jnp.mean(x * x, axis=-1, keepdims=True) + EPS) * g


def _silu(x):
    return x * jax.nn.sigmoid(x)


def _mm(a, b):
    return jnp.dot(a.astype(BF16), b.astype(BF16), preferred_element_type=F32)


def _mm_nt(a, b):
    return lax.dot_general(a.astype(BF16), b.astype(BF16), (((1,), (1,)), ((), ())), preferred_element_type=F32)


def _const_spec(shape):
    return pl.BlockSpec(shape, lambda *_: (0,) * len(shape))


def _row(v):
    return v.reshape(1, -1)


def _seq_params():
    return pltpu.CompilerParams(dimension_semantics=("parallel", "arbitrary"), vmem_limit_bytes=VMEM_LIMIT_BYTES)


def _ffn_body(h_ref, g_ref, wi_ref, wo_ref, *rest, ple, final_norm):
    o_ref = rest[-1]
    f = wo_ref.shape[0]
    h = h_ref[...]
    xn = _rms(h, g_ref[...]).astype(BF16)
    acts = []
    for s in range(0, f, MXU_WIDTH):
        gate = jnp.dot(xn, wi_ref[:, s:s + MXU_WIDTH], preferred_element_type=F32)
        up = jnp.dot(xn, wi_ref[:, f + s:f + s + MXU_WIDTH], preferred_element_type=F32)
        acts.append((_silu(gate) * up).astype(BF16))
    out = h + 0.5 * jnp.dot(jnp.concatenate(acts, axis=1), wo_ref[...], preferred_element_type=F32)
    if ple:
        pg_ref, p_ref, wg_ref, wp_ref, fg_ref = rest[:-1]
        out = out + jax.nn.sigmoid(_mm(_rms(out, pg_ref[...]), wg_ref[...])) * _mm(p_ref[...], wp_ref[...])
        if final_norm:
            out = _rms(out, fg_ref[...])
    o_ref[...] = out


def _layer_spec(stacked, layer):
    shape = stacked.shape[1:]
    return pl.BlockSpec((None,) + shape, lambda *_: (layer,) + (0,) * len(shape), pipeline_mode=pl.Buffered(1))


def _tile_params():
    return pltpu.CompilerParams(dimension_semantics=("parallel",), vmem_limit_bytes=VMEM_LIMIT_BYTES)


def _ffn(h, norm_g, w_in, w_out, *, layer, tm, ple=None, final_norm=False):
    t, d = h.shape
    in_specs = [pl.BlockSpec((tm, d), lambda i: (i, 0)), _const_spec((1, d)),
                _layer_spec(w_in, layer), _layer_spec(w_out, layer)]
    args = [h, _row(norm_g), w_in, w_out]
    if ple is not None:
        pg, p, wg, wp, fg = ple
        in_specs += [_const_spec((1, d)), pl.BlockSpec((None, tm, p.shape[-1]), lambda i: (layer, i, 0)),
                     _layer_spec(wg, layer), _layer_spec(wp, layer), _const_spec((1, d))]
        args += [_row(pg), p, wg, wp, _row(fg)]
    return pl.pallas_call(
        functools.partial(_ffn_body, ple=ple is not None, final_norm=final_norm),
        out_shape=jax.ShapeDtypeStruct((t, d), F32),
        grid=(t // tm,),
        in_specs=in_specs,
        out_specs=pl.BlockSpec((tm, d), lambda i: (i, 0)),
        compiler_params=_tile_params(),
        name="ffn",
    )(*args)


def _merge_body(h_ref, g_ref, ys5_ref, su_ref, dsk_ref, wglu_ref, bglu_ref, yb_ref, yc_ref, yd_ref, wg_ref, bg_ref,
                wb_ref, wo_ref, o_ref, stage_scr):
    h = h_ref[...]
    d = h.shape[-1]
    u = _rms(h, g_ref[...]).astype(BF16)
    z = jax.nn.gelu(_unfold_chunks(ys5_ref, stage_scr) + dsk_ref[...] * su_ref[...])
    ya = (z * jax.nn.sigmoid(_mm(z, wglu_ref[...]) + bglu_ref[...])).astype(BF16)
    mixed = jnp.zeros(h.shape, F32)
    for n, y in enumerate((ya, yb_ref[...], yc_ref[...], yd_ref[...])):
        gate = jax.nn.sigmoid(
            jnp.dot(u, wg_ref[:, n * d:(n + 1) * d], preferred_element_type=F32) + bg_ref[:, n * d:(n + 1) * d])
        mixed = mixed + gate * jnp.dot(y, wb_ref[n], preferred_element_type=F32)
    o_ref[...] = h + _mm(mixed, wo_ref[...])


def _merge(h, norm_g, y_s5, su, d_skip, w_glu, b_glu, ys, w_gate, b_gate, w_branch, w_out, *, layer, tm):
    t, d = h.shape
    nblk, _, fold = y_s5.shape
    w = nblk * LANES
    y_spec = pl.BlockSpec((tm, w), lambda i: (i, 0))
    return pl.pallas_call(
        _merge_body,
        out_shape=jax.ShapeDtypeStruct((t, d), F32),
        grid=(t // tm,),
        in_specs=[
            pl.BlockSpec((tm, d), lambda i: (i, 0)), _const_spec((1, d)),
            pl.BlockSpec((nblk, tm // S5_CHUNK, fold), lambda i: (0, i, 0)), y_spec,
            _const_spec((1, w)), _layer_spec(w_glu, layer), _const_spec((1, w)),
            y_spec, y_spec, y_spec,
            _layer_spec(w_gate, layer), _const_spec((1, b_gate.shape[-1])),
            _layer_spec(w_branch, layer), _layer_spec(w_out, layer),
        ],
        out_specs=pl.BlockSpec((tm, d), lambda i: (i, 0)),
        scratch_shapes=[pltpu.VMEM((nblk, tm, LANES), F32)],
        compiler_params=_tile_params(),
        name="merge",
    )(h, _row(norm_g), y_s5, su, _row(d_skip), w_glu, _row(b_glu), *ys, w_gate, _row(b_gate), w_branch, w_out)


def _segment_perm(l):
    r = jnp.arange(l)
    src = (r % SUBLANES) * (l // SUBLANES) + r // SUBLANES
    return (src[:, None] == jnp.arange(l)[None, :]).astype(BF16)


def _mixer_call(body, name, h, consts, scratch_shapes, *, bsz, tl, width, s5_input_outs=False):
    t, d = h.shape
    nt = t // bsz // tl
    out_shape = [jax.ShapeDtypeStruct((t, width), BF16)]
    out_specs = [pl.BlockSpec((tl, width), lambda b, i: (b * nt + i, 0))]
    if s5_input_outs:
        nblk, fold = width // LANES, S5_CHUNK * LANES
        out_shape += [jax.ShapeDtypeStruct((t, width), F32), jax.ShapeDtypeStruct((nblk, t // S5_CHUNK, fold), BF16)]
        out_specs += [pl.BlockSpec((tl, width), lambda b, i: (b * nt + i, 0)),
                      pl.BlockSpec((nblk, tl // S5_CHUNK, fold), lambda b, i: (0, b * nt + i, 0))]
    outs = pl.pallas_call(
        body,
        out_shape=tuple(out_shape),
        grid=(bsz, nt),
        in_specs=[pl.BlockSpec((tl, d), lambda b, i: (b * nt + i, 0))] + [_const_spec(c.shape) for c in consts],
        out_specs=tuple(out_specs),
        scratch_shapes=scratch_shapes,
        compiler_params=_seq_params(),
        name=name,
    )(h, *consts)
    return outs if s5_input_outs else outs[0]


def _fold_chunks(x, stage_scr, out_ref):
    rows = x.shape[0]
    for j in range(stage_scr.shape[0]):
        stage_scr[j] = x[:, j * LANES:(j + 1) * LANES]
    for j in range(stage_scr.shape[0]):
        for s in range(S5_CHUNK):
            out_ref[j, :, s * LANES:(s + 1) * LANES] = (
                stage_scr[j, pl.ds(s, rows // S5_CHUNK, stride=S5_CHUNK), :].astype(out_ref.dtype))


def _unfold_chunks(in_ref, stage_scr):
    rows = stage_scr.shape[1]
    for j in range(stage_scr.shape[0]):
        for s in range(S5_CHUNK):
            stage_scr[j, pl.ds(s, rows // S5_CHUNK, stride=S5_CHUNK), :] = in_ref[j, :, s * LANES:(s + 1) * LANES]
    return jnp.concatenate([stage_scr[j] for j in range(stage_scr.shape[0])], axis=1)


def _conv_halo(cur_tail, prev_tail):
    row = lax.broadcasted_iota(jnp.int32, cur_tail.shape, 0) % SUBLANES
    slab = jnp.where(row == SUBLANES - 1, prev_tail, cur_tail)
    n = cur_tail.shape[0] // SUBLANES
    return jnp.concatenate(
        [pltpu.roll(slab[g * SUBLANES:(g + 1) * SUBLANES], 1, 0) for g in range(n)], axis=0)


def _lru_body(h_ref, g_ref, p_ref, pt_ref, wx_ref, wg_ref, ws5_ref, cw_ref, cb_ref, wr_ref, br_ref, wi_ref, bi_ref,
              lam_ref, o_ref, su_ref, suf_ref, a_scr, b_scr, state_scr, tail_scr, stage_scr):
    l, w = a_scr.shape
    ls = l // SUBLANES
    kconv = cw_ref.shape[0]
    halo = (kconv - 1) * SUBLANES

    @pl.when(pl.program_id(1) == 0)
    def _():
        state_scr[...] = jnp.zeros_like(state_scr)
        tail_scr[...] = jnp.zeros_like(tail_scr)

    u = _rms(h_ref[...], g_ref[...]).astype(BF16)
    su = jnp.dot(u, ws5_ref[...], preferred_element_type=F32)
    su_ref[...] = su
    _fold_chunks(su, stage_scr, suf_ref)
    up = jnp.dot(p_ref[...], u, preferred_element_type=F32).astype(BF16)
    x = jnp.dot(up, wx_ref[...], preferred_element_type=F32)
    gate = jnp.dot(up, wg_ref[...], preferred_element_type=F32)

    cur_tail = x[l - halo:, :]
    xpad = jnp.concatenate([_conv_halo(cur_tail, tail_scr[...]), x], axis=0)
    tail_scr[...] = cur_tail
    xc = cb_ref[...] + sum(cw_ref[k:k + 1, :] * xpad[k * SUBLANES:k * SUBLANES + l] for k in range(kconv))

    r = jax.nn.sigmoid(_mm(xc, wr_ref[...]) + br_ref[...])
    ig = jax.nn.sigmoid(_mm(xc, wi_ref[...]) + bi_ref[...])
    log_a = (-LRU_C * jax.nn.softplus(-lam_ref[...])) * r
    a_scr[...] = jnp.exp(log_a)
    b_scr[...] = jnp.sqrt(-jnp.tanh(log_a) * (jnp.exp(2.0 * log_a) + 1.0)) * (ig * xc)

    def step(i, carry):
        hh, pp = carry
        rows = pl.ds(pl.multiple_of(i * SUBLANES, SUBLANES), SUBLANES)
        a_i = a_scr[rows, :]
        hh = a_i * hh + b_scr[rows, :]
        pp = a_i * pp
        b_scr[rows, :] = hh
        a_scr[rows, :] = pp
        return hh, pp

    h_end, p_end = lax.fori_loop(0, ls, step, (jnp.zeros((SUBLANES, w), F32), jnp.ones((SUBLANES, w), F32)),
                                 unroll=4)
    c = state_scr[...]
    carries = []
    for j in range(SUBLANES):
        carries.append(c)
        c = h_end[j:j + 1, :] + p_end[j:j + 1, :] * c
    state_scr[...] = c
    cmat = jnp.concatenate(carries, axis=0)
    hfull = b_scr[...].reshape(ls, SUBLANES, w) + a_scr[...].reshape(ls, SUBLANES, w) * cmat[None]
    out = (hfull.reshape(l, w) * jax.nn.gelu(gate)).astype(BF16)
    o_ref[...] = jnp.dot(pt_ref[...], out, preferred_element_type=F32).astype(BF16)


def _lru_mixer(h, norm_g, w_x, w_g, w_s5, conv_w, conv_b, w_r, b_r, w_i, b_i, lam, *, bsz, tl):
    w = w_x.shape[1]
    perm = _segment_perm(tl)
    blockdiag = lambda m: jax.scipy.linalg.block_diag(*m).astype(BF16)
    consts = (_row(norm_g), perm, perm.T, w_x, w_g, w_s5, conv_w, _row(conv_b), blockdiag(w_r), _row(b_r),
              blockdiag(w_i), _row(b_i), _row(lam))
    scratch = [pltpu.VMEM((tl, w), F32), pltpu.VMEM((tl, w), F32), pltpu.VMEM((1, w), F32),
               pltpu.VMEM(((conv_w.shape[0] - 1) * SUBLANES, w), F32), pltpu.VMEM((w // LANES, tl, LANES), F32)]
    return _mixer_call(_lru_body, "rglru", h, consts, scratch, bsz=bsz, tl=tl, width=w, s5_input_outs=True)


def _cmul(ar, ai, br, bi):
    return ar * br - ai * bi, ar * bi + ai * br


def _split3(x):
    hi = x.astype(BF16)
    r1 = x - hi.astype(F32)
    mid = r1.astype(BF16)
    lo = (r1 - mid.astype(F32)).astype(BF16)
    return hi, mid, lo


def _mm_exact_rhs(x, e):
    return sum(jnp.dot(part, e, preferred_element_type=F32) for part in _split3(x))


def _bmm_f32(a, b):
    a3, b3 = _split3(a), _split3(b)
    acc = None
    for i, ap in enumerate(a3):
        for j, bp in enumerate(b3):
            if i + j <= 2:
                t = jnp.einsum('gij,gjk->gik', ap, bp, preferred_element_type=F32)
                acc = t if acc is None else acc + t
    return acc


def _place(x, sel):
    return jnp.einsum('gij,gjk->gik', x.astype(BF16), sel, preferred_element_type=F32)


def _lane_sel(shape, target):
    g, r, n = (lax.broadcasted_iota(jnp.int32, shape, k) for k in range(3))
    return (n == target(g, r)).astype(BF16)


def _s5_prep_body(ls_ref, arow_ref, acol_ref, bt_ref, crep_ref, kd_ref, bst_ref, cst_ref, pw_ref):
    gb, _, p = arow_ref.shape
    ch = S5_GROUP_CH
    q = S5_CHUNK
    npw = pw_ref.shape[1]
    step = jnp.exp(ls_ref[...])
    a_re, a_im = arow_ref[:, 0:1, :], arow_ref[:, 1:2, :]
    sa_re, sa_im = a_re * step, a_im * step
    mag = jnp.exp(sa_re)
    ab_re, ab_im = mag * jnp.cos(sa_im), mag * jnp.sin(sa_im)
    den = a_re * a_re + a_im * a_im
    num_re = ab_re - 1.0
    f_re = (num_re * a_re + ab_im * a_im) / den
    f_im = (ab_im * a_re - num_re * a_im) / den
    bt_re, bt_im = bt_ref[:, 0], bt_ref[:, 1]
    bb_re = f_re * bt_re - f_im * bt_im
    bb_im = f_re * bt_im + f_im * bt_re
    ns = (q - 1 - lax.broadcasted_iota(jnp.int32, (1, q, 1), 1)).astype(F32)
    pmag = jnp.exp(ns * sa_re)
    pb_re, pb_im = pmag * jnp.cos(ns * sa_im), pmag * jnp.sin(ns * sa_im)
    pb_re = jnp.broadcast_to(pb_re[:, :, None, :], (gb, q, ch, p)).reshape(gb, q * ch, p)
    pb_im = jnp.broadcast_to(pb_im[:, :, None, :], (gb, q, ch, p)).reshape(gb, q * ch, p)
    bbr = jnp.concatenate([bb_re] * q, axis=1)
    bbi = jnp.concatenate([bb_im] * q, axis=1)
    sr, si = _cmul(bbr, bbi, pb_re, pb_im)
    sin = (_place(sr, _lane_sel((gb, p, 2 * gb * p), lambda g, r: g * p + r))
           + _place(si, _lane_sel((gb, p, 2 * gb * p), lambda g, r: (gb + g) * p + r))).astype(BF16)
    for g in range(gb):
        for s in range(q):
            bst_ref[s * LANES + g * ch:s * LANES + (g + 1) * ch, :] = sin[g, s * ch:(s + 1) * ch, :]
    sc_re, sc_im = acol_ref[:, 0] * step, acol_ref[:, 1] * step
    nd = lax.broadcasted_iota(jnp.int32, (1, 1, q), 2).astype(F32)
    cmag = jnp.exp(sc_re * nd)
    pc_re, pc_im = cmag * jnp.cos(sc_im * nd), cmag * jnp.sin(sc_im * nd)
    expand = (lax.broadcasted_iota(jnp.int32, (q, q * ch), 0)
              == lax.broadcasted_iota(jnp.int32, (q, q * ch), 1) // ch).astype(BF16)
    pc_re = _mm_exact_rhs(pc_re.reshape(gb * p, q), expand).reshape(gb, p, q * ch)
    pc_im = _mm_exact_rhs(pc_im.reshape(gb * p, q), expand).reshape(gb, p, q * ch)
    ca_re, ca_im = _cmul(crep_ref[:, 0], crep_ref[:, 1], pc_re, pc_im)
    kt = _bmm_f32(bb_re, ca_re) - _bmm_f32(bb_im, ca_im)
    for d in range(q):
        sel = _lane_sel((gb, q * ch, LANES), lambda g, r, d=d: jnp.where(r // ch == d, g * ch + r % ch, -1))
        kd_ref[d] = _place(kt, sel).reshape(gb * ch, LANES).astype(BF16)
    abc_mag = jnp.exp(sc_re)
    o_re, o_im = _cmul(ca_re, ca_im, abc_mag * jnp.cos(sc_im), abc_mag * jnp.sin(sc_im))
    sel = _lane_sel((gb, q * ch, q * LANES), lambda g, r: (r // ch) * LANES + g * ch + r % ch)
    out_re, out_im = _place(o_re, sel).astype(BF16), _place(-o_im, sel).astype(BF16)
    for g in range(gb):
        cst_ref[g * p:(g + 1) * p, :] = out_re[g]
        cst_ref[(gb + g) * p:(gb + g + 1) * p, :] = out_im[g]
    nn = jnp.left_shift(1, lax.broadcasted_iota(jnp.int32, (1, npw, 1), 1)).astype(F32) * float(q)
    wmag = jnp.exp(nn * sa_re)
    sel = _lane_sel((gb, p, gb * p), lambda g, r: g * p + r)
    for k, w in enumerate((wmag * jnp.cos(nn * sa_im), wmag * jnp.sin(nn * sa_im))):
        pw_ref[k] = sum(jnp.sum(_place(part, sel), axis=0) for part in _split3(w))


def _s5_prep(log_step, a_re, a_im, b_re, b_im, c_re, c_im):
    g, p = a_re.shape
    ch, q = S5_GROUP_CH, S5_CHUNK
    gb = S5_OCT
    arow = jnp.stack([a_re, a_im], axis=1)
    acol = arow[..., None]
    bt = jnp.stack([b_re, b_im], axis=1).transpose(0, 1, 3, 2)
    crep = jnp.tile(jnp.stack([c_re, c_im], axis=1).transpose(0, 1, 3, 2), (1, 1, 1, q))
    blk = lambda *shape: pl.BlockSpec((gb,) + shape, lambda i: (i,) + (0,) * len(shape))
    out = lambda *shape: pl.BlockSpec((None,) + shape, lambda i: (i,) + (0,) * len(shape))
    nblk, lanes, ns2 = g // gb, q * LANES, 2 * gb * p
    return pl.pallas_call(
        _s5_prep_body,
        out_shape=(jax.ShapeDtypeStruct((nblk, q, LANES, LANES), BF16), jax.ShapeDtypeStruct((nblk, lanes, ns2), BF16),
                   jax.ShapeDtypeStruct((nblk, ns2, lanes), BF16), jax.ShapeDtypeStruct((nblk, 2, S5_NPOW, gb * p), F32)),
        grid=(nblk,),
        in_specs=[blk(1, 1), blk(2, p), blk(2, p, 1), blk(2, ch, p), blk(2, p, q * ch)],
        out_specs=(out(q, LANES, LANES), out(lanes, ns2), out(ns2, lanes), out(2, S5_NPOW, gb * p)),
        compiler_params=_tile_params(),
        name="s5_prep",
    )(log_step.reshape(g, 1, 1), arow, acol, bt, crep)


def _shift_rows(x, sh):
    if sh % SUBLANES == 0:
        return jnp.concatenate([jnp.zeros((sh, x.shape[1]), x.dtype), x[:x.shape[0] - sh]], axis=0)
    row = lax.broadcasted_iota(jnp.int32, x.shape, 0)
    return jnp.where(row >= sh, pltpu.roll(x, sh, 0), 0.0)


def _s5_core_body(x_ref, kd_ref, bst_ref, cst_ref, pw_ref, o_ref, toep_scr):
    rb, lanes = o_ref.shape
    q = S5_CHUNK
    ns = bst_ref.shape[1] // 2

    @pl.when(pl.program_id(1) == 0)
    def _():
        toep_scr[...] = jnp.zeros_like(toep_scr)
        for s in range(q):
            for t in range(s, q):
                toep_scr[s * LANES:(s + 1) * LANES, t * LANES:(t + 1) * LANES] = kd_ref[t - s]

    z = x_ref[...]
    ys = []
    for tp in range(0, q, 2):
        k = (tp + 2) * LANES
        ys.append(jnp.dot(z[:, :k], toep_scr[:k, tp * LANES:(tp + 2) * LANES], preferred_element_type=F32))
    g = jnp.dot(z, bst_ref[...], preferred_element_type=F32)
    e_re, e_im = g[:, :ns], g[:, ns:]
    sh, m = 1, 0
    while sh < rb:
        sr, si = _shift_rows(e_re, sh), _shift_rows(e_im, sh)
        pr, pi = _cmul(pw_ref[0, m:m + 1, :], pw_ref[1, m:m + 1, :], sr, si)
        e_re, e_im = e_re + pr, e_im + pi
        sh, m = sh * 2, m + 1
    s_re = _shift_rows(e_re, 1).astype(BF16)
    s_im = _shift_rows(e_im, 1).astype(BF16)
    o_ref[...] = (jnp.concatenate(ys, axis=1) + jnp.dot(s_re, cst_ref[:ns, :], preferred_element_type=F32)
                  + jnp.dot(s_im, cst_ref[ns:, :], preferred_element_type=F32))


def _s5_core(x, log_step, a_re, a_im, b_re, b_im, c_re, c_im, *, bsz):
    nblk, nrows, lanes = x.shape
    ch, q, p = S5_GROUP_CH, S5_CHUNK, S5_STATE
    o8 = S5_OCT
    rb = nrows // bsz
    kd, bst_o, cst_o, pw_o = _s5_prep(log_step, a_re, a_im, b_re, b_im, c_re, c_im)
    wspec = lambda *shape: pl.BlockSpec((None,) + shape, lambda o, b: (o,) + (0,) * len(shape),
                                        pipeline_mode=pl.Buffered(1))
    y = pl.pallas_call(
        _s5_core_body,
        out_shape=jax.ShapeDtypeStruct((nblk, nrows, lanes), F32),
        grid=(nblk, bsz),
        in_specs=[pl.BlockSpec((None, rb, lanes), lambda o, b: (o, b, 0)),
                  wspec(q, LANES, LANES), wspec(lanes, 2 * o8 * p), wspec(2 * o8 * p, lanes),
                  wspec(2, S5_NPOW, o8 * p)],
        out_specs=pl.BlockSpec((None, rb, lanes), lambda o, b: (o, b, 0)),
        scratch_shapes=[pltpu.VMEM((lanes, lanes), BF16)],
        compiler_params=_seq_params(),
        name="s5_core",
    )(x, kd, bst_o, cst_o, pw_o)
    return y


def _causal_conv(x, tail_scr, cw_ref, first):
    l = x.shape[0]
    k = cw_ref.shape[0]

    @pl.when(first)
    def _():
        tail_scr[...] = jnp.zeros_like(tail_scr)

    xpad = jnp.concatenate([tail_scr[...], x], axis=0)
    tail_scr[...] = x[l - SUBLANES:, :]
    acc = cw_ref[k - 1:k, :] * x
    for j in range(k - 1):
        acc = acc + cw_ref[j:j + 1, :] * pltpu.roll(xpad, k - 1 - j, 0)[SUBLANES:]
    return acc


def _split_dot(tri, x):
    hi = x.astype(BF16)
    lo = (x - hi.astype(F32)).astype(BF16)
    return jnp.dot(tri, hi, preferred_element_type=F32) + jnp.dot(tri, lo, preferred_element_type=F32)


def _lower_tri(q):
    return lax.broadcasted_iota(jnp.int32, (q, q), 0) >= lax.broadcasted_iota(jnp.int32, (q, q), 1)


def _decay_matrix(cs, cst, lane, causal):
    diff = cs[:, lane:lane + 1] - cst[lane:lane + 1, :]
    return jnp.where(causal, jnp.exp(jnp.where(causal, diff, 0.0)), 0.0)


def _ssd_body(h_ref, g_ref, wz_ref, wx_ref, wdt_ref, cw_ref, cb_ref, dtb_ref, alog_ref, dsk_ref, ng_ref, o_ref,
              tail_scr, state_scr):
    l, w = o_ref.shape
    ngroups, nstate, gw = state_scr.shape
    q = M2_CHUNK
    first = pl.program_id(1) == 0

    @pl.when(first)
    def _():
        state_scr[...] = jnp.zeros_like(state_scr)

    u = _rms(h_ref[...], g_ref[...]).astype(BF16)
    z = jnp.dot(u, wz_ref[...], preferred_element_type=F32)
    xbc_raw = jnp.dot(u, wx_ref[...], preferred_element_type=F32)
    dt_raw = jnp.dot(u, wdt_ref[...], preferred_element_type=F32)
    xbc = _silu(_causal_conv(xbc_raw, tail_scr, cw_ref, first) + cb_ref[...])
    xs = xbc[:, :w]
    bm = xbc[:, w:w + ngroups * nstate]
    cm = xbc[:, w + ngroups * nstate:]
    dt = jax.nn.softplus(dt_raw + dtb_ref[...])
    da = dt * (-jnp.exp(alog_ref[...]))
    xdt = xs * dt

    causal = _lower_tri(q)
    tri = causal.astype(BF16)
    head_of_lane = lax.broadcasted_iota(jnp.int32, (1, gw), 1) // M2_HEAD_DIM

    states = [state_scr[g] for g in range(ngroups)]
    for c in range(l // q):
        rows = slice(c * q, (c + 1) * q)
        cs = _split_dot(tri, da[rows])
        cs_last = cs[q - 1:q, :]
        dte = jnp.exp(cs_last - cs)
        ecs = jnp.exp(cs)
        cdec = jnp.exp(cs_last)
        cst = cs.T
        ychunk = []
        for g in range(ngroups):
            gl = slice(g * gw, (g + 1) * gw)
            bg = bm[rows, g * nstate:(g + 1) * nstate]
            cg = cm[rows, g * nstate:(g + 1) * nstate]
            cb = _mm_nt(cg, bg)
            xg = xdt[rows, gl]
            yg = _mm(cg, states[g]) * ecs[:, gl]
            for hh in range(gw // M2_HEAD_DIM):
                dec = _decay_matrix(cs, cst, g * gw + hh * M2_HEAD_DIM, causal)
                yg = yg + _mm(cb * dec, jnp.where(head_of_lane == hh, xg, 0.0))
            states[g] = states[g] * cdec[:, gl] + _mm(bg.T, xg * dte[:, gl])
            ychunk.append(yg)
        y = jnp.concatenate(ychunk, axis=1) + dsk_ref[...] * xs[rows]
        y = y * _silu(z[rows])
        o_ref[rows, :] = _rms(y, ng_ref[...]).astype(BF16)
    for g in range(ngroups):
        state_scr[g] = states[g]


def _ssd_mixer(h, norm_g, w_z, w_xbc, w_dt, conv_w, conv_b, dt_bias, a_log, d_skip, out_norm_g, *, bsz, tl):
    w = w_z.shape[1]
    cdim = w_xbc.shape[1]
    rep = lambda v: jnp.repeat(v, M2_HEAD_DIM).reshape(1, w)
    consts = (_row(norm_g), w_z, w_xbc, jnp.repeat(w_dt, M2_HEAD_DIM, axis=1), conv_w, _row(conv_b), rep(dt_bias),
              rep(a_log), rep(d_skip), _row(out_norm_g))
    scratch = [pltpu.VMEM((SUBLANES, cdim), F32),
               pltpu.VMEM((M2_GROUPS, M2_STATE, w // M2_GROUPS), F32)]
    return _mixer_call(_ssd_body, "ssd", h, consts, scratch, bsz=bsz, tl=tl, width=w)


def _l2norm(x):
    return x * lax.rsqrt(jnp.sum(x * x, axis=-1, keepdims=True) + EPS)


def _gdn_body(h_ref, g_ref, wqkv_ref, wg_ref, wb_ref, wa_ref, cw_ref, dtb_ref, alog_ref, ng_ref, o_ref,
              tail_scr, state_scr):
    nb, lt, w = o_ref.shape
    _, nheads, dk, dv = state_scr.shape
    l = nb * lt
    q = GDN_CHUNK
    sq = nheads * q
    nchunks = l // q
    cpb = lt // q
    first = pl.program_id(0) == 0

    @pl.when(first)
    def _():
        state_scr[...] = jnp.zeros_like(state_scr)

    u = _rms(h_ref[...].reshape(l, h_ref.shape[-1]), g_ref[...]).astype(BF16)
    qkv_raw = jnp.dot(u, wqkv_ref[...], preferred_element_type=F32)
    gate = jnp.dot(u, wg_ref[...], preferred_element_type=F32)
    beta = jax.nn.sigmoid(jnp.dot(u, wb_ref[...], preferred_element_type=F32))
    a_raw = jnp.dot(u, wa_ref[...], preferred_element_type=F32)
    glog = -jnp.exp(alog_ref[...]) * jax.nn.softplus(a_raw + dtb_ref[...])
    qkv = jnp.concatenate([_silu(_causal_conv(qkv_raw[b * lt:(b + 1) * lt], tail_scr.at[b], cw_ref, first))
                           for b in range(nb)], axis=0)

    tri = _lower_tri(q).astype(BF16)
    gcsc = [_split_dot(tri, glog[c * q:(c + 1) * q]) for c in range(nchunks)]
    gcs = jnp.concatenate(gcsc, axis=0)
    gtot = jnp.concatenate([jnp.broadcast_to(g_[q - 1:q, :], (q, w)) for g_ in gcsc], axis=0)
    egcs = jnp.exp(gcs)

    def stack(x, c):
        return jnp.concatenate([x[c * q:(c + 1) * q, hd * dk:(hd + 1) * dk] for hd in range(nheads)], axis=0)

    qn = jnp.concatenate([_l2norm(qkv[:, hd * dk:(hd + 1) * dk]) * (dk ** -0.5) for hd in range(nheads)], axis=1)
    kn = jnp.concatenate([_l2norm(qkv[:, w + hd * dk:w + (hd + 1) * dk]) for hd in range(nheads)], axis=1)
    kb = kn * beta
    qd = qn * egcs
    kd = kn * jnp.exp(gtot - gcs)
    kbe = kb * egcs
    vb = qkv[:, 2 * w:] * beta

    rs = lax.broadcasted_iota(jnp.int32, (sq, sq), 0)
    cs = lax.broadcasted_iota(jnp.int32, (sq, sq), 1)
    same_head = (rs // q) == (cs // q)
    causal = same_head & (rs >= cs)
    strict = same_head & (rs > cs)
    eye = (rs == cs).astype(F32)
    head_block = (lax.broadcasted_iota(jnp.int32, (sq, nheads * dv), 0) // q
                  == lax.broadcasted_iota(jnp.int32, (sq, nheads * dv), 1) // dv)

    sol, qk, kd_t = [None] * nchunks, [None] * nchunks, [None] * nchunks
    for c0 in range(0, nchunks, GDN_INTERLEAVE):
        cg = list(range(c0, min(c0 + GDN_INTERLEAVE, nchunks)))
        k_s = [stack(kn, c) for c in cg]
        g_s = [stack(gcs, c) for c in cg]
        g_t = jnp.concatenate(g_s, axis=0).T
        decay = []
        for i in range(len(cg)):
            diff = jnp.concatenate([g_s[i]] * (sq // dk), axis=1) - g_t[0:1, i * sq:(i + 1) * sq]
            decay.append(jnp.where(causal, jnp.exp(jnp.where(causal, diff, 0.0)), 0.0))
        kk = [_mm_nt(stack(kb, c), k_s[i]) for i, c in enumerate(cg)]
        for i, c in enumerate(cg):
            qk[c] = _mm_nt(stack(qn, c), k_s[i]) * decay[i]
        pw = [jnp.where(strict, -(kk[i] * decay[i]), 0.0) for i in range(len(cg))]
        inv = [eye + p_ for p_ in pw]
        span = 2
        while span < q:
            pw = [_mm(p_, p_) for p_ in pw]
            inv = [a_ + _mm(a_, p_) for a_, p_ in zip(inv, pw)]
            span *= 2
        kdt = jnp.concatenate([stack(kd, c) for c in cg], axis=0).T
        for i, c in enumerate(cg):
            sol[c] = _mm(inv[i], jnp.concatenate([stack(vb, c), stack(kbe, c)], axis=1))
            kd_t[c] = kdt[:, i * sq:(i + 1) * sq]

    states = [[state_scr[b, hd] for hd in range(nheads)] for b in range(nb)]
    bs = range(nb)
    for cl in range(cpb):
        csb = [b * cpb + cl for b in bs]
        qd_s = [stack(qd, c) for c in csb]
        r = [[_mm(jnp.concatenate([sol[c][hd * q:(hd + 1) * q, dv:], qd_s[b][hd * q:(hd + 1) * q]], axis=0),
                  states[b][hd]) for hd in range(nheads)] for b, c in zip(bs, csb)]
        v_new = [jnp.concatenate([sol[c][hd * q:(hd + 1) * q, :dv] - r[b][hd][:q] for hd in range(nheads)], axis=0)
                 for b, c in zip(bs, csb)]
        upd = [_mm(kd_t[c], jnp.where(head_block, jnp.concatenate([v_new[b]] * nheads, axis=1), 0.0))
               for b, c in zip(bs, csb)]
        o_s = [jnp.concatenate([r[b][hd][q:] for hd in range(nheads)], axis=0) + _mm(qk[c], v_new[b])
               for b, c in zip(bs, csb)]
        for b, c in zip(bs, csb):
            cdec = jnp.exp(gtot[c * q:c * q + 1, :])
            for hd in range(nheads):
                hl = slice(hd * dk, (hd + 1) * dk)
                states[b][hd] = states[b][hd] * cdec[:, hl] + upd[b][:, hl]
                o_ref[b, cl * q:(cl + 1) * q, hl] = (
                    _rms(o_s[b][hd * q:(hd + 1) * q], ng_ref[...]) * _silu(gate[c * q:(c + 1) * q, hl])).astype(BF16)
    for b in bs:
        for hd in range(nheads):
            state_scr[b, hd] = states[b][hd]


def _gdn_mixer(h, norm_g, w_qkv, w_g, w_beta, w_a, conv_w, dt_bias, a_log, out_norm_g, *, bsz, tl):
    t, d = h.shape
    s = t // bsz
    w = w_g.shape[1]
    cdim = w_qkv.shape[1]
    rep = lambda v: jnp.repeat(v, GDN_HEAD_DIM).reshape(1, w)
    consts = (_row(norm_g), w_qkv, w_g, jnp.repeat(w_beta, GDN_HEAD_DIM, axis=1),
              jnp.repeat(w_a, GDN_HEAD_DIM, axis=1), conv_w, rep(dt_bias), rep(a_log), _row(out_norm_g))
    out = pl.pallas_call(
        _gdn_body,
        out_shape=jax.ShapeDtypeStruct((bsz, s, w), BF16),
        grid=(s // tl,),
        in_specs=[pl.BlockSpec((bsz, tl, d), lambda i: (0, i, 0))] + [_const_spec(c.shape) for c in consts],
        out_specs=pl.BlockSpec((bsz, tl, w), lambda i: (0, i, 0)),
        scratch_shapes=[pltpu.VMEM((bsz, SUBLANES, cdim), F32),
                        pltpu.VMEM((bsz, w // GDN_HEAD_DIM, GDN_HEAD_DIM, GDN_HEAD_DIM), F32)],
        compiler_params=pltpu.CompilerParams(dimension_semantics=("arbitrary",), vmem_limit_bytes=VMEM_LIMIT_BYTES),
        name="gdn",
    )(h.reshape(bsz, s, d), *consts)
    return out.reshape(t, w)


def _col_slices(sizes):
    out, start = [], 0
    for size in sizes:
        out.append(slice(start, start + size))
        start += size
    return out


def kernel(x, p, ffn1_norm, ffn1_w_in, ffn1_w_out, mix_norm, w_in, w_gate, b_gate, s5_log_step, s5_a_re, s5_a_im, s5_b_re, s5_b_im, s5_c_re, s5_c_im, s5_d, s5_w_glu, s5_b_glu, lru_conv_w, lru_conv_b, lru_w_r, lru_b_r, lru_w_i, lru_b_i, lru_lambda, m2_conv_w, m2_conv_b, m2_dt_bias, m2_a_log, m2_d, m2_norm, gdn_conv_w, gdn_dt_bias, gdn_a_log, gdn_norm, w_branch, w_out, ffn2_norm, ffn2_w_in, ffn2_w_out, ple_norm, ple_w_gate, ple_w_proj, final_norm):
    bsz, s, d = x.shape
    depth = p.shape[0]
    t = bsz * s
    h = x.reshape(t, d)
    p = p.reshape(depth, t, -1)
    tm, tl = 512, 512
    bf = lambda a: a.astype(BF16)
    ffn1_w_in, ffn1_w_out, ffn2_w_in, ffn2_w_out = bf(ffn1_w_in), bf(ffn1_w_out), bf(ffn2_w_in), bf(ffn2_w_out)
    w_gate, w_branch, w_out, s5_w_glu = bf(w_gate), bf(w_branch), bf(w_out), bf(s5_w_glu)
    ple_w_gate, ple_w_proj = bf(ple_w_gate), bf(ple_w_proj)
    for i in range(depth):
        h = _ffn(h, ffn1_norm[i], ffn1_w_in, ffn1_w_out, layer=i, tm=tm)
        (w_s5, w_lx, w_lg, w_mz, w_mx, w_mdt, w_gqkv, w_gg, w_gb, w_ga) = [bf(w_in[i][:, c]) for c in _col_slices(IN_SPLITS)]
        y_b, su, su_folded = _lru_mixer(h, mix_norm[i], w_lx, w_lg, w_s5, lru_conv_w[i], lru_conv_b[i], lru_w_r[i],
                                        lru_b_r[i], lru_w_i[i], lru_b_i[i], lru_lambda[i], bsz=bsz, tl=tl)
        y_s5 = _s5_core(su_folded, s5_log_step[i], s5_a_re[i], s5_a_im[i], s5_b_re[i], s5_b_im[i], s5_c_re[i],
                        s5_c_im[i], bsz=bsz)
        y_c = _ssd_mixer(h, mix_norm[i], w_mz, w_mx, w_mdt, m2_conv_w[i], m2_conv_b[i], m2_dt_bias[i],
                         m2_a_log[i], m2_d[i], m2_norm[i], bsz=bsz, tl=tl)
        y_d = _gdn_mixer(h, mix_norm[i], w_gqkv, w_gg, w_gb, w_ga, gdn_conv_w[i], gdn_dt_bias[i], gdn_a_log[i],
                         gdn_norm[i], bsz=bsz, tl=tl // bsz)
        h = _merge(h, mix_norm[i], y_s5, su, s5_d[i], s5_w_glu, s5_b_glu[i], (y_b, y_c, y_d), w_gate, b_gate[i],
                   w_branch, w_out, layer=i, tm=tm)
        h = _ffn(h, ffn2_norm[i], ffn2_w_in, ffn2_w_out, layer=i, tm=tm,
                 ple=(ple_norm[i], p, ple_w_gate, ple_w_proj, final_norm), final_norm=(i == depth - 1))
    return h.reshape(bsz, s, d)
```

```python
import functools

import jax
import jax.numpy as jnp
from jax import lax
from jax.experimental import pallas as pl
from jax.experimental.pallas import tpu as pltpu

EPS = 1e-6
BF16 = jnp.bfloat16
F32 = jnp.float32

V7X_VMEM_BYTES = 64 * 1024 * 1024
VMEM_LIMIT_BYTES = V7X_VMEM_BYTES - 8 * 1024 * 1024
SUBLANES = 8
LANES = 128

BRANCH_WIDTH = 512
S5_GROUP_CH = 16
S5_STATE = 64
S5_CHUNK = 16
S5_OCT = LANES // S5_GROUP_CH
S5_NPOW = 16
LRU_C = 8.0
M2_HEAD_DIM = 64
M2_HEADS = BRANCH_WIDTH // M2_HEAD_DIM
M2_GROUPS = 2
M2_STATE = 128
M2_CONV_DIM = BRANCH_WIDTH + 2 * M2_GROUPS * M2_STATE
M2_CHUNK = 128
GDN_HEAD_DIM = 128
GDN_HEADS = BRANCH_WIDTH // GDN_HEAD_DIM
GDN_CHUNK = 64
GDN_INTERLEAVE = 4
MXU_WIDTH = 256
IN_SPLITS = (BRANCH_WIDTH, BRANCH_WIDTH, BRANCH_WIDTH, BRANCH_WIDTH, M2_CONV_DIM, M2_HEADS, 3 * BRANCH_WIDTH, BRANCH_WIDTH, GDN_HEADS, GDN_HEADS)


def _rms(x, g):
    return x * lax.rsqrt(jnp.mean(x * x, axis=-1, keepdims=True) + EPS) * g


def _silu(x):
    return x * jax.nn.sigmoid(x)


def _mm(a, b):
    return jnp.dot(a.astype(BF16), b.astype(BF16), preferred_element_type=F32)


def _mm_nt(a, b):
    return lax.dot_general(a.astype(BF16), b.astype(BF16), (((1,), (1,)), ((), ())), preferred_element_type=F32)


def _const_spec(shape):
    return pl.BlockSpec(shape, lambda *_: (0,) * len(shape))


def _row(v):
    return v.reshape(1, -1)


def _seq_params():
    return pltpu.CompilerParams(dimension_semantics=("parallel", "arbitrary"), vmem_limit_bytes=VMEM_LIMIT_BYTES)


def _ffn_body(h_ref, g_ref, wi_ref, wo_ref, *rest, ple, final_norm):
    o_ref = rest[-1]
    f = wo_ref.shape[0]
    h = h_ref[...]
    xn = _rms(h, g_ref[...]).astype(BF16)
    acts = []
    for s in range(0, f, MXU_WIDTH):
        gate = jnp.dot(xn, wi_ref[:, s:s + MXU_WIDTH], preferred_element_type=F32)
        up = jnp.dot(xn, wi_ref[:, f + s:f + s + MXU_WIDTH], preferred_element_type=F32)
        acts.append((_silu(gate) * up).astype(BF16))
    out = h + 0.5 * jnp.dot(jnp.concatenate(acts, axis=1), wo_ref[...], preferred_element_type=F32)
    if ple:
        pg_ref, p_ref, wg_ref, wp_ref, fg_ref = rest[:-1]
        out = out + jax.nn.sigmoid(_mm(_rms(out, pg_ref[...]), wg_ref[...])) * _mm(p_ref[...], wp_ref[...])
        if final_norm:
            out = _rms(out, fg_ref[...])
    o_ref[...] = out


def _layer_spec(stacked, layer):
    shape = stacked.shape[1:]
    return pl.BlockSpec((None,) + shape, lambda *_: (layer,) + (0,) * len(shape), pipeline_mode=pl.Buffered(1))


def _tile_params():
    return pltpu.CompilerParams(dimension_semantics=("parallel",), vmem_limit_bytes=VMEM_LIMIT_BYTES)


def _ffn(h, norm_g, w_in, w_out, *, layer, tm, ple=None, final_norm=False):
    t, d = h.shape
    in_specs = [pl.BlockSpec((tm, d), lambda i: (i, 0)), _const_spec((1, d)),
                _layer_spec(w_in, layer), _layer_spec(w_out, layer)]
    args = [h, _row(norm_g), w_in, w_out]
    if ple is not None:
        pg, p, wg, wp, fg = ple
        in_specs += [_const_spec((1, d)), pl.BlockSpec((None, tm, p.shape[-1]), lambda i: (layer, i, 0)),
                     _layer_spec(wg, layer), _layer_spec(wp, layer), _const_spec((1, d))]
        args += [_row(pg), p, wg, wp, _row(fg)]
    return pl.pallas_call(
        functools.partial(_ffn_body, ple=ple is not None, final_norm=final_norm),
        out_shape=jax.ShapeDtypeStruct((t, d), F32),
        grid=(t // tm,),
        in_specs=in_specs,
        out_specs=pl.BlockSpec((tm, d), lambda i: (i, 0)),
        compiler_params=_tile_params(),
        name="ffn",
    )(*args)


def _merge_body(h_ref, g_ref, ys5_ref, su_ref, dsk_ref, wglu_ref, bglu_ref, yb_ref, yc_ref, yd_ref, wg_ref, bg_ref,
                wb_ref, wo_ref, o_ref, stage_scr):
    h = h_ref[...]
    d = h.shape[-1]
    u = _rms(h, g_ref[...]).astype(BF16)
    z = jax.nn.gelu(_unfold_chunks(ys5_ref, stage_scr) + dsk_ref[...] * su_ref[...])
    ya = (z * jax.nn.sigmoid(_mm(z, wglu_ref[...]) + bglu_ref[...])).astype(BF16)
    mixed = jnp.zeros(h.shape, F32)
    for n, y in enumerate((ya, yb_ref[...], yc_ref[...], yd_ref[...])):
        gate = jax.nn.sigmoid(
            jnp.dot(u, wg_ref[:, n * d:(n + 1) * d], preferred_element_type=F32) + bg_ref[:, n * d:(n + 1) * d])
        mixed = mixed + gate * jnp.dot(y, wb_ref[n], preferred_element_type=F32)
    o_ref[...] = h + _mm(mixed, wo_ref[...])


def _merge(h, norm_g, y_s5, su, d_skip, w_glu, b_glu, ys, w_gate, b_gate, w_branch, w_out, *, layer, tm):
    t, d = h.shape
    nblk, _, fold = y_s5.shape
    w = nblk * LANES
    y_spec = pl.BlockSpec((tm, w), lambda i: (i, 0))
    return pl.pallas_call(
        _merge_body,
        out_shape=jax.ShapeDtypeStruct((t, d), F32),
        grid=(t // tm,),
        in_specs=[
            pl.BlockSpec((tm, d), lambda i: (i, 0)), _const_spec((1, d)),
            pl.BlockSpec((nblk, tm // S5_CHUNK, fold), lambda i: (0, i, 0)), y_spec,
            _const_spec((1, w)), _layer_spec(w_glu, layer), _const_spec((1, w)),
            y_spec, y_spec, y_spec,
            _layer_spec(w_gate, layer), _const_spec((1, b_gate.shape[-1])),
            _layer_spec(w_branch, layer), _layer_spec(w_out, layer),
        ],
        out_specs=pl.BlockSpec((tm, d), lambda i: (i, 0)),
        scratch_shapes=[pltpu.VMEM((nblk, tm, LANES), F32)],
        compiler_params=_tile_params(),
        name="merge",
    )(h, _row(norm_g), y_s5, su, _row(d_skip), w_glu, _row(b_glu), *ys, w_gate, _row(b_gate), w_branch, w_out)


def _segment_perm(l):
    r = jnp.arange(l)
    src = (r % SUBLANES) * (l // SUBLANES) + r // SUBLANES
    return (src[:, None] == jnp.arange(l)[None, :]).astype(BF16)


def _mixer_call(body, name, h, consts, scratch_shapes, *, bsz, tl, width, s5_input_outs=False):
    t, d = h.shape
    nt = t // bsz // tl
    out_shape = [jax.ShapeDtypeStruct((t, width), BF16)]
    out_specs = [pl.BlockSpec((tl, width), lambda b, i: (b * nt + i, 0))]
    if s5_input_outs:
        nblk, fold = width // LANES, S5_CHUNK * LANES
        out_shape += [jax.ShapeDtypeStruct((t, width), F32), jax.ShapeDtypeStruct((nblk, t // S5_CHUNK, fold), BF16)]
        out_specs += [pl.BlockSpec((tl, width), lambda b, i: (b * nt + i, 0)),
                      pl.BlockSpec((nblk, tl // S5_CHUNK, fold), lambda b, i: (0, b * nt + i, 0))]
    outs = pl.pallas_call(
        body,
        out_shape=tuple(out_shape),
        grid=(bsz, nt),
        in_specs=[pl.BlockSpec((tl, d), lambda b, i: (b * nt + i, 0))] + [_const_spec(c.shape) for c in consts],
        out_specs=tuple(out_specs),
        scratch_shapes=scratch_shapes,
        compiler_params=_seq_params(),
        name=name,
    )(h, *consts)
    return outs if s5_input_outs else outs[0]


def _fold_chunks(x, stage_scr, out_ref):
    rows = x.shape[0]
    for j in range(stage_scr.shape[0]):
        stage_scr[j] = x[:, j * LANES:(j + 1) * LANES]
    for j in range(stage_scr.shape[0]):
        for s in range(S5_CHUNK):
            out_ref[j, :, s * LANES:(s + 1) * LANES] = (
                stage_scr[j, pl.ds(s, rows // S5_CHUNK, stride=S5_CHUNK), :].astype(out_ref.dtype))


def _unfold_chunks(in_ref, stage_scr):
    rows = stage_scr.shape[1]
    for j in range(stage_scr.shape[0]):
        for s in range(S5_CHUNK):
            stage_scr[j, pl.ds(s, rows // S5_CHUNK, stride=S5_CHUNK), :] = in_ref[j, :, s * LANES:(s + 1) * LANES]
    return jnp.concatenate([stage_scr[j] for j in range(stage_scr.shape[0])], axis=1)


def _conv_halo(cur_tail, prev_tail):
    row = lax.broadcasted_iota(jnp.int32, cur_tail.shape, 0) % SUBLANES
    slab = jnp.where(row == SUBLANES - 1, prev_tail, cur_tail)
    n = cur_tail.shape[0] // SUBLANES
    return jnp.concatenate(
        [pltpu.roll(slab[g * SUBLANES:(g + 1) * SUBLANES], 1, 0) for g in range(n)], axis=0)


def _lru_body(h_ref, g_ref, p_ref, pt_ref, wx_ref, wg_ref, ws5_ref, cw_ref, cb_ref, wr_ref, br_ref, wi_ref, bi_ref,
              lam_ref, o_ref, su_ref, suf_ref, a_scr, b_scr, state_scr, tail_scr, stage_scr):
    l, w = a_scr.shape
    ls = l // SUBLANES
    kconv = cw_ref.shape[0]
    halo = (kconv - 1) * SUBLANES

    @pl.when(pl.program_id(1) == 0)
    def _():
        state_scr[...] = jnp.zeros_like(state_scr)
        tail_scr[...] = jnp.zeros_like(tail_scr)

    u = _rms(h_ref[...], g_ref[...]).astype(BF16)
    su = jnp.dot(u, ws5_ref[...], preferred_element_type=F32)
    su_ref[...] = su
    _fold_chunks(su, stage_scr, suf_ref)
    up = jnp.dot(p_ref[...], u, preferred_element_type=F32).astype(BF16)
    x = jnp.dot(up, wx_ref[...], preferred_element_type=F32)
    gate = jnp.dot(up, wg_ref[...], preferred_element_type=F32)

    cur_tail = x[l - halo:, :]
    xpad = jnp.concatenate([_conv_halo(cur_tail, tail_scr[...]), x], axis=0)
    tail_scr[...] = cur_tail
    xc = cb_ref[...] + sum(cw_ref[k:k + 1, :] * xpad[k * SUBLANES:k * SUBLANES + l] for k in range(kconv))

    r = jax.nn.sigmoid(_mm(xc, wr_ref[...]) + br_ref[...])
    ig = jax.nn.sigmoid(_mm(xc, wi_ref[...]) + bi_ref[...])
    log_a = (-LRU_C * jax.nn.softplus(-lam_ref[...])) * r
    a_scr[...] = jnp.exp(log_a)
    b_scr[...] = jnp.sqrt(-jnp.tanh(log_a) * (jnp.exp(2.0 * log_a) + 1.0)) * (ig * xc)

    def step(i, carry):
        hh, pp = carry
        rows = pl.ds(pl.multiple_of(i * SUBLANES, SUBLANES), SUBLANES)
        a_i = a_scr[rows, :]
        hh = a_i * hh + b_scr[rows, :]
        pp = a_i * pp
        b_scr[rows, :] = hh
        a_scr[rows, :] = pp
        return hh, pp

    h_end, p_end = lax.fori_loop(0, ls, step, (jnp.zeros((SUBLANES, w), F32), jnp.ones((SUBLANES, w), F32)),
                                 unroll=4)
    c = state_scr[...]
    carries = []
    for j in range(SUBLANES):
        carries.append(c)
        c = h_end[j:j + 1, :] + p_end[j:j + 1, :] * c
    state_scr[...] = c
    cmat = jnp.concatenate(carries, axis=0)
    hfull = b_scr[...].reshape(ls, SUBLANES, w) + a_scr[...].reshape(ls, SUBLANES, w) * cmat[None]
    out = (hfull.reshape(l, w) * jax.nn.gelu(gate)).astype(BF16)
    o_ref[...] = jnp.dot(pt_ref[...], out, preferred_element_type=F32).astype(BF16)


def _lru_mixer(h, norm_g, w_x, w_g, w_s5, conv_w, conv_b, w_r, b_r, w_i, b_i, lam, *, bsz, tl):
    w = w_x.shape[1]
    perm = _segment_perm(tl)
    blockdiag = lambda m: jax.scipy.linalg.block_diag(*m).astype(BF16)
    consts = (_row(norm_g), perm, perm.T, w_x, w_g, w_s5, conv_w, _row(conv_b), blockdiag(w_r), _row(b_r),
              blockdiag(w_i), _row(b_i), _row(lam))
    scratch = [pltpu.VMEM((tl, w), F32), pltpu.VMEM((tl, w), F32), pltpu.VMEM((1, w), F32),
               pltpu.VMEM(((conv_w.shape[0] - 1) * SUBLANES, w), F32), pltpu.VMEM((w // LANES, tl, LANES), F32)]
    return _mixer_call(_lru_body, "rglru", h, consts, scratch, bsz=bsz, tl=tl, width=w, s5_input_outs=True)


def _cmul(ar, ai, br, bi):
    return ar * br - ai * bi, ar * bi + ai * br


def _split3(x):
    hi = x.astype(BF16)
    r1 = x - hi.astype(F32)
    mid = r1.astype(BF16)
    lo = (r1 - mid.astype(F32)).astype(BF16)
    return hi, mid, lo


def _mm_exact_rhs(x, e):
    return sum(jnp.dot(part, e, preferred_element_type=F32) for part in _split3(x))


def _bmm_f32(a, b):
    a3, b3 = _split3(a), _split3(b)
    acc = None
    for i, ap in enumerate(a3):
        for j, bp in enumerate(b3):
            if i + j <= 2:
                t = jnp.einsum('gij,gjk->gik', ap, bp, preferred_element_type=F32)
                acc = t if acc is None else acc + t
    return acc


def _place(x, sel):
    return jnp.einsum('gij,gjk->gik', x.astype(BF16), sel, preferred_element_type=F32)


def _lane_sel(shape, target):
    g, r, n = (lax.broadcasted_iota(jnp.int32, shape, k) for k in range(3))
    return (n == target(g, r)).astype(BF16)


def _s5_prep_body(ls_ref, arow_ref, acol_ref, bt_ref, crep_ref, kd_ref, bst_ref, cst_ref, pw_ref):
    gb, _, p = arow_ref.shape
    ch = S5_GROUP_CH
    q = S5_CHUNK
    npw = pw_ref.shape[1]
    step = jnp.exp(ls_ref[...])
    a_re, a_im = arow_ref[:, 0:1, :], arow_ref[:, 1:2, :]
    sa_re, sa_im = a_re * step, a_im * step
    mag = jnp.exp(sa_re)
    ab_re, ab_im = mag * jnp.cos(sa_im), mag * jnp.sin(sa_im)
    den = a_re * a_re + a_im * a_im
    num_re = ab_re - 1.0
    f_re = (num_re * a_re + ab_im * a_im) / den
    f_im = (ab_im * a_re - num_re * a_im) / den
    bt_re, bt_im = bt_ref[:, 0], bt_ref[:, 1]
    bb_re = f_re * bt_re - f_im * bt_im
    bb_im = f_re * bt_im + f_im * bt_re
    ns = (q - 1 - lax.broadcasted_iota(jnp.int32, (1, q, 1), 1)).astype(F32)
    pmag = jnp.exp(ns * sa_re)
    pb_re, pb_im = pmag * jnp.cos(ns * sa_im), pmag * jnp.sin(ns * sa_im)
    pb_re = jnp.broadcast_to(pb_re[:, :, None, :], (gb, q, ch, p)).reshape(gb, q * ch, p)
    pb_im = jnp.broadcast_to(pb_im[:, :, None, :], (gb, q, ch, p)).reshape(gb, q * ch, p)
    bbr = jnp.concatenate([bb_re] * q, axis=1)
    bbi = jnp.concatenate([bb_im] * q, axis=1)
    sr, si = _cmul(bbr, bbi, pb_re, pb_im)
    sin = (_place(sr, _lane_sel((gb, p, 2 * gb * p), lambda g, r: g * p + r))
           + _place(si, _lane_sel((gb, p, 2 * gb * p), lambda g, r: (gb + g) * p + r))).astype(BF16)
    for g in range(gb):
        for s in range(q):
            bst_ref[s * LANES + g * ch:s * LANES + (g + 1) * ch, :] = sin[g, s * ch:(s + 1) * ch, :]
    sc_re, sc_im = acol_ref[:, 0] * step, acol_ref[:, 1] * step
    nd = lax.broadcasted_iota(jnp.int32, (1, 1, q), 2).astype(F32)
    cmag = jnp.exp(sc_re * nd)
    pc_re, pc_im = cmag * jnp.cos(sc_im * nd), cmag * jnp.sin(sc_im * nd)
    expand = (lax.broadcasted_iota(jnp.int32, (q, q * ch), 0)
              == lax.broadcasted_iota(jnp.int32, (q, q * ch), 1) // ch).astype(BF16)
    pc_re = _mm_exact_rhs(pc_re.reshape(gb * p, q), expand).reshape(gb, p, q * ch)
    pc_im = _mm_exact_rhs(pc_im.reshape(gb * p, q), expand).reshape(gb, p, q * ch)
    ca_re, ca_im = _cmul(crep_ref[:, 0], crep_ref[:, 1], pc_re, pc_im)
    kt = _bmm_f32(bb_re, ca_re) - _bmm_f32(bb_im, ca_im)
    for d in range(q):
        sel = _lane_sel((gb, q * ch, LANES), lambda g, r, d=d: jnp.where(r // ch == d, g * ch + r % ch, -1))
        kd_ref[d] = _place(kt, sel).reshape(gb * ch, LANES).astype(BF16)
    abc_mag = jnp.exp(sc_re)
    o_re, o_im = _cmul(ca_re, ca_im, abc_mag * jnp.cos(sc_im), abc_mag * jnp.sin(sc_im))
    sel = _lane_sel((gb, q * ch, q * LANES), lambda g, r: (r // ch) * LANES + g * ch + r % ch)
    out_re, out_im = _place(o_re, sel).astype(BF16), _place(-o_im, sel).astype(BF16)
    for g in range(gb):
        cst_ref[g * p:(g + 1) * p, :] = out_re[g]
        cst_ref[(gb + g) * p:(gb + g + 1) * p, :] = out_im[g]
    nn = jnp.left_shift(1, lax.broadcasted_iota(jnp.int32, (1, npw, 1), 1)).astype(F32) * float(q)
    wmag = jnp.exp(nn * sa_re)
    sel = _lane_sel((gb, p, gb * p), lambda g, r: g * p + r)
    for k, w in enumerate((wmag * jnp.cos(nn * sa_im), wmag * jnp.sin(nn * sa_im))):
        pw_ref[k] = sum(jnp.sum(_place(part, sel), axis=0) for part in _split3(w))


def _s5_prep(log_step, a_re, a_im, b_re, b_im, c_re, c_im):
    g, p = a_re.shape
    ch, q = S5_GROUP_CH, S5_CHUNK
    gb = S5_OCT
    arow = jnp.stack([a_re, a_im], axis=1)
    acol = arow[..., None]
    bt = jnp.stack([b_re, b_im], axis=1).transpose(0, 1, 3, 2)
    crep = jnp.tile(jnp.stack([c_re, c_im], axis=1).transpose(0, 1, 3, 2), (1, 1, 1, q))
    blk = lambda *shape: pl.BlockSpec((gb,) + shape, lambda i: (i,) + (0,) * len(shape))
    out = lambda *shape: pl.BlockSpec((None,) + shape, lambda i: (i,) + (0,) * len(shape))
    nblk, lanes, ns2 = g // gb, q * LANES, 2 * gb * p
    return pl.pallas_call(
        _s5_prep_body,
        out_shape=(jax.ShapeDtypeStruct((nblk, q, LANES, LANES), BF16), jax.ShapeDtypeStruct((nblk, lanes, ns2), BF16),
                   jax.ShapeDtypeStruct((nblk, ns2, lanes), BF16), jax.ShapeDtypeStruct((nblk, 2, S5_NPOW, gb * p), F32)),
        grid=(nblk,),
        in_specs=[blk(1, 1), blk(2, p), blk(2, p, 1), blk(2, ch, p), blk(2, p, q * ch)],
        out_specs=(out(q, LANES, LANES), out(lanes, ns2), out(ns2, lanes), out(2, S5_NPOW, gb * p)),
        compiler_params=_tile_params(),
        name="s5_prep",
    )(log_step.reshape(g, 1, 1), arow, acol, bt, crep)


def _shift_rows(x, sh):
    if sh % SUBLANES == 0:
        return jnp.concatenate([jnp.zeros((sh, x.shape[1]), x.dtype), x[:x.shape[0] - sh]], axis=0)
    row = lax.broadcasted_iota(jnp.int32, x.shape, 0)
    return jnp.where(row >= sh, pltpu.roll(x, sh, 0), 0.0)


def _s5_core_body(x_ref, kd_ref, bst_ref, cst_ref, pw_ref, o_ref, toep_scr):
    rb, lanes = o_ref.shape
    q = S5_CHUNK
    ns = bst_ref.shape[1] // 2

    @pl.when(pl.program_id(1) == 0)
    def _():
        toep_scr[...] = jnp.zeros_like(toep_scr)
        for s in range(q):
            for t in range(s, q):
                toep_scr[s * LANES:(s + 1) * LANES, t * LANES:(t + 1) * LANES] = kd_ref[t - s]

    z = x_ref[...]
    ys = []
    for tp in range(0, q, 2):
        k = (tp + 2) * LANES
        ys.append(jnp.dot(z[:, :k], toep_scr[:k, tp * LANES:(tp + 2) * LANES], preferred_element_type=F32))
    g = jnp.dot(z, bst_ref[...], preferred_element_type=F32)
    e_re, e_im = g[:, :ns], g[:, ns:]
    sh, m = 1, 0
    while sh < rb:
        sr, si = _shift_rows(e_re, sh), _shift_rows(e_im, sh)
        pr, pi = _cmul(pw_ref[0, m:m + 1, :], pw_ref[1, m:m + 1, :], sr, si)
        e_re, e_im = e_re + pr, e_im + pi
        sh, m = sh * 2, m + 1
    s_re = _shift_rows(e_re, 1).astype(BF16)
    s_im = _shift_rows(e_im, 1).astype(BF16)
    o_ref[...] = (jnp.concatenate(ys, axis=1) + jnp.dot(s_re, cst_ref[:ns, :], preferred_element_type=F32)
                  + jnp.dot(s_im, cst_ref[ns:, :], preferred_element_type=F32))


def _s5_core(x, log_step, a_re, a_im, b_re, b_im, c_re, c_im, *, bsz):
    nblk, nrows, lanes = x.shape
    ch, q, p = S5_GROUP_CH, S5_CHUNK, S5_STATE
    o8 = S5_OCT
    rb = nrows // bsz
    kd, bst_o, cst_o, pw_o = _s5_prep(log_step, a_re, a_im, b_re, b_im, c_re, c_im)
    wspec = lambda *shape: pl.BlockSpec((None,) + shape, lambda o, b: (o,) + (0,) * len(shape),
                                        pipeline_mode=pl.Buffered(1))
    y = pl.pallas_call(
        _s5_core_body,
        out_shape=jax.ShapeDtypeStruct((nblk, nrows, lanes), F32),
        grid=(nblk, bsz),
        in_specs=[pl.BlockSpec((None, rb, lanes), lambda o, b: (o, b, 0)),
                  wspec(q, LANES, LANES), wspec(lanes, 2 * o8 * p), wspec(2 * o8 * p, lanes),
                  wspec(2, S5_NPOW, o8 * p)],
        out_specs=pl.BlockSpec((None, rb, lanes), lambda o, b: (o, b, 0)),
        scratch_shapes=[pltpu.VMEM((lanes, lanes), BF16)],
        compiler_params=_seq_params(),
        name="s5_core",
    )(x, kd, bst_o, cst_o, pw_o)
    return y


def _causal_conv(x, tail_scr, cw_ref, first):
    l = x.shape[0]
    k = cw_ref.shape[0]

    @pl.when(first)
    def _():
        tail_scr[...] = jnp.zeros_like(tail_scr)

    xpad = jnp.concatenate([tail_scr[...], x], axis=0)
    tail_scr[...] = x[l - SUBLANES:, :]
    acc = cw_ref[k - 1:k, :] * x
    for j in range(k - 1):
        acc = acc + cw_ref[j:j + 1, :] * pltpu.roll(xpad, k - 1 - j, 0)[SUBLANES:]
    return acc


def _split_dot(tri, x):
    hi = x.astype(BF16)
    lo = (x - hi.astype(F32)).astype(BF16)
    return jnp.dot(tri, hi, preferred_element_type=F32) + jnp.dot(tri, lo, preferred_element_type=F32)


def _lower_tri(q):
    return lax.broadcasted_iota(jnp.int32, (q, q), 0) >= lax.broadcasted_iota(jnp.int32, (q, q), 1)


def _pad_lanes(x):
    return jnp.pad(x, [(0, 0)] * (x.ndim - 1) + [(0, LANES - x.shape[-1])])


def _spread_heads(x, first, nheads):
    return jnp.concatenate(
        [jnp.broadcast_to(x[:, first + hd:first + hd + 1], (x.shape[0], LANES)) for hd in range(nheads)], axis=1)


def _decay_matrix(cs, cst, lane, causal):
    diff = cs[:, lane:lane + 1] - cst[lane:lane + 1, :]
    return jnp.where(causal, jnp.exp(jnp.where(causal, diff, 0.0)), 0.0)


def _ssd_body(h_ref, g_ref, wz_ref, wx_ref, wdt_ref, cw_ref, cb_ref, dtb_ref, alog_ref, dsk_ref, ng_ref, o_ref,
              tail_scr, state_scr):
    l, w = o_ref.shape
    ngroups, nstate, gw = state_scr.shape
    q = M2_CHUNK
    first = pl.program_id(1) == 0

    @pl.when(first)
    def _():
        state_scr[...] = jnp.zeros_like(state_scr)

    u = _rms(h_ref[...], g_ref[...]).astype(BF16)
    z = jnp.dot(u, wz_ref[...], preferred_element_type=F32)
    xbc_raw = jnp.dot(u, wx_ref[...], preferred_element_type=F32)
    dt_raw = jnp.dot(u, wdt_ref[...], preferred_element_type=F32)
    xbc = _silu(_causal_conv(xbc_raw, tail_scr, cw_ref, first) + cb_ref[...])
    xs = xbc[:, :w]
    bm = xbc[:, w:w + ngroups * nstate]
    cm = xbc[:, w + ngroups * nstate:]
    dt = jax.nn.softplus(dt_raw + dtb_ref[...])
    da = dt * (-jnp.exp(alog_ref[...]))
    xdt = xs * dt

    causal = _lower_tri(q)
    tri = causal.astype(BF16)
    head_of_lane = lax.broadcasted_iota(jnp.int32, (1, gw), 1) // M2_HEAD_DIM

    states = [state_scr[g] for g in range(ngroups)]
    for c in range(l // q):
        rows = slice(c * q, (c + 1) * q)
        cs = _split_dot(tri, da[rows])
        cs_last = cs[q - 1:q, :]
        dte = jnp.exp(cs_last - cs)
        ecs = jnp.exp(cs)
        cdec = jnp.exp(cs_last)
        cst = cs.T
        ychunk = []
        for g in range(ngroups):
            gl = slice(g * gw, (g + 1) * gw)
            bg = bm[rows, g * nstate:(g + 1) * nstate]
            cg = cm[rows, g * nstate:(g + 1) * nstate]
            cb = _mm_nt(cg, bg)
            xg = xdt[rows, gl]
            yg = _mm(cg, states[g]) * ecs[:, gl]
            for hh in range(gw // M2_HEAD_DIM):
                dec = _decay_matrix(cs, cst, g * gw + hh * M2_HEAD_DIM, causal)
                yg = yg + _mm(cb * dec, jnp.where(head_of_lane == hh, xg, 0.0))
            states[g] = states[g] * cdec[:, gl] + _mm(bg.T, xg * dte[:, gl])
            ychunk.append(yg)
        y = jnp.concatenate(ychunk, axis=1) + dsk_ref[...] * xs[rows]
        y = y * _silu(z[rows])
        o_ref[rows, :] = _rms(y, ng_ref[...]).astype(BF16)
    for g in range(ngroups):
        state_scr[g] = states[g]


def _ssd_mixer(h, norm_g, w_z, w_xbc, w_dt, conv_w, conv_b, dt_bias, a_log, d_skip, out_norm_g, *, bsz, tl):
    w = w_z.shape[1]
    cdim = w_xbc.shape[1]
    rep = lambda v: jnp.repeat(v, M2_HEAD_DIM).reshape(1, w)
    consts = (_row(norm_g), w_z, w_xbc, jnp.repeat(w_dt, M2_HEAD_DIM, axis=1), conv_w, _row(conv_b), rep(dt_bias),
              rep(a_log), rep(d_skip), _row(out_norm_g))
    scratch = [pltpu.VMEM((SUBLANES, cdim), F32),
               pltpu.VMEM((M2_GROUPS, M2_STATE, w // M2_GROUPS), F32)]
    return _mixer_call(_ssd_body, "ssd", h, consts, scratch, bsz=bsz, tl=tl, width=w)


def _l2norm(x):
    return x * lax.rsqrt(jnp.sum(x * x, axis=-1, keepdims=True) + EPS)


def _gdn_body(h_ref, g_ref, wqkv_ref, wg_ref, wba_ref, cw_ref, dtb_ref, alog_ref, ng_ref, o_ref,
              tail_scr, state_scr):
    nb, lt, w = o_ref.shape
    _, nheads, dk, dv = state_scr.shape
    l = nb * lt
    q = GDN_CHUNK
    sq = nheads * q
    nchunks = l // q
    cpb = lt // q
    first = pl.program_id(0) == 0

    @pl.when(first)
    def _():
        state_scr[...] = jnp.zeros_like(state_scr)

    u = _rms(h_ref[...].reshape(l, h_ref.shape[-1]), g_ref[...]).astype(BF16)
    qkv_raw = jnp.dot(u, wqkv_ref[...], preferred_element_type=F32)
    gate = jnp.dot(u, wg_ref[...], preferred_element_type=F32)
    ba = jnp.dot(u, wba_ref[...], preferred_element_type=F32)
    beta = _spread_heads(jax.nn.sigmoid(ba), 0, nheads)
    glog = _spread_heads(-jnp.exp(alog_ref[...]) * jax.nn.softplus(ba + dtb_ref[...]), nheads, nheads)
    qkv = jnp.concatenate([_silu(_causal_conv(qkv_raw[b * lt:(b + 1) * lt], tail_scr.at[b], cw_ref, first))
                           for b in range(nb)], axis=0)

    tri = _lower_tri(q).astype(BF16)
    gcsc = [_split_dot(tri, glog[c * q:(c + 1) * q]) for c in range(nchunks)]
    gcs = jnp.concatenate(gcsc, axis=0)
    gtot = jnp.concatenate([jnp.broadcast_to(g_[q - 1:q, :], (q, w)) for g_ in gcsc], axis=0)
    egcs = jnp.exp(gcs)

    def stack(x, c):
        return jnp.concatenate([x[c * q:(c + 1) * q, hd * dk:(hd + 1) * dk] for hd in range(nheads)], axis=0)

    qn = jnp.concatenate([_l2norm(qkv[:, hd * dk:(hd + 1) * dk]) * (dk ** -0.5) for hd in range(nheads)], axis=1)
    kn = jnp.concatenate([_l2norm(qkv[:, w + hd * dk:w + (hd + 1) * dk]) for hd in range(nheads)], axis=1)
    kb = kn * beta
    qd = qn * egcs
    kd = kn * jnp.exp(gtot - gcs)
    kbe = kb * egcs
    vb = qkv[:, 2 * w:] * beta

    rs = lax.broadcasted_iota(jnp.int32, (sq, sq), 0)
    cs = lax.broadcasted_iota(jnp.int32, (sq, sq), 1)
    same_head = (rs // q) == (cs // q)
    causal = same_head & (rs >= cs)
    strict = same_head & (rs > cs)
    eye = (rs == cs).astype(F32)
    head_block = (lax.broadcasted_iota(jnp.int32, (sq, nheads * dv), 0) // q
                  == lax.broadcasted_iota(jnp.int32, (sq, nheads * dv), 1) // dv)

    sol, qk, kd_t = [None] * nchunks, [None] * nchunks, [None] * nchunks
    for c0 in range(0, nchunks, GDN_INTERLEAVE):
        cg = list(range(c0, min(c0 + GDN_INTERLEAVE, nchunks)))
        k_s = [stack(kn, c) for c in cg]
        g_s = [stack(gcs, c) for c in cg]
        g_t = jnp.concatenate(g_s, axis=0).T
        decay = []
        for i in range(len(cg)):
            diff = jnp.concatenate([g_s[i]] * (sq // dk), axis=1) - g_t[0:1, i * sq:(i + 1) * sq]
            decay.append(jnp.where(causal, jnp.exp(jnp.where(causal, diff, 0.0)), 0.0))
        kk = [_mm_nt(stack(kb, c), k_s[i]) for i, c in enumerate(cg)]
        for i, c in enumerate(cg):
            qk[c] = _mm_nt(stack(qn, c), k_s[i]) * decay[i]
        pw = [jnp.where(strict, -(kk[i] * decay[i]), 0.0) for i in range(len(cg))]
        inv = [eye + p_ for p_ in pw]
        span = 2
        while span < q:
            pw = [_mm(p_, p_) for p_ in pw]
            inv = [a_ + _mm(a_, p_) for a_, p_ in zip(inv, pw)]
            span *= 2
        kdt = jnp.concatenate([stack(kd, c) for c in cg], axis=0).T
        for i, c in enumerate(cg):
            sol[c] = _mm(inv[i], jnp.concatenate([stack(vb, c), stack(kbe, c)], axis=1))
            kd_t[c] = kdt[:, i * sq:(i + 1) * sq]

    states = [[state_scr[b, hd] for hd in range(nheads)] for b in range(nb)]
    bs = range(nb)
    for cl in range(cpb):
        csb = [b * cpb + cl for b in bs]
        qd_s = [stack(qd, c) for c in csb]
        r = [[_mm(jnp.concatenate([sol[c][hd * q:(hd + 1) * q, dv:], qd_s[b][hd * q:(hd + 1) * q]], axis=0),
                  states[b][hd]) for hd in range(nheads)] for b, c in zip(bs, csb)]
        v_new = [jnp.concatenate([sol[c][hd * q:(hd + 1) * q, :dv] - r[b][hd][:q] for hd in range(nheads)], axis=0)
                 for b, c in zip(bs, csb)]
        upd = [_mm(kd_t[c], jnp.where(head_block, jnp.concatenate([v_new[b]] * nheads, axis=1), 0.0))
               for b, c in zip(bs, csb)]
        o_s = [jnp.concatenate([r[b][hd][q:] for hd in range(nheads)], axis=0) + _mm(qk[c], v_new[b])
               for b, c in zip(bs, csb)]
        for b, c in zip(bs, csb):
            cdec = jnp.exp(gtot[c * q:c * q + 1, :])
            for hd in range(nheads):
                hl = slice(hd * dk, (hd + 1) * dk)
                states[b][hd] = states[b][hd] * cdec[:, hl] + upd[b][:, hl]
                o_ref[b, cl * q:(cl + 1) * q, hl] = (
                    _rms(o_s[b][hd * q:(hd + 1) * q], ng_ref[...]) * _silu(gate[c * q:(c + 1) * q, hl])).astype(BF16)
    for b in bs:
        for hd in range(nheads):
            state_scr[b, hd] = states[b][hd]


def _gdn_mixer(h, norm_g, w_qkv, w_g, w_beta, w_a, conv_w, dt_bias, a_log, out_norm_g, *, bsz, tl):
    t, d = h.shape
    s = t // bsz
    w = w_g.shape[1]
    cdim = w_qkv.shape[1]
    after_beta = lambda v: _pad_lanes(jnp.concatenate([jnp.zeros_like(v), v]).reshape(1, -1))
    consts = (_row(norm_g), w_qkv, w_g, _pad_lanes(jnp.concatenate([w_beta, w_a], axis=1)), conv_w,
              after_beta(dt_bias), after_beta(a_log), _row(out_norm_g))
    out = pl.pallas_call(
        _gdn_body,
        out_shape=jax.ShapeDtypeStruct((bsz, s, w), BF16),
        grid=(s // tl,),
        in_specs=[pl.BlockSpec((bsz, tl, d), lambda i: (0, i, 0))] + [_const_spec(c.shape) for c in consts],
        out_specs=pl.BlockSpec((bsz, tl, w), lambda i: (0, i, 0)),
        scratch_shapes=[pltpu.VMEM((bsz, SUBLANES, cdim), F32),
                        pltpu.VMEM((bsz, w // GDN_HEAD_DIM, GDN_HEAD_DIM, GDN_HEAD_DIM), F32)],
        compiler_params=pltpu.CompilerParams(dimension_semantics=("arbitrary",), vmem_limit_bytes=VMEM_LIMIT_BYTES),
        name="gdn",
    )(h.reshape(bsz, s, d), *consts)
    return out.reshape(t, w)


def _col_slices(sizes):
    out, start = [], 0
    for size in sizes:
        out.append(slice(start, start + size))
        start += size
    return out


def kernel(x, p, ffn1_norm, ffn1_w_in, ffn1_w_out, mix_norm, w_in, w_gate, b_gate, s5_log_step, s5_a_re, s5_a_im, s5_b_re, s5_b_im, s5_c_re, s5_c_im, s5_d, s5_w_glu, s5_b_glu, lru_conv_w, lru_conv_b, lru_w_r, lru_b_r, lru_w_i, lru_b_i, lru_lambda, m2_conv_w, m2_conv_b, m2_dt_bias, m2_a_log, m2_d, m2_norm, gdn_conv_w, gdn_dt_bias, gdn_a_log, gdn_norm, w_branch, w_out, ffn2_norm, ffn2_w_in, ffn2_w_out, ple_norm, ple_w_gate, ple_w_proj, final_norm):
    bsz, s, d = x.shape
    depth = p.shape[0]
    t = bsz * s
    h = x.reshape(t, d)
    p = p.reshape(depth, t, -1)
    tm, tl = 512, 512
    bf = lambda a: a.astype(BF16)
    ffn1_w_in, ffn1_w_out, ffn2_w_in, ffn2_w_out = bf(ffn1_w_in), bf(ffn1_w_out), bf(ffn2_w_in), bf(ffn2_w_out)
    w_gate, w_branch, w_out, s5_w_glu = bf(w_gate), bf(w_branch), bf(w_out), bf(s5_w_glu)
    ple_w_gate, ple_w_proj = bf(ple_w_gate), bf(ple_w_proj)
    for i in range(depth):
        h = _ffn(h, ffn1_norm[i], ffn1_w_in, ffn1_w_out, layer=i, tm=tm)
        (w_s5, w_lx, w_lg, w_mz, w_mx, w_mdt, w_gqkv, w_gg, w_gb, w_ga) = [bf(w_in[i][:, c]) for c in _col_slices(IN_SPLITS)]
        y_b, su, su_folded = _lru_mixer(h, mix_norm[i], w_lx, w_lg, w_s5, lru_conv_w[i], lru_conv_b[i], lru_w_r[i],
                                        lru_b_r[i], lru_w_i[i], lru_b_i[i], lru_lambda[i], bsz=bsz, tl=tl)
        y_s5 = _s5_core(su_folded, s5_log_step[i], s5_a_re[i], s5_a_im[i], s5_b_re[i], s5_b_im[i], s5_c_re[i],
                        s5_c_im[i], bsz=bsz)
        y_c = _ssd_mixer(h, mix_norm[i], w_mz, w_mx, w_mdt, m2_conv_w[i], m2_conv_b[i], m2_dt_bias[i],
                         m2_a_log[i], m2_d[i], m2_norm[i], bsz=bsz, tl=tl)
        y_d = _gdn_mixer(h, mix_norm[i], w_gqkv, w_gg, w_gb, w_ga, gdn_conv_w[i], gdn_dt_bias[i], gdn_a_log[i],
                         gdn_norm[i], bsz=bsz, tl=tl // bsz)
        h = _merge(h, mix_norm[i], y_s5, su, s5_d[i], s5_w_glu, s5_b_glu[i], (y_b, y_c, y_d), w_gate, b_gate[i],
                   w_branch, w_out, layer=i, tm=tm)
        h = _ffn(h, ffn2_norm[i], ffn2_w_in, ffn2_w_out, layer=i, tm=tm,
                 ple=(ple_norm[i], p, ple_w_gate, ple_w_proj, final_norm), final_norm=(i == depth - 1))
    return h.reshape(bsz, s, d)
```

```python
import functools

import jax
import jax.numpy as jnp
from jax import lax
from jax.experimental import pallas as pl
from jax.experimental.pallas import tpu as pltpu

EPS = 1e-6
BF16 = jnp.bfloat16
F32 = jnp.float32

V7X_VMEM_BYTES = 64 * 1024 * 1024
VMEM_LIMIT_BYTES = V7X_VMEM_BYTES - 8 * 1024 * 1024
SUBLANES = 8
LANES = 128

BRANCH_WIDTH = 512
S5_GROUP_CH = 16
S5_STATE = 64
S5_CHUNK = 16
S5_OCT = LANES // S5_GROUP_CH
S5_NPOW = 16
LRU_C = 8.0
M2_HEAD_DIM = 64
M2_HEADS = BRANCH_WIDTH // M2_HEAD_DIM
M2_GROUPS = 2
M2_STATE = 128
M2_CONV_DIM = BRANCH_WIDTH + 2 * M2_GROUPS * M2_STATE
M2_CHUNK = 128
GDN_HEAD_DIM = 128
GDN_HEADS = BRANCH_WIDTH // GDN_HEAD_DIM
GDN_CHUNK = 64
GDN_INTERLEAVE = 4
MXU_WIDTH = 256
IN_SPLITS = (BRANCH_WIDTH, BRANCH_WIDTH, BRANCH_WIDTH, BRANCH_WIDTH, M2_CONV_DIM, M2_HEADS, 3 * BRANCH_WIDTH, BRANCH_WIDTH, GDN_HEADS, GDN_HEADS)


def _rms(x, g):
    return x * lax.rsqrt(jnp.mean(x * x, axis=-1, keepdims=True) + EPS) * g


def _silu(x):
    return x * jax.nn.sigmoid(x)


def _mm(a, b):
    return jnp.dot(a.astype(BF16), b.astype(BF16), preferred_element_type=F32)


def _mm_nt(a, b):
    return lax.dot_general(a.astype(BF16), b.astype(BF16), (((1,), (1,)), ((), ())), preferred_element_type=F32)


def _const_spec(shape):
    return pl.BlockSpec(shape, lambda *_: (0,) * len(shape))


def _row(v):
    return v.reshape(1, -1)


def _seq_params():
    return pltpu.CompilerParams(dimension_semantics=("parallel", "arbitrary"), vmem_limit_bytes=VMEM_LIMIT_BYTES)


def _ffn_body(h_ref, g_ref, wi_ref, wo_ref, *rest, ple, final_norm):
    o_ref = rest[-1]
    f = wo_ref.shape[0]
    h = h_ref[...]
    xn = _rms(h, g_ref[...]).astype(BF16)
    acts = []
    for s in range(0, f, MXU_WIDTH):
        gate = jnp.dot(xn, wi_ref[:, s:s + MXU_WIDTH], preferred_element_type=F32)
        up = jnp.dot(xn, wi_ref[:, f + s:f + s + MXU_WIDTH], preferred_element_type=F32)
        acts.append((_silu(gate) * up).astype(BF16))
    out = h + 0.5 * jnp.dot(jnp.concatenate(acts, axis=1), wo_ref[...], preferred_element_type=F32)
    if ple:
        pg_ref, p_ref, wg_ref, wp_ref, fg_ref = rest[:-1]
        out = out + jax.nn.sigmoid(_mm(_rms(out, pg_ref[...]), wg_ref[...])) * _mm(p_ref[...], wp_ref[...])
        if final_norm:
            out = _rms(out, fg_ref[...])
    o_ref[...] = out


def _layer_spec(stacked, layer):
    shape = stacked.shape[1:]
    return pl.BlockSpec((None,) + shape, lambda *_: (layer,) + (0,) * len(shape), pipeline_mode=pl.Buffered(1))


def _tile_params():
    return pltpu.CompilerParams(dimension_semantics=("parallel",), vmem_limit_bytes=VMEM_LIMIT_BYTES)


def _ffn(h, norm_g, w_in, w_out, *, layer, tm, ple=None, final_norm=False):
    t, d = h.shape
    in_specs = [pl.BlockSpec((tm, d), lambda i: (i, 0)), _const_spec((1, d)),
                _layer_spec(w_in, layer), _layer_spec(w_out, layer)]
    args = [h, _row(norm_g), w_in, w_out]
    if ple is not None:
        pg, p, wg, wp, fg = ple
        in_specs += [_const_spec((1, d)), pl.BlockSpec((None, tm, p.shape[-1]), lambda i: (layer, i, 0)),
                     _layer_spec(wg, layer), _layer_spec(wp, layer), _const_spec((1, d))]
        args += [_row(pg), p, wg, wp, _row(fg)]
    return pl.pallas_call(
        functools.partial(_ffn_body, ple=ple is not None, final_norm=final_norm),
        out_shape=jax.ShapeDtypeStruct((t, d), F32),
        grid=(t // tm,),
        in_specs=in_specs,
        out_specs=pl.BlockSpec((tm, d), lambda i: (i, 0)),
        compiler_params=_tile_params(),
        name="ffn",
    )(*args)


def _merge_body(h_ref, g_ref, ys5_ref, su_ref, dsk_ref, wglu_ref, bglu_ref, yb_ref, yc_ref, yd_ref, wg_ref, bg_ref,
                wb_ref, wo_ref, o_ref, stage_scr):
    h = h_ref[...]
    d = h.shape[-1]
    u = _rms(h, g_ref[...]).astype(BF16)
    z = jax.nn.gelu(_unfold_chunks(ys5_ref, stage_scr) + dsk_ref[...] * su_ref[...])
    ya = (z * jax.nn.sigmoid(_mm(z, wglu_ref[...]) + bglu_ref[...])).astype(BF16)
    mixed = jnp.zeros(h.shape, F32)
    for n, y in enumerate((ya, yb_ref[...], yc_ref[...], yd_ref[...])):
        gate = jax.nn.sigmoid(
            jnp.dot(u, wg_ref[:, n * d:(n + 1) * d], preferred_element_type=F32) + bg_ref[:, n * d:(n + 1) * d])
        mixed = mixed + gate * jnp.dot(y, wb_ref[n], preferred_element_type=F32)
    o_ref[...] = h + _mm(mixed, wo_ref[...])


def _merge(h, norm_g, y_s5, su, d_skip, w_glu, b_glu, ys, w_gate, b_gate, w_branch, w_out, *, layer, tm):
    t, d = h.shape
    nblk, _, fold = y_s5.shape
    w = nblk * LANES
    y_spec = pl.BlockSpec((tm, w), lambda i: (i, 0))
    return pl.pallas_call(
        _merge_body,
        out_shape=jax.ShapeDtypeStruct((t, d), F32),
        grid=(t // tm,),
        in_specs=[
            pl.BlockSpec((tm, d), lambda i: (i, 0)), _const_spec((1, d)),
            pl.BlockSpec((nblk, tm // S5_CHUNK, fold), lambda i: (0, i, 0)), y_spec,
            _const_spec((1, w)), _layer_spec(w_glu, layer), _const_spec((1, w)),
            y_spec, y_spec, y_spec,
            _layer_spec(w_gate, layer), _const_spec((1, b_gate.shape[-1])),
            _layer_spec(w_branch, layer), _layer_spec(w_out, layer),
        ],
        out_specs=pl.BlockSpec((tm, d), lambda i: (i, 0)),
        scratch_shapes=[pltpu.VMEM((nblk, tm, LANES), F32)],
        compiler_params=_tile_params(),
        name="merge",
    )(h, _row(norm_g), y_s5, su, _row(d_skip), w_glu, _row(b_glu), *ys, w_gate, _row(b_gate), w_branch, w_out)


def _segment_perm(l):
    r = jnp.arange(l)
    src = (r % SUBLANES) * (l // SUBLANES) + r // SUBLANES
    return (src[:, None] == jnp.arange(l)[None, :]).astype(BF16)


def _mixer_call(body, name, h, consts, scratch_shapes, *, bsz, tl, width, s5_input_outs=False):
    t, d = h.shape
    nt = t // bsz // tl
    out_shape = [jax.ShapeDtypeStruct((t, width), BF16)]
    out_specs = [pl.BlockSpec((tl, width), lambda b, i: (b * nt + i, 0))]
    if s5_input_outs:
        nblk, fold = width // LANES, S5_CHUNK * LANES
        out_shape += [jax.ShapeDtypeStruct((t, width), F32), jax.ShapeDtypeStruct((nblk, t // S5_CHUNK, fold), BF16)]
        out_specs += [pl.BlockSpec((tl, width), lambda b, i: (b * nt + i, 0)),
                      pl.BlockSpec((nblk, tl // S5_CHUNK, fold), lambda b, i: (0, b * nt + i, 0))]
    outs = pl.pallas_call(
        body,
        out_shape=tuple(out_shape),
        grid=(bsz, nt),
        in_specs=[pl.BlockSpec((tl, d), lambda b, i: (b * nt + i, 0))] + [_const_spec(c.shape) for c in consts],
        out_specs=tuple(out_specs),
        scratch_shapes=scratch_shapes,
        compiler_params=_seq_params(),
        name=name,
    )(h, *consts)
    return outs if s5_input_outs else outs[0]


def _fold_chunks(x, stage_scr, out_ref):
    rows = x.shape[0]
    for j in range(stage_scr.shape[0]):
        stage_scr[j] = x[:, j * LANES:(j + 1) * LANES]
    for j in range(stage_scr.shape[0]):
        for s in range(S5_CHUNK):
            out_ref[j, :, s * LANES:(s + 1) * LANES] = (
                stage_scr[j, pl.ds(s, rows // S5_CHUNK, stride=S5_CHUNK), :].astype(out_ref.dtype))


def _unfold_chunks(in_ref, stage_scr):
    rows = stage_scr.shape[1]
    for j in range(stage_scr.shape[0]):
        for s in range(S5_CHUNK):
            stage_scr[j, pl.ds(s, rows // S5_CHUNK, stride=S5_CHUNK), :] = in_ref[j, :, s * LANES:(s + 1) * LANES]
    return jnp.concatenate([stage_scr[j] for j in range(stage_scr.shape[0])], axis=1)


def _conv_halo(cur_tail, prev_tail):
    row = lax.broadcasted_iota(jnp.int32, cur_tail.shape, 0) % SUBLANES
    slab = jnp.where(row == SUBLANES - 1, prev_tail, cur_tail)
    n = cur_tail.shape[0] // SUBLANES
    return jnp.concatenate(
        [pltpu.roll(slab[g * SUBLANES:(g + 1) * SUBLANES], 1, 0) for g in range(n)], axis=0)


def _lru_body(h_ref, g_ref, p_ref, pt_ref, wx_ref, wg_ref, ws5_ref, cw_ref, cb_ref, wr_ref, br_ref, wi_ref, bi_ref,
              lam_ref, o_ref, su_ref, suf_ref, a_scr, b_scr, state_scr, tail_scr, stage_scr):
    l, w = a_scr.shape
    ls = l // SUBLANES
    kconv = cw_ref.shape[0]
    halo = (kconv - 1) * SUBLANES

    @pl.when(pl.program_id(1) == 0)
    def _():
        state_scr[...] = jnp.zeros_like(state_scr)
        tail_scr[...] = jnp.zeros_like(tail_scr)

    u = _rms(h_ref[...], g_ref[...]).astype(BF16)
    su = jnp.dot(u, ws5_ref[...], preferred_element_type=F32)
    su_ref[...] = su
    _fold_chunks(su, stage_scr, suf_ref)
    up = jnp.dot(p_ref[...], u, preferred_element_type=F32).astype(BF16)
    x = jnp.dot(up, wx_ref[...], preferred_element_type=F32)
    gate = jnp.dot(up, wg_ref[...], preferred_element_type=F32)

    cur_tail = x[l - halo:, :]
    xpad = jnp.concatenate([_conv_halo(cur_tail, tail_scr[...]), x], axis=0)
    tail_scr[...] = cur_tail
    xc = cb_ref[...] + sum(cw_ref[k:k + 1, :] * xpad[k * SUBLANES:k * SUBLANES + l] for k in range(kconv))

    r = jax.nn.sigmoid(_mm(xc, wr_ref[...]) + br_ref[...])
    ig = jax.nn.sigmoid(_mm(xc, wi_ref[...]) + bi_ref[...])
    log_a = (-LRU_C * jax.nn.softplus(-lam_ref[...])) * r
    a_scr[...] = jnp.exp(log_a)
    b_scr[...] = jnp.sqrt(-jnp.tanh(log_a) * (jnp.exp(2.0 * log_a) + 1.0)) * (ig * xc)

    def step(i, carry):
        hh, pp = carry
        rows = pl.ds(pl.multiple_of(i * SUBLANES, SUBLANES), SUBLANES)
        a_i = a_scr[rows, :]
        hh = a_i * hh + b_scr[rows, :]
        pp = a_i * pp
        b_scr[rows, :] = hh
        a_scr[rows, :] = pp
        return hh, pp

    h_end, p_end = lax.fori_loop(0, ls, step, (jnp.zeros((SUBLANES, w), F32), jnp.ones((SUBLANES, w), F32)),
                                 unroll=4)
    c = state_scr[...]
    carries = []
    for j in range(SUBLANES):
        carries.append(c)
        c = h_end[j:j + 1, :] + p_end[j:j + 1, :] * c
    state_scr[...] = c
    cmat = jnp.concatenate(carries, axis=0)
    hfull = b_scr[...].reshape(ls, SUBLANES, w) + a_scr[...].reshape(ls, SUBLANES, w) * cmat[None]
    out = (hfull.reshape(l, w) * jax.nn.gelu(gate)).astype(BF16)
    o_ref[...] = jnp.dot(pt_ref[...], out, preferred_element_type=F32).astype(BF16)


def _lru_mixer(h, norm_g, w_x, w_g, w_s5, conv_w, conv_b, w_r, b_r, w_i, b_i, lam, *, bsz, tl):
    w = w_x.shape[1]
    perm = _segment_perm(tl)
    blockdiag = lambda m: jax.scipy.linalg.block_diag(*m).astype(BF16)
    consts = (_row(norm_g), perm, perm.T, w_x, w_g, w_s5, conv_w, _row(conv_b), blockdiag(w_r), _row(b_r),
              blockdiag(w_i), _row(b_i), _row(lam))
    scratch = [pltpu.VMEM((tl, w), F32), pltpu.VMEM((tl, w), F32), pltpu.VMEM((1, w), F32),
               pltpu.VMEM(((conv_w.shape[0] - 1) * SUBLANES, w), F32), pltpu.VMEM((w // LANES, tl, LANES), F32)]
    return _mixer_call(_lru_body, "rglru", h, consts, scratch, bsz=bsz, tl=tl, width=w, s5_input_outs=True)


def _cmul(ar, ai, br, bi):
    return ar * br - ai * bi, ar * bi + ai * br


def _split3(x):
    hi = x.astype(BF16)
    r1 = x - hi.astype(F32)
    mid = r1.astype(BF16)
    lo = (r1 - mid.astype(F32)).astype(BF16)
    return hi, mid, lo


def _mm_exact_rhs(x, e):
    return sum(jnp.dot(part, e, preferred_element_type=F32) for part in _split3(x))


def _bmm_f32(a, b):
    a3, b3 = _split3(a), _split3(b)
    acc = None
    for i, ap in enumerate(a3):
        for j, bp in enumerate(b3):
            if i + j <= 2:
                t = jnp.einsum('gij,gjk->gik', ap, bp, preferred_element_type=F32)
                acc = t if acc is None else acc + t
    return acc


def _place(x, sel):
    return jnp.einsum('gij,gjk->gik', x.astype(BF16), sel, preferred_element_type=F32)


def _lane_sel(shape, target):
    g, r, n = (lax.broadcasted_iota(jnp.int32, shape, k) for k in range(3))
    return (n == target(g, r)).astype(BF16)


def _s5_prep_body(ls_ref, arow_ref, acol_ref, bt_ref, crep_ref, kd_ref, bst_ref, cst_ref, pw_ref):
    gb, _, p = arow_ref.shape
    ch = S5_GROUP_CH
    q = S5_CHUNK
    npw = pw_ref.shape[1]
    step = jnp.exp(ls_ref[...])
    a_re, a_im = arow_ref[:, 0:1, :], arow_ref[:, 1:2, :]
    sa_re, sa_im = a_re * step, a_im * step
    mag = jnp.exp(sa_re)
    ab_re, ab_im = mag * jnp.cos(sa_im), mag * jnp.sin(sa_im)
    den = a_re * a_re + a_im * a_im
    num_re = ab_re - 1.0
    f_re = (num_re * a_re + ab_im * a_im) / den
    f_im = (ab_im * a_re - num_re * a_im) / den
    bt_re, bt_im = bt_ref[:, 0], bt_ref[:, 1]
    bb_re = f_re * bt_re - f_im * bt_im
    bb_im = f_re * bt_im + f_im * bt_re
    ns = (q - 1 - lax.broadcasted_iota(jnp.int32, (1, q, 1), 1)).astype(F32)
    pmag = jnp.exp(ns * sa_re)
    pb_re, pb_im = pmag * jnp.cos(ns * sa_im), pmag * jnp.sin(ns * sa_im)
    pb_re = jnp.broadcast_to(pb_re[:, :, None, :], (gb, q, ch, p)).reshape(gb, q * ch, p)
    pb_im = jnp.broadcast_to(pb_im[:, :, None, :], (gb, q, ch, p)).reshape(gb, q * ch, p)
    bbr = jnp.concatenate([bb_re] * q, axis=1)
    bbi = jnp.concatenate([bb_im] * q, axis=1)
    sr, si = _cmul(bbr, bbi, pb_re, pb_im)
    sin = (_place(sr, _lane_sel((gb, p, 2 * gb * p), lambda g, r: g * p + r))
           + _place(si, _lane_sel((gb, p, 2 * gb * p), lambda g, r: (gb + g) * p + r))).astype(BF16)
    for g in range(gb):
        for s in range(q):
            bst_ref[s * LANES + g * ch:s * LANES + (g + 1) * ch, :] = sin[g, s * ch:(s + 1) * ch, :]
    sc_re, sc_im = acol_ref[:, 0] * step, acol_ref[:, 1] * step
    nd = lax.broadcasted_iota(jnp.int32, (1, 1, q), 2).astype(F32)
    cmag = jnp.exp(sc_re * nd)
    pc_re, pc_im = cmag * jnp.cos(sc_im * nd), cmag * jnp.sin(sc_im * nd)
    expand = (lax.broadcasted_iota(jnp.int32, (q, q * ch), 0)
              == lax.broadcasted_iota(jnp.int32, (q, q * ch), 1) // ch).astype(BF16)
    pc_re = _mm_exact_rhs(pc_re.reshape(gb * p, q), expand).reshape(gb, p, q * ch)
    pc_im = _mm_exact_rhs(pc_im.reshape(gb * p, q), expand).reshape(gb, p, q * ch)
    ca_re, ca_im = _cmul(crep_ref[:, 0], crep_ref[:, 1], pc_re, pc_im)
    kt = _bmm_f32(bb_re, ca_re) - _bmm_f32(bb_im, ca_im)
    for d in range(q):
        sel = _lane_sel((gb, q * ch, LANES), lambda g, r, d=d: jnp.where(r // ch == d, g * ch + r % ch, -1))
        kd_ref[d] = _place(kt, sel).reshape(gb * ch, LANES).astype(BF16)
    abc_mag = jnp.exp(sc_re)
    o_re, o_im = _cmul(ca_re, ca_im, abc_mag * jnp.cos(sc_im), abc_mag * jnp.sin(sc_im))
    sel = _lane_sel((gb, q * ch, q * LANES), lambda g, r: (r // ch) * LANES + g * ch + r % ch)
    out_re, out_im = _place(o_re, sel).astype(BF16), _place(-o_im, sel).astype(BF16)
    for g in range(gb):
        cst_ref[g * p:(g + 1) * p, :] = out_re[g]
        cst_ref[(gb + g) * p:(gb + g + 1) * p, :] = out_im[g]
    nn = jnp.left_shift(1, lax.broadcasted_iota(jnp.int32, (1, npw, 1), 1)).astype(F32) * float(q)
    wmag = jnp.exp(nn * sa_re)
    sel = _lane_sel((gb, p, gb * p), lambda g, r: g * p + r)
    for k, w in enumerate((wmag * jnp.cos(nn * sa_im), wmag * jnp.sin(nn * sa_im))):
        pw_ref[k] = sum(jnp.sum(_place(part, sel), axis=0) for part in _split3(w))


def _s5_prep(log_step, a_re, a_im, b_re, b_im, c_re, c_im):
    g, p = a_re.shape
    ch, q = S5_GROUP_CH, S5_CHUNK
    gb = S5_OCT
    arow = jnp.stack([a_re, a_im], axis=1)
    acol = arow[..., None]
    bt = jnp.stack([b_re, b_im], axis=1).transpose(0, 1, 3, 2)
    crep = jnp.tile(jnp.stack([c_re, c_im], axis=1).transpose(0, 1, 3, 2), (1, 1, 1, q))
    blk = lambda *shape: pl.BlockSpec((gb,) + shape, lambda i: (i,) + (0,) * len(shape))
    out = lambda *shape: pl.BlockSpec((None,) + shape, lambda i: (i,) + (0,) * len(shape))
    nblk, lanes, ns2 = g // gb, q * LANES, 2 * gb * p
    return pl.pallas_call(
        _s5_prep_body,
        out_shape=(jax.ShapeDtypeStruct((nblk, q, LANES, LANES), BF16), jax.ShapeDtypeStruct((nblk, lanes, ns2), BF16),
                   jax.ShapeDtypeStruct((nblk, ns2, lanes), BF16), jax.ShapeDtypeStruct((nblk, 2, S5_NPOW, gb * p), F32)),
        grid=(nblk,),
        in_specs=[blk(1, 1), blk(2, p), blk(2, p, 1), blk(2, ch, p), blk(2, p, q * ch)],
        out_specs=(out(q, LANES, LANES), out(lanes, ns2), out(ns2, lanes), out(2, S5_NPOW, gb * p)),
        compiler_params=_tile_params(),
        name="s5_prep",
    )(log_step.reshape(g, 1, 1), arow, acol, bt, crep)


def _shift_rows(x, sh):
    if sh % SUBLANES == 0:
        return jnp.concatenate([jnp.zeros((sh, x.shape[1]), x.dtype), x[:x.shape[0] - sh]], axis=0)
    row = lax.broadcasted_iota(jnp.int32, x.shape, 0)
    return jnp.where(row >= sh, pltpu.roll(x, sh, 0), 0.0)


def _s5_core_body(x_ref, kd_ref, bst_ref, cst_ref, pw_ref, o_ref, toep_scr):
    rb, lanes = o_ref.shape
    q = S5_CHUNK
    ns = bst_ref.shape[1] // 2

    @pl.when(pl.program_id(1) == 0)
    def _():
        toep_scr[...] = jnp.zeros_like(toep_scr)
        for s in range(q):
            for t in range(s, q):
                toep_scr[s * LANES:(s + 1) * LANES, t * LANES:(t + 1) * LANES] = kd_ref[t - s]

    z = x_ref[...]
    ys = []
    for tp in range(0, q, 2):
        k = (tp + 2) * LANES
        ys.append(jnp.dot(z[:, :k], toep_scr[:k, tp * LANES:(tp + 2) * LANES], preferred_element_type=F32))
    g = jnp.dot(z, bst_ref[...], preferred_element_type=F32)
    e_re, e_im = g[:, :ns], g[:, ns:]
    sh, m = 1, 0
    while sh < rb:
        sr, si = _shift_rows(e_re, sh), _shift_rows(e_im, sh)
        pr, pi = _cmul(pw_ref[0, m:m + 1, :], pw_ref[1, m:m + 1, :], sr, si)
        e_re, e_im = e_re + pr, e_im + pi
        sh, m = sh * 2, m + 1
    s_re = _shift_rows(e_re, 1).astype(BF16)
    s_im = _shift_rows(e_im, 1).astype(BF16)
    o_ref[...] = (jnp.concatenate(ys, axis=1) + jnp.dot(s_re, cst_ref[:ns, :], preferred_element_type=F32)
                  + jnp.dot(s_im, cst_ref[ns:, :], preferred_element_type=F32))


def _s5_core(x, log_step, a_re, a_im, b_re, b_im, c_re, c_im, *, bsz):
    nblk, nrows, lanes = x.shape
    ch, q, p = S5_GROUP_CH, S5_CHUNK, S5_STATE
    o8 = S5_OCT
    rb = nrows // bsz
    kd, bst_o, cst_o, pw_o = _s5_prep(log_step, a_re, a_im, b_re, b_im, c_re, c_im)
    wspec = lambda *shape: pl.BlockSpec((None,) + shape, lambda o, b: (o,) + (0,) * len(shape),
                                        pipeline_mode=pl.Buffered(1))
    y = pl.pallas_call(
        _s5_core_body,
        out_shape=jax.ShapeDtypeStruct((nblk, nrows, lanes), F32),
        grid=(nblk, bsz),
        in_specs=[pl.BlockSpec((None, rb, lanes), lambda o, b: (o, b, 0)),
                  wspec(q, LANES, LANES), wspec(lanes, 2 * o8 * p), wspec(2 * o8 * p, lanes),
                  wspec(2, S5_NPOW, o8 * p)],
        out_specs=pl.BlockSpec((None, rb, lanes), lambda o, b: (o, b, 0)),
        scratch_shapes=[pltpu.VMEM((lanes, lanes), BF16)],
        compiler_params=_seq_params(),
        name="s5_core",
    )(x, kd, bst_o, cst_o, pw_o)
    return y


def _causal_conv(x, tail_scr, cw_ref, first):
    l = x.shape[0]
    k = cw_ref.shape[0]

    @pl.when(first)
    def _():
        tail_scr[...] = jnp.zeros_like(tail_scr)

    xpad = jnp.concatenate([tail_scr[...], x], axis=0)
    tail_scr[...] = x[l - SUBLANES:, :]
    acc = cw_ref[k - 1:k, :] * x
    for j in range(k - 1):
        acc = acc + cw_ref[j:j + 1, :] * pltpu.roll(xpad, k - 1 - j, 0)[SUBLANES:]
    return acc


def _split_dot(tri, x):
    hi = x.astype(BF16)
    lo = (x - hi.astype(F32)).astype(BF16)
    return jnp.dot(tri, hi, preferred_element_type=F32) + jnp.dot(tri, lo, preferred_element_type=F32)


def _lower_tri(q):
    return lax.broadcasted_iota(jnp.int32, (q, q), 0) >= lax.broadcasted_iota(jnp.int32, (q, q), 1)


def _pad_lanes(x):
    return jnp.pad(x, [(0, 0)] * (x.ndim - 1) + [(0, LANES - x.shape[-1])])


def _spread_heads(x, first, nheads):
    return jnp.concatenate(
        [jnp.broadcast_to(x[:, first + hd:first + hd + 1], (x.shape[0], LANES)) for hd in range(nheads)], axis=1)


def _decay_matrix(cs, cst, lane, causal):
    diff = cs[:, lane:lane + 1] - cst[lane:lane + 1, :]
    return jnp.where(causal, jnp.exp(jnp.where(causal, diff, 0.0)), 0.0)


def _ssd_body(h_ref, g_ref, wz_ref, wx_ref, wdt_ref, cw_ref, cb_ref, dtb_ref, alog_ref, dsk_ref, ng_ref, o_ref,
              tail_scr, state_scr):
    l, w = o_ref.shape
    ngroups, nstate, gw = state_scr.shape
    q = M2_CHUNK
    first = pl.program_id(1) == 0

    @pl.when(first)
    def _():
        state_scr[...] = jnp.zeros_like(state_scr)

    u = _rms(h_ref[...], g_ref[...]).astype(BF16)
    z = jnp.dot(u, wz_ref[...], preferred_element_type=F32)
    xbc_raw = jnp.dot(u, wx_ref[...], preferred_element_type=F32)
    dt_raw = jnp.dot(u, wdt_ref[...], preferred_element_type=F32)
    xbc = _silu(_causal_conv(xbc_raw, tail_scr, cw_ref, first) + cb_ref[...])
    xs = xbc[:, :w]
    bm = xbc[:, w:w + ngroups * nstate]
    cm = xbc[:, w + ngroups * nstate:]
    dt = jax.nn.softplus(dt_raw + dtb_ref[...])
    da = dt * (-jnp.exp(alog_ref[...]))
    xdt = xs * dt

    causal = _lower_tri(q)
    tri = causal.astype(BF16)
    head_of_lane = lax.broadcasted_iota(jnp.int32, (1, gw), 1) // M2_HEAD_DIM

    states = [state_scr[g] for g in range(ngroups)]
    for c in range(l // q):
        rows = slice(c * q, (c + 1) * q)
        cs = _split_dot(tri, da[rows])
        cs_last = cs[q - 1:q, :]
        dte = jnp.exp(cs_last - cs)
        ecs = jnp.exp(cs)
        cdec = jnp.exp(cs_last)
        cst = cs.T
        ychunk = []
        for g in range(ngroups):
            gl = slice(g * gw, (g + 1) * gw)
            bg = bm[rows, g * nstate:(g + 1) * nstate]
            cg = cm[rows, g * nstate:(g + 1) * nstate]
            cb = _mm_nt(cg, bg)
            xg = xdt[rows, gl]
            yg = _mm(cg, states[g]) * ecs[:, gl]
            for hh in range(gw // M2_HEAD_DIM):
                dec = _decay_matrix(cs, cst, g * gw + hh * M2_HEAD_DIM, causal)
                yg = yg + _mm(cb * dec, jnp.where(head_of_lane == hh, xg, 0.0))
            states[g] = states[g] * cdec[:, gl] + _mm(bg.T, xg * dte[:, gl])
            ychunk.append(yg)
        y = jnp.concatenate(ychunk, axis=1) + dsk_ref[...] * xs[rows]
        y = y * _silu(z[rows])
        o_ref[rows, :] = _rms(y, ng_ref[...]).astype(BF16)
    for g in range(ngroups):
        state_scr[g] = states[g]


def _ssd_mixer(h, norm_g, w_z, w_xbc, w_dt, conv_w, conv_b, dt_bias, a_log, d_skip, out_norm_g, *, bsz, tl):
    w = w_z.shape[1]
    cdim = w_xbc.shape[1]
    rep = lambda v: jnp.repeat(v, M2_HEAD_DIM).reshape(1, w)
    consts = (_row(norm_g), w_z, w_xbc, jnp.repeat(w_dt, M2_HEAD_DIM, axis=1), conv_w, _row(conv_b), rep(dt_bias),
              rep(a_log), rep(d_skip), _row(out_norm_g))
    scratch = [pltpu.VMEM((SUBLANES, cdim), F32),
               pltpu.VMEM((M2_GROUPS, M2_STATE, w // M2_GROUPS), F32)]
    return _mixer_call(_ssd_body, "ssd", h, consts, scratch, bsz=bsz, tl=tl, width=w)


def _l2norm(x):
    return x * lax.rsqrt(jnp.sum(x * x, axis=-1, keepdims=True) + EPS)


def _gdn_body(h_ref, g_ref, wqkv_ref, wg_ref, wba_ref, cw_ref, dtb_ref, alog_ref, ng_ref, o_ref,
              tail_scr, state_scr):
    nb, lt, w = o_ref.shape
    _, nheads, dk, dv = state_scr.shape
    l = nb * lt
    q = GDN_CHUNK
    sq = nheads * q
    nchunks = l // q
    cpb = lt // q
    first = pl.program_id(0) == 0

    @pl.when(first)
    def _():
        state_scr[...] = jnp.zeros_like(state_scr)

    u = _rms(h_ref[...].reshape(l, h_ref.shape[-1]), g_ref[...]).astype(BF16)
    qkv_raw = jnp.dot(u, wqkv_ref[...], preferred_element_type=F32)
    gate = jnp.dot(u, wg_ref[...], preferred_element_type=F32)
    ba = jnp.dot(u, wba_ref[...], preferred_element_type=F32)
    beta = _spread_heads(jax.nn.sigmoid(ba), 0, nheads)
    glog = _spread_heads(-jnp.exp(alog_ref[...]) * jax.nn.softplus(ba + dtb_ref[...]), nheads, nheads)
    qkv = jnp.concatenate([_silu(_causal_conv(qkv_raw[b * lt:(b + 1) * lt], tail_scr.at[b], cw_ref, first))
                           for b in range(nb)], axis=0)

    tri = _lower_tri(q).astype(BF16)
    gcsc = [_split_dot(tri, glog[c * q:(c + 1) * q]) for c in range(nchunks)]
    gcs = jnp.concatenate(gcsc, axis=0)
    gtot = jnp.concatenate([jnp.broadcast_to(g_[q - 1:q, :], (q, w)) for g_ in gcsc], axis=0)
    egcs = jnp.exp(gcs)

    def stack(x, c):
        return jnp.concatenate([x[c * q:(c + 1) * q, hd * dk:(hd + 1) * dk] for hd in range(nheads)], axis=0)

    qn = jnp.concatenate([_l2norm(qkv[:, hd * dk:(hd + 1) * dk]) * (dk ** -0.5) for hd in range(nheads)], axis=1)
    kn = jnp.concatenate([_l2norm(qkv[:, w + hd * dk:w + (hd + 1) * dk]) for hd in range(nheads)], axis=1)
    kb = kn * beta
    qd = qn * egcs
    kd = kn * jnp.exp(gtot - gcs)
    kbe = kb * egcs
    vb = qkv[:, 2 * w:] * beta

    rs = lax.broadcasted_iota(jnp.int32, (sq, sq), 0)
    cs = lax.broadcasted_iota(jnp.int32, (sq, sq), 1)
    same_head = (rs // q) == (cs // q)
    causal = same_head & (rs >= cs)
    strict = same_head & (rs > cs)
    eye = (rs == cs).astype(F32)
    head_block = (lax.broadcasted_iota(jnp.int32, (sq, nheads * dv), 0) // q
                  == lax.broadcasted_iota(jnp.int32, (sq, nheads * dv), 1) // dv)

    sol, qk, kd_t = [None] * nchunks, [None] * nchunks, [None] * nchunks
    for c0 in range(0, nchunks, GDN_INTERLEAVE):
        cg = list(range(c0, min(c0 + GDN_INTERLEAVE, nchunks)))
        k_s = [stack(kn, c) for c in cg]
        g_s = [stack(gcs, c) for c in cg]
        g_t = jnp.concatenate(g_s, axis=0).T
        decay = []
        for i in range(len(cg)):
            diff = jnp.concatenate([g_s[i]] * (sq // dk), axis=1) - g_t[0:1, i * sq:(i + 1) * sq]
            decay.append(jnp.where(causal, jnp.exp(jnp.where(causal, diff, 0.0)), 0.0))
        kk = [_mm_nt(stack(kb, c), k_s[i]) for i, c in enumerate(cg)]
        for i, c in enumerate(cg):
            qk[c] = _mm_nt(stack(qn, c), k_s[i]) * decay[i]
        pw = [jnp.where(strict, -(kk[i] * decay[i]), 0.0) for i in range(len(cg))]
        inv = [eye + p_ for p_ in pw]
        span = 2
        while span < q:
            pw = [_mm(p_, p_) for p_ in pw]
            inv = [a_ + _mm(a_, p_) for a_, p_ in zip(inv, pw)]
            span *= 2
        kdt = jnp.concatenate([stack(kd, c) for c in cg], axis=0).T
        for i, c in enumerate(cg):
            sol[c] = _mm(inv[i], jnp.concatenate([stack(vb, c), stack(kbe, c)], axis=1))
            kd_t[c] = kdt[:, i * sq:(i + 1) * sq]

    states = [[state_scr[b, hd] for hd in range(nheads)] for b in range(nb)]
    bs = range(nb)
    for cl in range(cpb):
        csb = [b * cpb + cl for b in bs]
        qd_s = [stack(qd, c) for c in csb]
        r = [[_mm(jnp.concatenate([sol[c][hd * q:(hd + 1) * q, dv:], qd_s[b][hd * q:(hd + 1) * q]], axis=0),
                  states[b][hd]) for hd in range(nheads)] for b, c in zip(bs, csb)]
        v_new = [jnp.concatenate([sol[c][hd * q:(hd + 1) * q, :dv] - r[b][hd][:q] for hd in range(nheads)], axis=0)
                 for b, c in zip(bs, csb)]
        upd = [_mm(kd_t[c], jnp.where(head_block, jnp.concatenate([v_new[b]] * nheads, axis=1), 0.0))
               for b, c in zip(bs, csb)]
        o_s = [jnp.concatenate([r[b][hd][q:] for hd in range(nheads)], axis=0) + _mm(qk[c], v_new[b])
               for b, c in zip(bs, csb)]
        for b, c in zip(bs, csb):
            cdec = jnp.exp(gtot[c * q:c * q + 1, :])
            for hd in range(nheads):
                hl = slice(hd * dk, (hd + 1) * dk)
                states[b][hd] = states[b][hd] * cdec[:, hl] + upd[b][:, hl]
                o_ref[b, cl * q:(cl + 1) * q, hl] = (
                    _rms(o_s[b][hd * q:(hd + 1) * q], ng_ref[...]) * _silu(gate[c * q:(c + 1) * q, hl])).astype(BF16)
    for b in bs:
        for hd in range(nheads):
            state_scr[b, hd] = states[b][hd]


def _gdn_mixer(h, norm_g, w_qkv, w_g, w_beta, w_a, conv_w, dt_bias, a_log, out_norm_g, *, bsz, tl):
    t, d = h.shape
    s = t // bsz
    w = w_g.shape[1]
    cdim = w_qkv.shape[1]
    after_beta = lambda v: _pad_lanes(jnp.concatenate([jnp.zeros_like(v), v]).reshape(1, -1))
    consts = (_row(norm_g), w_qkv, w_g, _pad_lanes(jnp.concatenate([w_beta, w_a], axis=1)), conv_w,
              after_beta(dt_bias), after_beta(a_log), _row(out_norm_g))
    out = pl.pallas_call(
        _gdn_body,
        out_shape=jax.ShapeDtypeStruct((bsz, s, w), BF16),
        grid=(s // tl,),
        in_specs=[pl.BlockSpec((bsz, tl, d), lambda i: (0, i, 0))] + [_const_spec(c.shape) for c in consts],
        out_specs=pl.BlockSpec((bsz, tl, w), lambda i: (0, i, 0)),
        scratch_shapes=[pltpu.VMEM((bsz, SUBLANES, cdim), F32),
                        pltpu.VMEM((bsz, w // GDN_HEAD_DIM, GDN_HEAD_DIM, GDN_HEAD_DIM), F32)],
        compiler_params=pltpu.CompilerParams(dimension_semantics=("arbitrary",), vmem_limit_bytes=VMEM_LIMIT_BYTES),
        name="gdn",
    )(h.reshape(bsz, s, d), *consts)
    return out.reshape(t, w)


def _col_slices(sizes):
    out, start = [], 0
    for size in sizes:
        out.append(slice(start, start + size))
        start += size
    return out


def kernel(x, p, ffn1_norm, ffn1_w_in, ffn1_w_out, mix_norm, w_in, w_gate, b_gate, s5_log_step, s5_a_re, s5_a_im, s5_b_re, s5_b_im, s5_c_re, s5_c_im, s5_d, s5_w_glu, s5_b_glu, lru_conv_w, lru_conv_b, lru_w_r, lru_b_r, lru_w_i, lru_b_i, lru_lambda, m2_conv_w, m2_conv_b, m2_dt_bias, m2_a_log, m2_d, m2_norm, gdn_conv_w, gdn_dt_bias, gdn_a_log, gdn_norm, w_branch, w_out, ffn2_norm, ffn2_w_in, ffn2_w_out, ple_norm, ple_w_gate, ple_w_proj, final_norm):
    bsz, s, d = x.shape
    depth = p.shape[0]
    t = bsz * s
    h = x.reshape(t, d)
    p = p.reshape(depth, t, -1)
    tm, tl, tm_ffn = 512, 512, 1024
    bf = lambda a: a.astype(BF16)
    ffn1_w_in, ffn1_w_out, ffn2_w_in, ffn2_w_out = bf(ffn1_w_in), bf(ffn1_w_out), bf(ffn2_w_in), bf(ffn2_w_out)
    w_gate, w_branch, w_out, s5_w_glu = bf(w_gate), bf(w_branch), bf(w_out), bf(s5_w_glu)
    ple_w_gate, ple_w_proj = bf(ple_w_gate), bf(ple_w_proj)
    for i in range(depth):
        h = _ffn(h, ffn1_norm[i], ffn1_w_in, ffn1_w_out, layer=i, tm=tm_ffn)
        (w_s5, w_lx, w_lg, w_mz, w_mx, w_mdt, w_gqkv, w_gg, w_gb, w_ga) = [bf(w_in[i][:, c]) for c in _col_slices(IN_SPLITS)]
        y_b, su, su_folded = _lru_mixer(h, mix_norm[i], w_lx, w_lg, w_s5, lru_conv_w[i], lru_conv_b[i], lru_w_r[i],
                                        lru_b_r[i], lru_w_i[i], lru_b_i[i], lru_lambda[i], bsz=bsz, tl=tl)
        y_s5 = _s5_core(su_folded, s5_log_step[i], s5_a_re[i], s5_a_im[i], s5_b_re[i], s5_b_im[i], s5_c_re[i],
                        s5_c_im[i], bsz=bsz)
        y_c = _ssd_mixer(h, mix_norm[i], w_mz, w_mx, w_mdt, m2_conv_w[i], m2_conv_b[i], m2_dt_bias[i],
                         m2_a_log[i], m2_d[i], m2_norm[i], bsz=bsz, tl=tl)
        y_d = _gdn_mixer(h, mix_norm[i], w_gqkv, w_gg, w_gb, w_ga, gdn_conv_w[i], gdn_dt_bias[i], gdn_a_log[i],
                         gdn_norm[i], bsz=bsz, tl=tl // bsz)
        h = _merge(h, mix_norm[i], y_s5, su, s5_d[i], s5_w_glu, s5_b_glu[i], (y_b, y_c, y_d), w_gate, b_gate[i],
                   w_branch, w_out, layer=i, tm=tm)
        h = _ffn(h, ffn2_norm[i], ffn2_w_in, ffn2_w_out, layer=i, tm=tm_ffn,
                 ple=(ple_norm[i], p, ple_w_gate, ple_w_proj, final_norm), final_norm=(i == depth - 1))
    return h.reshape(bsz, s, d)
```

```python
import functools

import jax
import jax.numpy as jnp
from jax import lax
from jax.experimental import pallas as pl
from jax.experimental.pallas import tpu as pltpu

EPS = 1e-6
BF16 = jnp.bfloat16
F32 = jnp.float32

V7X_VMEM_BYTES = 64 * 1024 * 1024
VMEM_LIMIT_BYTES = V7X_VMEM_BYTES - 8 * 1024 * 1024
SUBLANES = 8
LANES = 128

BRANCH_WIDTH = 512
S5_GROUP_CH = 16
S5_STATE = 64
S5_CHUNK = 16
S5_OCT = LANES // S5_GROUP_CH
S5_NPOW = 16
LRU_C = 8.0
M2_HEAD_DIM = 64
M2_HEADS = BRANCH_WIDTH // M2_HEAD_DIM
M2_GROUPS = 2
M2_STATE = 128
M2_CONV_DIM = BRANCH_WIDTH + 2 * M2_GROUPS * M2_STATE
M2_CHUNK = 128
GDN_HEAD_DIM = 128
GDN_HEADS = BRANCH_WIDTH // GDN_HEAD_DIM
GDN_CHUNK = 64
GDN_INTERLEAVE = 4
MXU_WIDTH = 256
IN_SPLITS = (BRANCH_WIDTH, BRANCH_WIDTH, BRANCH_WIDTH, BRANCH_WIDTH, M2_CONV_DIM, M2_HEADS, 3 * BRANCH_WIDTH, BRANCH_WIDTH, GDN_HEADS, GDN_HEADS)


def _rms(x, g):
    return x * lax.rsqrt(jnp.mean(x * x, axis=-1, keepdims=True) + EPS) * g


def _silu(x):
    return x * jax.nn.sigmoid(x)


def _mm(a, b):
    return jnp.dot(a.astype(BF16), b.astype(BF16), preferred_element_type=F32)


def _mm_nt(a, b):
    return lax.dot_general(a.astype(BF16), b.astype(BF16), (((1,), (1,)), ((), ())), preferred_element_type=F32)


def _const_spec(shape):
    return pl.BlockSpec(shape, lambda *_: (0,) * len(shape))


def _row(v):
    return v.reshape(1, -1)


def _seq_params():
    return pltpu.CompilerParams(dimension_semantics=("parallel", "arbitrary"), vmem_limit_bytes=VMEM_LIMIT_BYTES)


def _ffn_body(h_ref, g_ref, wi_ref, wo_ref, *rest, ple, final_norm):
    o_ref = rest[-1]
    f = wo_ref.shape[0]
    h = h_ref[...]
    xn = _rms(h, g_ref[...]).astype(BF16)
    acts = []
    for s in range(0, f, MXU_WIDTH):
        gate = jnp.dot(xn, wi_ref[:, s:s + MXU_WIDTH], preferred_element_type=F32)
        up = jnp.dot(xn, wi_ref[:, f + s:f + s + MXU_WIDTH], preferred_element_type=F32)
        acts.append((_silu(gate) * up).astype(BF16))
    out = h + 0.5 * jnp.dot(jnp.concatenate(acts, axis=1), wo_ref[...], preferred_element_type=F32)
    if ple:
        pg_ref, p_ref, wg_ref, wp_ref, fg_ref = rest[:-1]
        out = out + jax.nn.sigmoid(_mm(_rms(out, pg_ref[...]), wg_ref[...])) * _mm(p_ref[...], wp_ref[...])
        if final_norm:
            out = _rms(out, fg_ref[...])
    o_ref[...] = out


def _layer_spec(stacked, layer):
    shape = stacked.shape[1:]
    return pl.BlockSpec((None,) + shape, lambda *_: (layer,) + (0,) * len(shape), pipeline_mode=pl.Buffered(1))


def _tile_params():
    return pltpu.CompilerParams(dimension_semantics=("parallel",), vmem_limit_bytes=VMEM_LIMIT_BYTES)


def _ffn(h, norm_g, w_in, w_out, *, layer, tm, ple=None, final_norm=False):
    t, d = h.shape
    in_specs = [pl.BlockSpec((tm, d), lambda i: (i, 0)), _const_spec((1, d)),
                _layer_spec(w_in, layer), _layer_spec(w_out, layer)]
    args = [h, _row(norm_g), w_in, w_out]
    if ple is not None:
        pg, p, wg, wp, fg = ple
        in_specs += [_const_spec((1, d)), pl.BlockSpec((None, tm, p.shape[-1]), lambda i: (layer, i, 0)),
                     _layer_spec(wg, layer), _layer_spec(wp, layer), _const_spec((1, d))]
        args += [_row(pg), p, wg, wp, _row(fg)]
    return pl.pallas_call(
        functools.partial(_ffn_body, ple=ple is not None, final_norm=final_norm),
        out_shape=jax.ShapeDtypeStruct((t, d), F32),
        grid=(t // tm,),
        in_specs=in_specs,
        out_specs=pl.BlockSpec((tm, d), lambda i: (i, 0)),
        compiler_params=_tile_params(),
        name="ffn",
    )(*args)


def _merge_body(h_ref, g_ref, ys5_ref, su_ref, dsk_ref, wglu_ref, bglu_ref, yb_ref, yc_ref, yd_ref, wg_ref, bg_ref,
                wb_ref, wo_ref, o_ref, stage_scr):
    h = h_ref[...]
    d = h.shape[-1]
    u = _rms(h, g_ref[...]).astype(BF16)
    z = jax.nn.gelu(_unfold_chunks(ys5_ref, stage_scr) + dsk_ref[...] * su_ref[...])
    ya = (z * jax.nn.sigmoid(_mm(z, wglu_ref[...]) + bglu_ref[...])).astype(BF16)
    mixed = jnp.zeros(h.shape, F32)
    for n, y in enumerate((ya, yb_ref[...], yc_ref[...], yd_ref[...])):
        gate = jax.nn.sigmoid(
            jnp.dot(u, wg_ref[:, n * d:(n + 1) * d], preferred_element_type=F32) + bg_ref[:, n * d:(n + 1) * d])
        mixed = mixed + gate * jnp.dot(y, wb_ref[n], preferred_element_type=F32)
    o_ref[...] = h + _mm(mixed, wo_ref[...])


def _merge(h, norm_g, y_s5, su, d_skip, w_glu, b_glu, ys, w_gate, b_gate, w_branch, w_out, *, layer, tm):
    t, d = h.shape
    nblk, _, fold = y_s5.shape
    w = nblk * LANES
    y_spec = pl.BlockSpec((tm, w), lambda i: (i, 0))
    return pl.pallas_call(
        _merge_body,
        out_shape=jax.ShapeDtypeStruct((t, d), F32),
        grid=(t // tm,),
        in_specs=[
            pl.BlockSpec((tm, d), lambda i: (i, 0)), _const_spec((1, d)),
            pl.BlockSpec((nblk, tm // S5_CHUNK, fold), lambda i: (0, i, 0)), y_spec,
            _const_spec((1, w)), _layer_spec(w_glu, layer), _const_spec((1, w)),
            y_spec, y_spec, y_spec,
            _layer_spec(w_gate, layer), _const_spec((1, b_gate.shape[-1])),
            _layer_spec(w_branch, layer), _layer_spec(w_out, layer),
        ],
        out_specs=pl.BlockSpec((tm, d), lambda i: (i, 0)),
        scratch_shapes=[pltpu.VMEM((nblk, tm, LANES), F32)],
        compiler_params=_tile_params(),
        name="merge",
    )(h, _row(norm_g), y_s5, su, _row(d_skip), w_glu, _row(b_glu), *ys, w_gate, _row(b_gate), w_branch, w_out)


def _segment_perm(l):
    r = jnp.arange(l)
    src = (r % SUBLANES) * (l // SUBLANES) + r // SUBLANES
    return (src[:, None] == jnp.arange(l)[None, :]).astype(BF16)


def _mixer_call(body, name, h, consts, scratch_shapes, *, bsz, tl, width, s5_input_outs=False):
    t, d = h.shape
    nt = t // bsz // tl
    out_shape = [jax.ShapeDtypeStruct((t, width), BF16)]
    out_specs = [pl.BlockSpec((tl, width), lambda b, i: (b * nt + i, 0))]
    if s5_input_outs:
        nblk, fold = width // LANES, S5_CHUNK * LANES
        out_shape += [jax.ShapeDtypeStruct((t, width), F32), jax.ShapeDtypeStruct((nblk, t // S5_CHUNK, fold), BF16)]
        out_specs += [pl.BlockSpec((tl, width), lambda b, i: (b * nt + i, 0)),
                      pl.BlockSpec((nblk, tl // S5_CHUNK, fold), lambda b, i: (0, b * nt + i, 0))]
    outs = pl.pallas_call(
        body,
        out_shape=tuple(out_shape),
        grid=(bsz, nt),
        in_specs=[pl.BlockSpec((tl, d), lambda b, i: (b * nt + i, 0))] + [_const_spec(c.shape) for c in consts],
        out_specs=tuple(out_specs),
        scratch_shapes=scratch_shapes,
        compiler_params=_seq_params(),
        name=name,
    )(h, *consts)
    return outs if s5_input_outs else outs[0]


def _fold_chunks(x, stage_scr, out_ref):
    rows = x.shape[0]
    for j in range(stage_scr.shape[0]):
        stage_scr[j] = x[:, j * LANES:(j + 1) * LANES]
    for j in range(stage_scr.shape[0]):
        for s in range(S5_CHUNK):
            out_ref[j, :, s * LANES:(s + 1) * LANES] = (
                stage_scr[j, pl.ds(s, rows // S5_CHUNK, stride=S5_CHUNK), :].astype(out_ref.dtype))


def _unfold_chunks(in_ref, stage_scr):
    rows = stage_scr.shape[1]
    for j in range(stage_scr.shape[0]):
        for s in range(S5_CHUNK):
            stage_scr[j, pl.ds(s, rows // S5_CHUNK, stride=S5_CHUNK), :] = in_ref[j, :, s * LANES:(s + 1) * LANES]
    return jnp.concatenate([stage_scr[j] for j in range(stage_scr.shape[0])], axis=1)


def _conv_halo(cur_tail, prev_tail):
    row = lax.broadcasted_iota(jnp.int32, cur_tail.shape, 0) % SUBLANES
    slab = jnp.where(row == SUBLANES - 1, prev_tail, cur_tail)
    n = cur_tail.shape[0] // SUBLANES
    return jnp.concatenate(
        [pltpu.roll(slab[g * SUBLANES:(g + 1) * SUBLANES], 1, 0) for g in range(n)], axis=0)


def _lru_body(h_ref, g_ref, p_ref, pt_ref, wx_ref, wg_ref, ws5_ref, cw_ref, cb_ref, wr_ref, br_ref, wi_ref, bi_ref,
              lam_ref, o_ref, su_ref, suf_ref, a_scr, b_scr, state_scr, tail_scr, stage_scr):
    l, w = a_scr.shape
    ls = l // SUBLANES
    kconv = cw_ref.shape[0]
    halo = (kconv - 1) * SUBLANES

    @pl.when(pl.program_id(1) == 0)
    def _():
        state_scr[...] = jnp.zeros_like(state_scr)
        tail_scr[...] = jnp.zeros_like(tail_scr)

    u = _rms(h_ref[...], g_ref[...]).astype(BF16)
    su = jnp.dot(u, ws5_ref[...], preferred_element_type=F32)
    su_ref[...] = su
    _fold_chunks(su, stage_scr, suf_ref)
    up = jnp.dot(p_ref[...], u, preferred_element_type=F32).astype(BF16)
    x = jnp.dot(up, wx_ref[...], preferred_element_type=F32)
    gate = jnp.dot(up, wg_ref[...], preferred_element_type=F32)

    cur_tail = x[l - halo:, :]
    xpad = jnp.concatenate([_conv_halo(cur_tail, tail_scr[...]), x], axis=0)
    tail_scr[...] = cur_tail
    xc = cb_ref[...] + sum(cw_ref[k:k + 1, :] * xpad[k * SUBLANES:k * SUBLANES + l] for k in range(kconv))

    r = jax.nn.sigmoid(_mm(xc, wr_ref[...]) + br_ref[...])
    ig = jax.nn.sigmoid(_mm(xc, wi_ref[...]) + bi_ref[...])
    log_a = (-LRU_C * jax.nn.softplus(-lam_ref[...])) * r
    a_scr[...] = jnp.exp(log_a)
    b_scr[...] = jnp.sqrt(-jnp.tanh(log_a) * (jnp.exp(2.0 * log_a) + 1.0)) * (ig * xc)

    def step(i, carry):
        hh, pp = carry
        rows = pl.ds(pl.multiple_of(i * SUBLANES, SUBLANES), SUBLANES)
        a_i = a_scr[rows, :]
        hh = a_i * hh + b_scr[rows, :]
        pp = a_i * pp
        b_scr[rows, :] = hh
        a_scr[rows, :] = pp
        return hh, pp

    h_end, p_end = lax.fori_loop(0, ls, step, (jnp.zeros((SUBLANES, w), F32), jnp.ones((SUBLANES, w), F32)),
                                 unroll=4)
    c = state_scr[...]
    carries = []
    for j in range(SUBLANES):
        carries.append(c)
        c = h_end[j:j + 1, :] + p_end[j:j + 1, :] * c
    state_scr[...] = c
    cmat = jnp.concatenate(carries, axis=0)
    hfull = b_scr[...].reshape(ls, SUBLANES, w) + a_scr[...].reshape(ls, SUBLANES, w) * cmat[None]
    out = (hfull.reshape(l, w) * jax.nn.gelu(gate)).astype(BF16)
    o_ref[...] = jnp.dot(pt_ref[...], out, preferred_element_type=F32).astype(BF16)


def _lru_mixer(h, norm_g, w_x, w_g, w_s5, conv_w, conv_b, w_r, b_r, w_i, b_i, lam, *, bsz, tl):
    w = w_x.shape[1]
    perm = _segment_perm(tl)
    blockdiag = lambda m: jax.scipy.linalg.block_diag(*m).astype(BF16)
    consts = (_row(norm_g), perm, perm.T, w_x, w_g, w_s5, conv_w, _row(conv_b), blockdiag(w_r), _row(b_r),
              blockdiag(w_i), _row(b_i), _row(lam))
    scratch = [pltpu.VMEM((tl, w), F32), pltpu.VMEM((tl, w), F32), pltpu.VMEM((1, w), F32),
               pltpu.VMEM(((conv_w.shape[0] - 1) * SUBLANES, w), F32), pltpu.VMEM((w // LANES, tl, LANES), F32)]
    return _mixer_call(_lru_body, "rglru", h, consts, scratch, bsz=bsz, tl=tl, width=w, s5_input_outs=True)


def _cmul(ar, ai, br, bi):
    return ar * br - ai * bi, ar * bi + ai * br


def _split3(x):
    hi = x.astype(BF16)
    r1 = x - hi.astype(F32)
    mid = r1.astype(BF16)
    lo = (r1 - mid.astype(F32)).astype(BF16)
    return hi, mid, lo


def _mm_exact_rhs(x, e):
    return sum(jnp.dot(part, e, preferred_element_type=F32) for part in _split3(x))


def _bmm_f32(a, b):
    a3, b3 = _split3(a), _split3(b)
    acc = None
    for i, ap in enumerate(a3):
        for j, bp in enumerate(b3):
            if i + j <= 2:
                t = jnp.einsum('gij,gjk->gik', ap, bp, preferred_element_type=F32)
                acc = t if acc is None else acc + t
    return acc


def _place(x, sel):
    return jnp.einsum('gij,gjk->gik', x.astype(BF16), sel, preferred_element_type=F32)


def _lane_sel(shape, target):
    g, r, n = (lax.broadcasted_iota(jnp.int32, shape, k) for k in range(3))
    return (n == target(g, r)).astype(BF16)


def _s5_prep_body(ls_ref, arow_ref, acol_ref, bt_ref, crep_ref, kd_ref, bst_ref, cst_ref, pw_ref):
    gb, _, p = arow_ref.shape
    ch = S5_GROUP_CH
    q = S5_CHUNK
    npw = pw_ref.shape[1]
    step = jnp.exp(ls_ref[...])
    a_re, a_im = arow_ref[:, 0:1, :], arow_ref[:, 1:2, :]
    sa_re, sa_im = a_re * step, a_im * step
    mag = jnp.exp(sa_re)
    ab_re, ab_im = mag * jnp.cos(sa_im), mag * jnp.sin(sa_im)
    den = a_re * a_re + a_im * a_im
    num_re = ab_re - 1.0
    f_re = (num_re * a_re + ab_im * a_im) / den
    f_im = (ab_im * a_re - num_re * a_im) / den
    bt_re, bt_im = bt_ref[:, 0], bt_ref[:, 1]
    bb_re = f_re * bt_re - f_im * bt_im
    bb_im = f_re * bt_im + f_im * bt_re
    ns = (q - 1 - lax.broadcasted_iota(jnp.int32, (1, q, 1), 1)).astype(F32)
    pmag = jnp.exp(ns * sa_re)
    pb_re, pb_im = pmag * jnp.cos(ns * sa_im), pmag * jnp.sin(ns * sa_im)
    pb_re = jnp.broadcast_to(pb_re[:, :, None, :], (gb, q, ch, p)).reshape(gb, q * ch, p)
    pb_im = jnp.broadcast_to(pb_im[:, :, None, :], (gb, q, ch, p)).reshape(gb, q * ch, p)
    bbr = jnp.concatenate([bb_re] * q, axis=1)
    bbi = jnp.concatenate([bb_im] * q, axis=1)
    sr, si = _cmul(bbr, bbi, pb_re, pb_im)
    sin = (_place(sr, _lane_sel((gb, p, 2 * gb * p), lambda g, r: g * p + r))
           + _place(si, _lane_sel((gb, p, 2 * gb * p), lambda g, r: (gb + g) * p + r))).astype(BF16)
    for g in range(gb):
        for s in range(q):
            bst_ref[s * LANES + g * ch:s * LANES + (g + 1) * ch, :] = sin[g, s * ch:(s + 1) * ch, :]
    sc_re, sc_im = acol_ref[:, 0] * step, acol_ref[:, 1] * step
    nd = lax.broadcasted_iota(jnp.int32, (1, 1, q), 2).astype(F32)
    cmag = jnp.exp(sc_re * nd)
    pc_re, pc_im = cmag * jnp.cos(sc_im * nd), cmag * jnp.sin(sc_im * nd)
    expand = (lax.broadcasted_iota(jnp.int32, (q, q * ch), 0)
              == lax.broadcasted_iota(jnp.int32, (q, q * ch), 1) // ch).astype(BF16)
    pc_re = _mm_exact_rhs(pc_re.reshape(gb * p, q), expand).reshape(gb, p, q * ch)
    pc_im = _mm_exact_rhs(pc_im.reshape(gb * p, q), expand).reshape(gb, p, q * ch)
    ca_re, ca_im = _cmul(crep_ref[:, 0], crep_ref[:, 1], pc_re, pc_im)
    kt = _bmm_f32(bb_re, ca_re) - _bmm_f32(bb_im, ca_im)
    for d in range(q):
        sel = _lane_sel((gb, q * ch, LANES), lambda g, r, d=d: jnp.where(r // ch == d, g * ch + r % ch, -1))
        kd_ref[d] = _place(kt, sel).reshape(gb * ch, LANES).astype(BF16)
    abc_mag = jnp.exp(sc_re)
    o_re, o_im = _cmul(ca_re, ca_im, abc_mag * jnp.cos(sc_im), abc_mag * jnp.sin(sc_im))
    sel0 = _lane_sel((1, q * ch, q * LANES), lambda g, r: (r // ch) * LANES + r % ch)[0]
    place0 = lambda x: jnp.dot(x.astype(BF16).reshape(gb * p, q * ch), sel0,
                               preferred_element_type=F32).reshape(gb, p, q * LANES)
    out_re, out_im = place0(o_re), place0(-o_im)
    for g in range(gb):
        cst_ref[g * p:(g + 1) * p, :] = pltpu.roll(out_re[g], g * ch, 1).astype(BF16)
        cst_ref[(gb + g) * p:(gb + g + 1) * p, :] = pltpu.roll(out_im[g], g * ch, 1).astype(BF16)
    nn = jnp.left_shift(1, lax.broadcasted_iota(jnp.int32, (1, npw, 1), 1)).astype(F32) * float(q)
    wmag = jnp.exp(nn * sa_re)
    sel = _lane_sel((gb, p, gb * p), lambda g, r: g * p + r)
    for k, w in enumerate((wmag * jnp.cos(nn * sa_im), wmag * jnp.sin(nn * sa_im))):
        pw_ref[k] = sum(jnp.sum(_place(part, sel), axis=0) for part in _split3(w))


def _s5_prep(log_step, a_re, a_im, b_re, b_im, c_re, c_im):
    g, p = a_re.shape
    ch, q = S5_GROUP_CH, S5_CHUNK
    gb = S5_OCT
    arow = jnp.stack([a_re, a_im], axis=1)
    acol = arow[..., None]
    bt = jnp.stack([b_re, b_im], axis=1).transpose(0, 1, 3, 2)
    crep = jnp.tile(jnp.stack([c_re, c_im], axis=1).transpose(0, 1, 3, 2), (1, 1, 1, q))
    blk = lambda *shape: pl.BlockSpec((gb,) + shape, lambda i: (i,) + (0,) * len(shape))
    out = lambda *shape: pl.BlockSpec((None,) + shape, lambda i: (i,) + (0,) * len(shape))
    nblk, lanes, ns2 = g // gb, q * LANES, 2 * gb * p
    return pl.pallas_call(
        _s5_prep_body,
        out_shape=(jax.ShapeDtypeStruct((nblk, q, LANES, LANES), BF16), jax.ShapeDtypeStruct((nblk, lanes, ns2), BF16),
                   jax.ShapeDtypeStruct((nblk, ns2, lanes), BF16), jax.ShapeDtypeStruct((nblk, 2, S5_NPOW, gb * p), F32)),
        grid=(nblk,),
        in_specs=[blk(1, 1), blk(2, p), blk(2, p, 1), blk(2, ch, p), blk(2, p, q * ch)],
        out_specs=(out(q, LANES, LANES), out(lanes, ns2), out(ns2, lanes), out(2, S5_NPOW, gb * p)),
        compiler_params=_tile_params(),
        name="s5_prep",
    )(log_step.reshape(g, 1, 1), arow, acol, bt, crep)


def _shift_rows(x, sh):
    if sh % SUBLANES == 0:
        return jnp.concatenate([jnp.zeros((sh, x.shape[1]), x.dtype), x[:x.shape[0] - sh]], axis=0)
    row = lax.broadcasted_iota(jnp.int32, x.shape, 0)
    return jnp.where(row >= sh, pltpu.roll(x, sh, 0), 0.0)


def _s5_core_body(x_ref, kd_ref, bst_ref, cst_ref, pw_ref, o_ref, toep_scr):
    rb, lanes = o_ref.shape
    q = S5_CHUNK
    ns = bst_ref.shape[1] // 2

    @pl.when(pl.program_id(1) == 0)
    def _():
        toep_scr[...] = jnp.zeros_like(toep_scr)
        for s in range(q):
            for t in range(s, q):
                toep_scr[s * LANES:(s + 1) * LANES, t * LANES:(t + 1) * LANES] = kd_ref[t - s]

    z = x_ref[...]
    ys = []
    for tp in range(0, q, 2):
        k = (tp + 2) * LANES
        ys.append(jnp.dot(z[:, :k], toep_scr[:k, tp * LANES:(tp + 2) * LANES], preferred_element_type=F32))
    g = jnp.dot(z, bst_ref[...], preferred_element_type=F32)
    e_re, e_im = g[:, :ns], g[:, ns:]
    sh, m = 1, 0
    while sh < rb:
        sr, si = _shift_rows(e_re, sh), _shift_rows(e_im, sh)
        pr, pi = _cmul(pw_ref[0, m:m + 1, :], pw_ref[1, m:m + 1, :], sr, si)
        e_re, e_im = e_re + pr, e_im + pi
        sh, m = sh * 2, m + 1
    s_re = _shift_rows(e_re, 1).astype(BF16)
    s_im = _shift_rows(e_im, 1).astype(BF16)
    o_ref[...] = (jnp.concatenate(ys, axis=1) + jnp.dot(s_re, cst_ref[:ns, :], preferred_element_type=F32)
                  + jnp.dot(s_im, cst_ref[ns:, :], preferred_element_type=F32))


def _s5_core(x, log_step, a_re, a_im, b_re, b_im, c_re, c_im, *, bsz):
    nblk, nrows, lanes = x.shape
    ch, q, p = S5_GROUP_CH, S5_CHUNK, S5_STATE
    o8 = S5_OCT
    rb = nrows // bsz
    kd, bst_o, cst_o, pw_o = _s5_prep(log_step, a_re, a_im, b_re, b_im, c_re, c_im)
    wspec = lambda *shape: pl.BlockSpec((None,) + shape, lambda o, b: (o,) + (0,) * len(shape))
    y = pl.pallas_call(
        _s5_core_body,
        out_shape=jax.ShapeDtypeStruct((nblk, nrows, lanes), F32),
        grid=(nblk, bsz),
        in_specs=[pl.BlockSpec((None, rb, lanes), lambda o, b: (o, b, 0)),
                  wspec(q, LANES, LANES), wspec(lanes, 2 * o8 * p), wspec(2 * o8 * p, lanes),
                  wspec(2, S5_NPOW, o8 * p)],
        out_specs=pl.BlockSpec((None, rb, lanes), lambda o, b: (o, b, 0)),
        scratch_shapes=[pltpu.VMEM((lanes, lanes), BF16)],
        compiler_params=_seq_params(),
        name="s5_core",
    )(x, kd, bst_o, cst_o, pw_o)
    return y


def _causal_conv(x, tail_scr, cw_ref, first):
    l = x.shape[0]
    k = cw_ref.shape[0]

    @pl.when(first)
    def _():
        tail_scr[...] = jnp.zeros_like(tail_scr)

    xpad = jnp.concatenate([tail_scr[...], x], axis=0)
    tail_scr[...] = x[l - SUBLANES:, :]
    acc = cw_ref[k - 1:k, :] * x
    for j in range(k - 1):
        acc = acc + cw_ref[j:j + 1, :] * pltpu.roll(xpad, k - 1 - j, 0)[SUBLANES:]
    return acc


def _split_dot(tri, x):
    hi = x.astype(BF16)
    lo = (x - hi.astype(F32)).astype(BF16)
    return jnp.dot(tri, hi, preferred_element_type=F32) + jnp.dot(tri, lo, preferred_element_type=F32)


def _lower_tri(q):
    return lax.broadcasted_iota(jnp.int32, (q, q), 0) >= lax.broadcasted_iota(jnp.int32, (q, q), 1)


def _pad_lanes(x):
    return jnp.pad(x, [(0, 0)] * (x.ndim - 1) + [(0, LANES - x.shape[-1])])


def _spread_heads(x, first, nheads):
    return jnp.concatenate(
        [jnp.broadcast_to(x[:, first + hd:first + hd + 1], (x.shape[0], LANES)) for hd in range(nheads)], axis=1)


def _decay_matrix(cs, cst, lane, causal):
    diff = cs[:, lane:lane + 1] - cst[lane:lane + 1, :]
    return jnp.where(causal, jnp.exp(jnp.where(causal, diff, 0.0)), 0.0)


def _ssd_body(h_ref, g_ref, wz_ref, wx_ref, wdt_ref, cw_ref, cb_ref, dtb_ref, alog_ref, dsk_ref, ng_ref, o_ref,
              tail_scr, state_scr):
    l, w = o_ref.shape
    ngroups, nstate, gw = state_scr.shape
    q = M2_CHUNK
    first = pl.program_id(1) == 0

    @pl.when(first)
    def _():
        state_scr[...] = jnp.zeros_like(state_scr)

    u = _rms(h_ref[...], g_ref[...]).astype(BF16)
    z = jnp.dot(u, wz_ref[...], preferred_element_type=F32)
    xbc_raw = jnp.dot(u, wx_ref[...], preferred_element_type=F32)
    dt_raw = jnp.dot(u, wdt_ref[...], preferred_element_type=F32)
    xbc = _silu(_causal_conv(xbc_raw, tail_scr, cw_ref, first) + cb_ref[...])
    xs = xbc[:, :w]
    bm = xbc[:, w:w + ngroups * nstate]
    cm = xbc[:, w + ngroups * nstate:]
    dt = jax.nn.softplus(dt_raw + dtb_ref[...])
    da = dt * (-jnp.exp(alog_ref[...]))
    xdt = xs * dt

    causal = _lower_tri(q)
    tri = causal.astype(BF16)
    head_of_lane = lax.broadcasted_iota(jnp.int32, (1, gw), 1) // M2_HEAD_DIM

    states = [state_scr[g] for g in range(ngroups)]
    for c in range(l // q):
        rows = slice(c * q, (c + 1) * q)
        cs = _split_dot(tri, da[rows])
        cs_last = cs[q - 1:q, :]
        dte = jnp.exp(cs_last - cs)
        ecs = jnp.exp(cs)
        cdec = jnp.exp(cs_last)
        cst = cs.T
        ychunk = []
        for g in range(ngroups):
            gl = slice(g * gw, (g + 1) * gw)
            bg = bm[rows, g * nstate:(g + 1) * nstate]
            cg = cm[rows, g * nstate:(g + 1) * nstate]
            cb = _mm_nt(cg, bg)
            xg = xdt[rows, gl]
            yg = _mm(cg, states[g]) * ecs[:, gl]
            for hh in range(gw // M2_HEAD_DIM):
                dec = _decay_matrix(cs, cst, g * gw + hh * M2_HEAD_DIM, causal)
                yg = yg + _mm(cb * dec, jnp.where(head_of_lane == hh, xg, 0.0))
            states[g] = states[g] * cdec[:, gl] + _mm(bg.T, xg * dte[:, gl])
            ychunk.append(yg)
        y = jnp.concatenate(ychunk, axis=1) + dsk_ref[...] * xs[rows]
        y = y * _silu(z[rows])
        o_ref[rows, :] = _rms(y, ng_ref[...]).astype(BF16)
    for g in range(ngroups):
        state_scr[g] = states[g]


def _ssd_mixer(h, norm_g, w_z, w_xbc, w_dt, conv_w, conv_b, dt_bias, a_log, d_skip, out_norm_g, *, bsz, tl):
    w = w_z.shape[1]
    cdim = w_xbc.shape[1]
    rep = lambda v: jnp.repeat(v, M2_HEAD_DIM).reshape(1, w)
    consts = (_row(norm_g), w_z, w_xbc, jnp.repeat(w_dt, M2_HEAD_DIM, axis=1), conv_w, _row(conv_b), rep(dt_bias),
              rep(a_log), rep(d_skip), _row(out_norm_g))
    scratch = [pltpu.VMEM((SUBLANES, cdim), F32),
               pltpu.VMEM((M2_GROUPS, M2_STATE, w // M2_GROUPS), F32)]
    return _mixer_call(_ssd_body, "ssd", h, consts, scratch, bsz=bsz, tl=tl, width=w)


def _l2norm(x):
    return x * lax.rsqrt(jnp.sum(x * x, axis=-1, keepdims=True) + EPS)


def _gdn_body(h_ref, g_ref, wqkv_ref, wg_ref, wba_ref, cw_ref, dtb_ref, alog_ref, ng_ref, o_ref,
              tail_scr, state_scr):
    nb, lt, w = o_ref.shape
    _, nheads, dk, dv = state_scr.shape
    l = nb * lt
    q = GDN_CHUNK
    sq = nheads * q
    nchunks = l // q
    cpb = lt // q
    first = pl.program_id(0) == 0

    @pl.when(first)
    def _():
        state_scr[...] = jnp.zeros_like(state_scr)

    u = _rms(h_ref[...].reshape(l, h_ref.shape[-1]), g_ref[...]).astype(BF16)
    qkv_raw = jnp.dot(u, wqkv_ref[...], preferred_element_type=F32)
    gate = jnp.dot(u, wg_ref[...], preferred_element_type=F32)
    ba = jnp.dot(u, wba_ref[...], preferred_element_type=F32)
    beta = _spread_heads(jax.nn.sigmoid(ba), 0, nheads)
    glog = _spread_heads(-jnp.exp(alog_ref[...]) * jax.nn.softplus(ba + dtb_ref[...]), nheads, nheads)
    qkv = jnp.concatenate([_silu(_causal_conv(qkv_raw[b * lt:(b + 1) * lt], tail_scr.at[b], cw_ref, first))
                           for b in range(nb)], axis=0)

    tri = _lower_tri(q).astype(BF16)
    gcsc = [_split_dot(tri, glog[c * q:(c + 1) * q]) for c in range(nchunks)]
    gcs = jnp.concatenate(gcsc, axis=0)
    gtot = jnp.concatenate([jnp.broadcast_to(g_[q - 1:q, :], (q, w)) for g_ in gcsc], axis=0)
    egcs = jnp.exp(gcs)

    def stack(x, c):
        return jnp.concatenate([x[c * q:(c + 1) * q, hd * dk:(hd + 1) * dk] for hd in range(nheads)], axis=0)

    qn = jnp.concatenate([_l2norm(qkv[:, hd * dk:(hd + 1) * dk]) * (dk ** -0.5) for hd in range(nheads)], axis=1)
    kn = jnp.concatenate([_l2norm(qkv[:, w + hd * dk:w + (hd + 1) * dk]) for hd in range(nheads)], axis=1)
    kb = kn * beta
    qd = qn * egcs
    kd = kn * jnp.exp(gtot - gcs)
    kbe = kb * egcs
    vb = qkv[:, 2 * w:] * beta

    rs = lax.broadcasted_iota(jnp.int32, (sq, sq), 0)
    cs = lax.broadcasted_iota(jnp.int32, (sq, sq), 1)
    same_head = (rs // q) == (cs // q)
    causal = same_head & (rs >= cs)
    strict = same_head & (rs > cs)
    eye = (rs == cs).astype(F32)
    head_block = (lax.broadcasted_iota(jnp.int32, (sq, nheads * dv), 0) // q
                  == lax.broadcasted_iota(jnp.int32, (sq, nheads * dv), 1) // dv)

    sol, qk, kd_t = [None] * nchunks, [None] * nchunks, [None] * nchunks
    for c0 in range(0, nchunks, GDN_INTERLEAVE):
        cg = list(range(c0, min(c0 + GDN_INTERLEAVE, nchunks)))
        k_s = [stack(kn, c) for c in cg]
        g_s = [stack(gcs, c) for c in cg]
        g_t = jnp.concatenate(g_s, axis=0).T
        decay = []
        for i in range(len(cg)):
            diff = jnp.concatenate([g_s[i]] * (sq // dk), axis=1) - g_t[0:1, i * sq:(i + 1) * sq]
            decay.append(jnp.where(causal, jnp.exp(jnp.where(causal, diff, 0.0)), 0.0))
        kk = [_mm_nt(stack(kb, c), k_s[i]) for i, c in enumerate(cg)]
        for i, c in enumerate(cg):
            qk[c] = _mm_nt(stack(qn, c), k_s[i]) * decay[i]
        pw = [jnp.where(strict, -(kk[i] * decay[i]), 0.0) for i in range(len(cg))]
        inv = [eye + p_ for p_ in pw]
        span = 2
        while span < q:
            pw = [_mm(p_, p_) for p_ in pw]
            inv = [a_ + _mm(a_, p_) for a_, p_ in zip(inv, pw)]
            span *= 2
        kdt = jnp.concatenate([stack(kd, c) for c in cg], axis=0).T
        for i, c in enumerate(cg):
            sol[c] = _mm(inv[i], jnp.concatenate([stack(vb, c), stack(kbe, c)], axis=1))
            kd_t[c] = kdt[:, i * sq:(i + 1) * sq]

    states = [[state_scr[b, hd] for hd in range(nheads)] for b in range(nb)]
    bs = range(nb)
    for cl in range(cpb):
        csb = [b * cpb + cl for b in bs]
        qd_s = [stack(qd, c) for c in csb]
        r = [[_mm(jnp.concatenate([sol[c][hd * q:(hd + 1) * q, dv:], qd_s[b][hd * q:(hd + 1) * q]], axis=0),
                  states[b][hd]) for hd in range(nheads)] for b, c in zip(bs, csb)]
        v_new = [jnp.concatenate([sol[c][hd * q:(hd + 1) * q, :dv] - r[b][hd][:q] for hd in range(nheads)], axis=0)
                 for b, c in zip(bs, csb)]
        upd = [_mm(kd_t[c], jnp.where(head_block, jnp.concatenate([v_new[b]] * nheads, axis=1), 0.0))
               for b, c in zip(bs, csb)]
        o_s = [jnp.concatenate([r[b][hd][q:] for hd in range(nheads)], axis=0) + _mm(qk[c], v_new[b])
               for b, c in zip(bs, csb)]
        for b, c in zip(bs, csb):
            cdec = jnp.exp(gtot[c * q:c * q + 1, :])
            for hd in range(nheads):
                hl = slice(hd * dk, (hd + 1) * dk)
                states[b][hd] = states[b][hd] * cdec[:, hl] + upd[b][:, hl]
                o_ref[b, cl * q:(cl + 1) * q, hl] = (
                    _rms(o_s[b][hd * q:(hd + 1) * q], ng_ref[...]) * _silu(gate[c * q:(c + 1) * q, hl])).astype(BF16)
    for b in bs:
        for hd in range(nheads):
            state_scr[b, hd] = states[b][hd]


def _gdn_mixer(h, norm_g, w_qkv, w_g, w_beta, w_a, conv_w, dt_bias, a_log, out_norm_g, *, bsz, tl):
    t, d = h.shape
    s = t // bsz
    w = w_g.shape[1]
    cdim = w_qkv.shape[1]
    after_beta = lambda v: _pad_lanes(jnp.concatenate([jnp.zeros_like(v), v]).reshape(1, -1))
    consts = (_row(norm_g), w_qkv, w_g, _pad_lanes(jnp.concatenate([w_beta, w_a], axis=1)), conv_w,
              after_beta(dt_bias), after_beta(a_log), _row(out_norm_g))
    out = pl.pallas_call(
        _gdn_body,
        out_shape=jax.ShapeDtypeStruct((bsz, s, w), BF16),
        grid=(s // tl,),
        in_specs=[pl.BlockSpec((bsz, tl, d), lambda i: (0, i, 0))] + [_const_spec(c.shape) for c in consts],
        out_specs=pl.BlockSpec((bsz, tl, w), lambda i: (0, i, 0)),
        scratch_shapes=[pltpu.VMEM((bsz, SUBLANES, cdim), F32),
                        pltpu.VMEM((bsz, w // GDN_HEAD_DIM, GDN_HEAD_DIM, GDN_HEAD_DIM), F32)],
        compiler_params=pltpu.CompilerParams(dimension_semantics=("arbitrary",), vmem_limit_bytes=VMEM_LIMIT_BYTES),
        name="gdn",
    )(h.reshape(bsz, s, d), *consts)
    return out.reshape(t, w)


def _col_slices(sizes):
    out, start = [], 0
    for size in sizes:
        out.append(slice(start, start + size))
        start += size
    return out


def kernel(x, p, ffn1_norm, ffn1_w_in, ffn1_w_out, mix_norm, w_in, w_gate, b_gate, s5_log_step, s5_a_re, s5_a_im, s5_b_re, s5_b_im, s5_c_re, s5_c_im, s5_d, s5_w_glu, s5_b_glu, lru_conv_w, lru_conv_b, lru_w_r, lru_b_r, lru_w_i, lru_b_i, lru_lambda, m2_conv_w, m2_conv_b, m2_dt_bias, m2_a_log, m2_d, m2_norm, gdn_conv_w, gdn_dt_bias, gdn_a_log, gdn_norm, w_branch, w_out, ffn2_norm, ffn2_w_in, ffn2_w_out, ple_norm, ple_w_gate, ple_w_proj, final_norm):
    bsz, s, d = x.shape
    depth = p.shape[0]
    t = bsz * s
    h = x.reshape(t, d)
    p = p.reshape(depth, t, -1)
    tm, tl, tm_ffn = 512, 512, 1024
    bf = lambda a: a.astype(BF16)
    ffn1_w_in, ffn1_w_out, ffn2_w_in, ffn2_w_out = bf(ffn1_w_in), bf(ffn1_w_out), bf(ffn2_w_in), bf(ffn2_w_out)
    w_gate, w_branch, w_out, s5_w_glu = bf(w_gate), bf(w_branch), bf(w_out), bf(s5_w_glu)
    ple_w_gate, ple_w_proj = bf(ple_w_gate), bf(ple_w_proj)
    for i in range(depth):
        h = _ffn(h, ffn1_norm[i], ffn1_w_in, ffn1_w_out, layer=i, tm=tm_ffn)
        (w_s5, w_lx, w_lg, w_mz, w_mx, w_mdt, w_gqkv, w_gg, w_gb, w_ga) = [bf(w_in[i][:, c]) for c in _col_slices(IN_SPLITS)]
        y_b, su, su_folded = _lru_mixer(h, mix_norm[i], w_lx, w_lg, w_s5, lru_conv_w[i], lru_conv_b[i], lru_w_r[i],
                                        lru_b_r[i], lru_w_i[i], lru_b_i[i], lru_lambda[i], bsz=bsz, tl=tl)
        y_s5 = _s5_core(su_folded, s5_log_step[i], s5_a_re[i], s5_a_im[i], s5_b_re[i], s5_b_im[i], s5_c_re[i],
                        s5_c_im[i], bsz=bsz)
        y_c = _ssd_mixer(h, mix_norm[i], w_mz, w_mx, w_mdt, m2_conv_w[i], m2_conv_b[i], m2_dt_bias[i],
                         m2_a_log[i], m2_d[i], m2_norm[i], bsz=bsz, tl=tl)
        y_d = _gdn_mixer(h, mix_norm[i], w_gqkv, w_gg, w_gb, w_ga, gdn_conv_w[i], gdn_dt_bias[i], gdn_a_log[i],
                         gdn_norm[i], bsz=bsz, tl=tl // bsz)
        h = _merge(h, mix_norm[i], y_s5, su, s5_d[i], s5_w_glu, s5_b_glu[i], (y_b, y_c, y_d), w_gate, b_gate[i],
                   w_branch, w_out, layer=i, tm=tm)
        h = _ffn(h, ffn2_norm[i], ffn2_w_in, ffn2_w_out, layer=i, tm=tm_ffn,
                 ple=(ple_norm[i], p, ple_w_gate, ple_w_proj, final_norm), final_norm=(i == depth - 1))
    return h.reshape(bsz, s, d)
```
